```python
import math
import jax, jax.numpy as jnp
from jax import lax
import numpy as np

D_MODEL = 1024
BATCH = 1
SEQ = 16384
DEPTH = 1
DEC_BATCH = 16
DEC_SEQ = 16
PAST_LEN = 4096

CHUNK = 64
QBLOCK = 128
N_HEADS = 8
QK_NOPE = 64
QK_ROPE = 32
V_DIM = 64
Q_LORA = 256
KV_LORA = 256
ROPE_THETA = 10000.0
MLA_WIDTH = N_HEADS * V_DIM
ATTN_SCALE = 1.0 / math.sqrt(QK_NOPE + QK_ROPE)
POOL_WINDOWS = (2, 4, 8, 16)
N_POOL_GROUPS = 4
POOL_WIDTH = D_MODEL - MLA_WIDTH
POOL_GROUP_W = POOL_WIDTH // N_POOL_GROUPS
POOL_HIST = max(POOL_WINDOWS) - 1
IN_WIDTH = Q_LORA + KV_LORA + QK_ROPE + POOL_WIDTH
D_FF = 2816
CONV_W = 3
PLE_DIM = 256
ALPHA = (2 * DEPTH) ** 0.25
BETA = (8 * DEPTH) ** -0.25
LN_EPS = 1e-5
RMS_EPS = 1e-6
NEG = -1e30

kernel_name = "mla_pool_convffn_deepnorm_stream_step"


def rms_norm(x, g):
    xf = x.astype(jnp.float32)
    y = xf * lax.rsqrt(jnp.mean(xf * xf, axis=-1, keepdims=True) + RMS_EPS)
    return (y * g.astype(jnp.float32)).astype(x.dtype)


def layer_norm(x, g, b):
    xf = x.astype(jnp.float32)
    mu = jnp.mean(xf, axis=-1, keepdims=True)
    var = jnp.mean(jnp.square(xf - mu), axis=-1, keepdims=True)
    y = (xf - mu) * lax.rsqrt(var + LN_EPS)
    return (y * g.astype(jnp.float32) + b.astype(jnp.float32)).astype(x.dtype)


def rope_tables(pos):
    inv = 1.0 / (ROPE_THETA ** (jnp.arange(0, QK_ROPE, 2, dtype=jnp.float32) / QK_ROPE))
    ang = pos.astype(jnp.float32)[:, None] * inv[None, :]
    return jnp.cos(ang), jnp.sin(ang)


def apply_rope(x, cos, sin):
    half = x.shape[-1] // 2
    x1, x2 = x[..., :half], x[..., half:]
    c, s = cos.astype(x.dtype), sin.astype(x.dtype)
    return jnp.concatenate([x1 * c - x2 * s, x2 * c + x1 * s], axis=-1)


def prompt_attention(q, k, v, pos):
    B, S, H, Dk = q.shape
    nb = S // QBLOCK
    qb = q.reshape(B, nb, QBLOCK, H, Dk).transpose(1, 0, 2, 3, 4)
    k_chunk = pos // CHUNK
    q_chunk = (pos // CHUNK).reshape(nb, QBLOCK)

    def one_block(args):
        qi, qc = args
        s = jnp.einsum('bqhd,bkhd->bhqk', qi, k).astype(jnp.float32) * ATTN_SCALE
        mask = k_chunk[None, :] <= qc[:, None]
        s = jnp.where(mask[None, None], s, NEG)
        p = jax.nn.softmax(s, axis=-1).astype(v.dtype)
        return jnp.einsum('bhqk,bkhd->bqhd', p, v)

    o = lax.map(one_block, (qb, q_chunk))
    return o.transpose(1, 0, 2, 3, 4).reshape(B, S, H, V_DIM)


def sample_attention(q_nope, q_rope, c_all, kr_all, w_kv_b, q_pos):
    L = c_all.shape[1]
    w = w_kv_b.reshape(KV_LORA, N_HEADS, QK_NOPE + V_DIM)
    w_uk, w_uv = w[..., :QK_NOPE], w[..., QK_NOPE:]
    q_lat = jnp.einsum('bqhn,lhn->bqhl', q_nope, w_uk)
    s = (jnp.einsum('bqhl,bkl->bhqk', q_lat, c_all)
         + jnp.einsum('bqhr,bkr->bhqk', q_rope, kr_all)).astype(jnp.float32) * ATTN_SCALE
    mask = (jnp.arange(L) // CHUNK)[None, :] <= (q_pos // CHUNK)[:, None]
    s = jnp.where(mask[None, None], s, NEG)
    p = jax.nn.softmax(s, axis=-1).astype(c_all.dtype)
    o_lat = jnp.einsum('bhqk,bkl->bqhl', p, c_all)
    return jnp.einsum('bqhl,lhv->bqhv', o_lat, w_uv)


def pool_mix(u_ext, n_hist, pos0, w_pool, s_pool):
    B, L, _ = u_ext.shape
    T = L - n_hist
    G, C = N_POOL_GROUPS, POOL_GROUP_W
    ug = u_ext.astype(jnp.float32).reshape(B, L, G, C)
    cs = jnp.concatenate([jnp.zeros((B, 1, G, C), jnp.float32), jnp.cumsum(ug, axis=1)], axis=1)
    win = np.array(POOL_WINDOWS, dtype=np.int32)
    j = np.arange(T, dtype=np.int32)[:, None]
    hi = np.broadcast_to(n_hist + j + 1, (T, G))
    lo = np.maximum(hi - win[None, :], 0)
    cnt = np.minimum(win[None, :], pos0 + j + 1).astype(np.float32)
    gi = np.arange(G)[None, :]
    win_sum = cs[:, hi, gi] - cs[:, lo, gi]
    d = win_sum / jnp.asarray(cnt)[None, :, :, None] - ug[:, n_hist:]
    y = jnp.einsum('btgc,gcd->btgd', d.astype(u_ext.dtype), w_pool)
    return y.reshape(B, T, POOL_WIDTH) * s_pool


def conv_ffn(h, conv_hist, w_up, w_dw, b_dw, w_down):
    up = h @ w_up
    T = up.shape[1]
    ext = jnp.concatenate([conv_hist.astype(up.dtype), up], axis=1)
    c = ext[:, 0:T] * w_dw[0] + ext[:, 1:T + 1] * w_dw[1] + ext[:, 2:T + 2] * w_dw[2] + b_dw
    a, b = jnp.split(c, 2, axis=-1)
    return (jax.nn.silu(a) * b) @ w_down, ext[:, -(CONV_W - 1):]


def layer(x, pe, pos0, ckv_hist, kr_hist, pool_hist, conv_hist, lp):
    (w_in, g_q, w_q_b, g_kv, w_kv_b, w_pool, s_pool, w_o, ln1_g, ln1_b,
     w_up, w_dw, b_dw, w_down, w_pg, w_pe, ln2_g, ln2_b) = lp
    B, T, _ = x.shape
    pos = pos0 + jnp.arange(T, dtype=jnp.int32)
    cos, sin = rope_tables(pos)
    z = x @ w_in
    q_a, c_raw, kr_raw, u = jnp.split(z, [Q_LORA, Q_LORA + KV_LORA, Q_LORA + KV_LORA + QK_ROPE], axis=-1)
    q = (rms_norm(q_a, g_q) @ w_q_b).reshape(B, T, N_HEADS, QK_NOPE + QK_ROPE)
    q_nope = q[..., :QK_NOPE]
    q_rope = apply_rope(q[..., QK_NOPE:], cos[:, None, :], sin[:, None, :])
    c = rms_norm(c_raw, g_kv)
    kr = apply_rope(kr_raw, cos, sin)
    if ckv_hist is None:
        kv = (c @ w_kv_b).reshape(B, T, N_HEADS, QK_NOPE + V_DIM)
        k = jnp.concatenate([kv[..., :QK_NOPE],
                             jnp.broadcast_to(kr[:, :, None, :], (B, T, N_HEADS, QK_ROPE))], axis=-1)
        qf = jnp.concatenate([q_nope, q_rope], axis=-1)
        o = prompt_attention(qf, k, kv[..., QK_NOPE:], pos)
        u_ext, n_hist = u, 0
    else:
        c_all = jnp.concatenate([ckv_hist.astype(c.dtype), c], axis=1)
        kr_all = jnp.concatenate([kr_hist.astype(kr.dtype), kr], axis=1)
        o = sample_attention(q_nope, q_rope, c_all, kr_all, w_kv_b, pos)
        u_ext, n_hist = jnp.concatenate([pool_hist.astype(u.dtype), u], axis=1), POOL_HIST
    pool_out = pool_mix(u_ext, n_hist, pos0, w_pool, s_pool)
    new_pool = u_ext[:, -POOL_HIST:]
    mix = jnp.concatenate([o.reshape(B, T, MLA_WIDTH), pool_out], axis=-1) @ w_o
    x1 = layer_norm(ALPHA * x + mix, ln1_g, ln1_b)
    ffn, new_conv = conv_ffn(x1, conv_hist, w_up, w_dw, b_dw, w_down)
    ple = jax.nn.sigmoid(x1 @ w_pg) * (pe @ w_pe)
    x2 = layer_norm(ALPHA * x1 + ffn + ple, ln2_g, ln2_b)
    return x2, c, kr, new_pool, new_conv


def setup_inputs(seed: int = 0) -> dict:
    key = jax.random.key(seed)
    ks = jax.random.split(key, 32)
    f32 = jnp.float32
    nrm = lambda k, shape, std: jax.random.normal(k, shape, f32) * std
    return {
        "x_prompt": nrm(ks[0], (BATCH, SEQ, D_MODEL), 1.0),
        "x_sample": nrm(ks[1], (DEC_BATCH, DEC_SEQ, D_MODEL), 1.0),
        "cache_ckv": nrm(ks[2], (DEPTH, DEC_BATCH, PAST_LEN, KV_LORA), 1.0),
        "cache_krope": nrm(ks[3], (DEPTH, DEC_BATCH, PAST_LEN, QK_ROPE), 1.0),
        "state_pool": nrm(ks[4], (DEPTH, DEC_BATCH, POOL_HIST, POOL_WIDTH), 1.0),
        "state_ffn_conv": nrm(ks[5], (DEPTH, DEC_BATCH, CONV_W - 1, 2 * D_FF), 1.0),
        "p_prompt": nrm(ks[6], (DEPTH, BATCH, SEQ, PLE_DIM), 1.0),
        "p_sample": nrm(ks[7], (DEPTH, DEC_BATCH, DEC_SEQ, PLE_DIM), 1.0),
        "w_in": nrm(ks[8], (DEPTH, D_MODEL, IN_WIDTH), D_MODEL ** -0.5),
        "g_q": 1.0 + nrm(ks[9], (DEPTH, Q_LORA), 0.02),
        "w_q_b": nrm(ks[10], (DEPTH, Q_LORA, N_HEADS * (QK_NOPE + QK_ROPE)), Q_LORA ** -0.5),
        "g_kv": 1.0 + nrm(ks[11], (DEPTH, KV_LORA), 0.02),
        "w_kv_b": nrm(ks[12], (DEPTH, KV_LORA, N_HEADS * (QK_NOPE + V_DIM)), KV_LORA ** -0.5),
        "w_pool": nrm(ks[13], (DEPTH, N_POOL_GROUPS, POOL_GROUP_W, POOL_GROUP_W), POOL_GROUP_W ** -0.5),
        "s_pool": 1.0 + nrm(ks[14], (DEPTH, POOL_WIDTH), 0.02),
        "w_o": nrm(ks[15], (DEPTH, D_MODEL, D_MODEL), BETA * D_MODEL ** -0.5),
        "ln1_g": 1.0 + nrm(ks[16], (DEPTH, D_MODEL), 0.02),
        "ln1_b": nrm(ks[17], (DEPTH, D_MODEL), 0.02),
        "w_up": nrm(ks[18], (DEPTH, D_MODEL, 2 * D_FF), D_MODEL ** -0.5),
        "w_dw": nrm(ks[19], (DEPTH, CONV_W, 2 * D_FF), CONV_W ** -0.5),
        "b_dw": nrm(ks[20], (DEPTH, 2 * D_FF), 0.02),
        "w_down": nrm(ks[21], (DEPTH, D_FF, D_MODEL), BETA * D_FF ** -0.5),
        "w_pg": nrm(ks[22], (DEPTH, D_MODEL, D_MODEL), D_MODEL ** -0.5),
        "w_pe": nrm(ks[23], (DEPTH, PLE_DIM, D_MODEL), BETA * PLE_DIM ** -0.5),
        "ln2_g": 1.0 + nrm(ks[24], (DEPTH, D_MODEL), 0.02),
        "ln2_b": nrm(ks[25], (DEPTH, D_MODEL), 0.02),
    }


def reference(x_prompt, x_sample, cache_ckv, cache_krope, state_pool, state_ffn_conv, p_prompt, p_sample,
              w_in, g_q, w_q_b, g_kv, w_kv_b, w_pool, s_pool, w_o, ln1_g, ln1_b,
              w_up, w_dw, b_dw, w_down, w_pg, w_pe, ln2_g, ln2_b):
    n_past = cache_ckv.shape[2]
    hp, hs = x_prompt, x_sample
    ckv_p, kr_p, pool_p, conv_p = [], [], [], []
    ckv_s, kr_s, pool_s, conv_s = [], [], [], []
    for i in range(DEPTH):
        lp = (w_in[i], g_q[i], w_q_b[i], g_kv[i], w_kv_b[i], w_pool[i], s_pool[i], w_o[i],
              ln1_g[i], ln1_b[i], w_up[i], w_dw[i], b_dw[i], w_down[i], w_pg[i], w_pe[i],
              ln2_g[i], ln2_b[i])
        zero_conv = jnp.zeros((hp.shape[0], CONV_W - 1, 2 * D_FF), hp.dtype)
        hp, c1, k1, pl1, cv1 = layer(hp, p_prompt[i], 0, None, None, None, zero_conv, lp)
        hs, c2, k2, pl2, cv2 = layer(hs, p_sample[i], n_past, cache_ckv[i], cache_krope[i],
                                     state_pool[i], state_ffn_conv[i], lp)
        ckv_p.append(c1); kr_p.append(k1); pool_p.append(pl1); conv_p.append(cv1)
        ckv_s.append(c2); kr_s.append(k2); pool_s.append(pl2); conv_s.append(cv2)
    return (hp, hs,
            jnp.stack(ckv_p), jnp.stack(kr_p), jnp.stack(pool_p), jnp.stack(conv_p),
            jnp.stack(ckv_s), jnp.stack(kr_s), jnp.stack(pool_s), jnp.stack(conv_s))
```

```python
import functools
import math

import jax
import jax.numpy as jnp
from jax import lax
from jax.experimental import pallas as pl
from jax.experimental.pallas import tpu as pltpu

F32 = jnp.float32
BF16 = jnp.bfloat16

D_MODEL = 1024
CHUNK = 64
N_HEADS = 8
QK_NOPE = 64
QK_ROPE = 32
V_DIM = 64
Q_LORA = 256
KV_LORA = 256
ROPE_THETA = 10000.0
MLA_WIDTH = N_HEADS * V_DIM
POOL_WINDOWS = (2, 4, 8, 16)
POOL_GROUP_W = 128
POOL_WIDTH = D_MODEL - MLA_WIDTH
POOL_HIST = max(POOL_WINDOWS) - 1
D_FF = 2816
CONV_W = 3
PLE_DIM = 256
DEPTH = 1
ALPHA = (2 * DEPTH) ** 0.25
LN_EPS = 1e-5
RMS_EPS = 1e-6
NEG = -1e30
ATTN_SCALE = 1.0 / math.sqrt(QK_NOPE + QK_ROPE)

LANES = 128
HEAD_PAD = 128
ROPE_LO = QK_NOPE
ROPE_MID = QK_NOPE + QK_ROPE // 2
ROPE_HI = QK_NOPE + QK_ROPE
HALO = 16
CONV_HALO = 8
FF_CHUNK = 256
Q_SCALE = ATTN_SCALE * math.log2(math.e)
VMEM_LIMIT = 56 * 1024 * 1024

NT_DIMS = (((1,), (1,)), ((), ()))


def _rms(x, g):
    return x * lax.rsqrt(jnp.mean(x * x, axis=-1, keepdims=True) + RMS_EPS) * g


def _layer_norm(x, g, b):
    mu = jnp.mean(x, axis=-1, keepdims=True)
    xc = x - mu
    var = jnp.mean(xc * xc, axis=-1, keepdims=True)
    return xc * lax.rsqrt(var + LN_EPS) * g + b


def _token_rope_tables(pos_col, inv_lane):
    ang = pos_col.astype(F32) * inv_lane
    lane = lax.broadcasted_iota(jnp.int32, ang.shape, 1)
    first = lane < ROPE_MID
    sin = jnp.sin(ang)
    return jnp.cos(ang), jnp.where(first, -sin, sin), first


def _token_rope(x, cos, sin_signed, first):
    partner = jnp.where(first, pltpu.roll(x, LANES - QK_ROPE // 2, 1), pltpu.roll(x, QK_ROPE // 2, 1))
    return x * cos + partner * sin_signed


def _proj_prompt_kernel(x_ref, win_ref, gq_ref, gkv_ref, wqt_ref, wuk_ref, wuvt_ref, invl_ref, invs_ref,
                        qt_ref, k_ref, vt_ref, c_ref, kr_ref, u_ref, *, tm, tk, pos0):
    i = pl.program_id(0)
    z = jnp.dot(x_ref[...].astype(BF16), win_ref[...], preferred_element_type=F32)
    q_a = z[:, 0:Q_LORA]
    c_raw = z[:, Q_LORA:Q_LORA + KV_LORA]
    kr_pad = z[:, 512:640]
    u_ref[...] = z[:, 640:640 + POOL_WIDTH]

    qn = _rms(q_a, gq_ref[...]).astype(BF16)
    c = _rms(c_raw, gkv_ref[...])
    c_ref[...] = c
    cb = c.astype(BF16)

    pos_col = pos0 + i * tm + lax.broadcasted_iota(jnp.int32, (tm, LANES), 0)
    cos, sin_signed, first = _token_rope_tables(pos_col, invl_ref[...])
    kr_rot = _token_rope(kr_pad, cos, sin_signed, first)
    kr_ref[...] = kr_rot[:, ROPE_LO:ROPE_HI]

    k_nope = jnp.dot(cb, wuk_ref[...], preferred_element_type=F32)
    for h in range(N_HEADS):
        sl = slice(h * HEAD_PAD, (h + 1) * HEAD_PAD)
        k_ref[:, sl] = (k_nope[:, sl] + kr_rot).astype(BF16)

    vt = lax.dot_general(wuvt_ref[...], cb, NT_DIMS, preferred_element_type=F32)
    for s in range(tm // tk):
        vt_ref[s] = vt[:, s * tk:(s + 1) * tk].astype(BF16)

    qt = lax.dot_general(wqt_ref[...], qn, NT_DIMS, preferred_element_type=F32)
    pos_row = pos0 + i * tm + lax.broadcasted_iota(jnp.int32, (QK_ROPE // 2, tm), 1)
    ang = pos_row.astype(F32) * invs_ref[...]
    cos_t, sin_t = jnp.cos(ang), jnp.sin(ang)
    for h in range(N_HEADS):
        b0 = h * HEAD_PAD
        qt_ref[b0:b0 + ROPE_LO, :] = (qt[b0:b0 + ROPE_LO] * Q_SCALE).astype(BF16)
        x1 = qt[b0 + ROPE_LO:b0 + ROPE_MID]
        x2 = qt[b0 + ROPE_MID:b0 + ROPE_HI]
        qt_ref[b0 + ROPE_LO:b0 + ROPE_MID, :] = ((x1 * cos_t - x2 * sin_t) * Q_SCALE).astype(BF16)
        qt_ref[b0 + ROPE_MID:b0 + ROPE_HI, :] = ((x2 * cos_t + x1 * sin_t) * Q_SCALE).astype(BF16)
        qt_ref[b0 + ROPE_HI:b0 + HEAD_PAD, :] = jnp.zeros((HEAD_PAD - ROPE_HI, tm), BF16)


def _proj_sample_kernel(x_ref, win_ref, gq_ref, gkv_ref, wq_ref, invl_ref,
                        q_ref, c_ref, kr_ref, u_ref, *, tm, seq_len, pos0):
    z = jnp.dot(x_ref[...].astype(BF16), win_ref[...], preferred_element_type=F32)
    q_a = z[:, 0:Q_LORA]
    c_raw = z[:, Q_LORA:Q_LORA + KV_LORA]
    kr_pad = z[:, 512:640]
    u_ref[...] = z[:, 640:640 + POOL_WIDTH]

    qn = _rms(q_a, gq_ref[...]).astype(BF16)
    c_ref[...] = _rms(c_raw, gkv_ref[...])

    row = lax.broadcasted_iota(jnp.int32, (tm, LANES), 0)
    pos_col = pos0 + (row & (seq_len - 1))
    cos, sin_signed, first = _token_rope_tables(pos_col, invl_ref[...])
    kr_ref[...] = _token_rope(kr_pad, cos, sin_signed, first)[:, ROPE_LO:ROPE_HI]

    q = jnp.dot(qn, wq_ref[...], preferred_element_type=F32)
    for h in range(N_HEADS):
        sl = slice(h * HEAD_PAD, (h + 1) * HEAD_PAD)
        q_ref[:, sl] = _token_rope(q[:, sl], cos, sin_signed, first) * Q_SCALE


def _attn_prompt_kernel(qt_ref, k_ref, vt_ref, o_ref, *, tq, tk):
    i = pl.program_id(1)
    qt = qt_ref[...]
    ratio = tq // tk

    def step(j, carry, masked):
        m, l, acc = carry
        s = jnp.dot(k_ref[j], qt, preferred_element_type=F32)
        if masked:
            kpos = j * tk + lax.broadcasted_iota(jnp.int32, (tk, tq), 0)
            qpos = i * tq + lax.broadcasted_iota(jnp.int32, (tk, tq), 1)
            s = jnp.where((kpos // CHUNK) <= (qpos // CHUNK), s, NEG)
        m_new = jnp.maximum(m, jnp.max(s, axis=0, keepdims=True))
        alpha = jnp.exp2(m - m_new)
        p = jnp.exp2(s - m_new)
        l = alpha * l + jnp.sum(p, axis=0, keepdims=True)
        acc = alpha * acc + jnp.dot(vt_ref[j], p.astype(BF16), preferred_element_type=F32)
        return m_new, l, acc

    carry = (jnp.full((1, tq), NEG, F32), jnp.zeros((1, tq), F32), jnp.zeros((V_DIM, tq), F32))
    carry = lax.fori_loop(0, i * ratio, functools.partial(step, masked=False), carry)
    for d in range(ratio):
        carry = step(i * ratio + d, carry, masked=True)
    _, l, acc = carry
    o_ref[...] = acc / l


def _attn_sample_kernel(q_ref, cn_ref, krn_ref, ch_ref, krh_ref, wukt_ref, wuvbd_ref, o_ref, *, t, n_past):
    q = q_ref[...]
    qlat, qrope = [], []
    for h in range(N_HEADS):
        b0 = h * HEAD_PAD
        qn = q[:, b0:b0 + QK_NOPE].astype(BF16)
        qlat.append(jnp.dot(qn, wukt_ref[h], preferred_element_type=F32))
        qrope.append(q[:, b0 + ROPE_LO:b0 + ROPE_HI])
    ql = jnp.concatenate(qlat, axis=0).astype(BF16)
    qr = jnp.concatenate(qrope, axis=0).astype(BF16)

    chb = ch_ref[...].astype(BF16)
    krhb = krh_ref[...].astype(BF16)
    cnb = cn_ref[...].astype(BF16)
    krnb = krn_ref[...].astype(BF16)
    s_h = (lax.dot_general(ql, chb, NT_DIMS, preferred_element_type=F32)
           + lax.dot_general(qr, krhb, NT_DIMS, preferred_element_type=F32))
    s_n = (lax.dot_general(ql, cnb, NT_DIMS, preferred_element_type=F32)
           + lax.dot_general(qr, krnb, NT_DIMS, preferred_element_type=F32))

    rows = N_HEADS * t
    qchunk_h = (n_past + (lax.broadcasted_iota(jnp.int32, (rows, n_past), 0) & (t - 1))) // CHUNK
    s_h = jnp.where((lax.broadcasted_iota(jnp.int32, (rows, n_past), 1) // CHUNK) <= qchunk_h, s_h, NEG)
    qchunk_n = (n_past + (lax.broadcasted_iota(jnp.int32, (rows, t), 0) & (t - 1))) // CHUNK
    s_n = jnp.where(((n_past + lax.broadcasted_iota(jnp.int32, (rows, t), 1)) // CHUNK) <= qchunk_n, s_n, NEG)

    m = jnp.maximum(jnp.max(s_h, axis=-1, keepdims=True), jnp.max(s_n, axis=-1, keepdims=True))
    p_h = jnp.exp2(s_h - m)
    p_n = jnp.exp2(s_n - m)
    l = jnp.sum(p_h, axis=-1, keepdims=True) + jnp.sum(p_n, axis=-1, keepdims=True)
    olat = (jnp.dot(p_h.astype(BF16), chb, preferred_element_type=F32)
            + jnp.dot(p_n.astype(BF16), cnb, preferred_element_type=F32)) / l
    wide = jnp.concatenate([olat[h * t:(h + 1) * t] for h in range(N_HEADS)], axis=1).astype(BF16)
    o_ref[...] = jnp.dot(wide, wuvbd_ref[...], preferred_element_type=F32)


def _post_kernel(x_ref, o_ref, u_ref, uprev_ref, pe_ref, chist_ref,
                 wpool_ref, spool_ref, wo_ref, ln1g_ref, ln1b_ref, wup_ref, wdw_ref, bdw_ref, wdown_ref,
                 wpg_ref, wpe_ref, ln2g_ref, ln2b_ref,
                 y_ref, clast_ref,
                 ubuf, bufa, bufb, hbuf, cbuf, *, tm, carry, o_transposed, pos0):
    i = pl.program_id(0)

    if carry:
        @pl.when(i == 0)
        def _():
            cbuf[...] = jnp.zeros_like(cbuf)
            cbuf[CONV_HALO - (CONV_W - 1):CONV_HALO, :] = chist_ref[...]
    else:
        cbuf[...] = jnp.zeros_like(cbuf)
        cbuf[CONV_HALO - (CONV_W - 1):CONV_HALO, :] = chist_ref[...]

    u = u_ref[...]
    uprev = uprev_ref[...]
    if carry:
        uprev = jnp.where(i > 0, uprev, 0.0)
    ubuf[0:HALO, :] = uprev
    ubuf[HALO:HALO + tm, :] = u
    frame = lax.broadcasted_iota(jnp.int32, (tm, POOL_GROUP_W), 0) + (i * tm if carry else 0)
    pooled = []
    for g, w in enumerate(POOL_WINDOWS):
        cols = slice(g * POOL_GROUP_W, (g + 1) * POOL_GROUP_W)
        win = u[:, cols]
        for k in range(1, w):
            win = win + ubuf[HALO - k:HALO - k + tm, cols]
        cnt = jnp.minimum(w, pos0 + frame + 1).astype(F32)
        d = win / cnt - u[:, cols]
        yg = jnp.dot(d.astype(BF16), wpool_ref[g], preferred_element_type=F32) * spool_ref[:, cols]
        pooled.append(yg.astype(BF16))
    pooled = jnp.concatenate(pooled, axis=1)

    attn = o_ref[...].T if o_transposed else o_ref[...]
    mix = (jnp.dot(attn.astype(BF16), wo_ref[0:MLA_WIDTH, :], preferred_element_type=F32)
           + jnp.dot(pooled, wo_ref[MLA_WIDTH:D_MODEL, :], preferred_element_type=F32))
    x1 = _layer_norm(ALPHA * x_ref[...] + mix, ln1g_ref[...], ln1b_ref[...])
    x1b = x1.astype(BF16)

    def conv_chunk(cols, buf):
        up = jnp.dot(x1b, wup_ref[:, cols], preferred_element_type=F32)
        buf[0:CONV_HALO, :] = cbuf[:, cols]
        buf[CONV_HALO:CONV_HALO + tm, :] = up
        cbuf[:, cols] = buf[tm:tm + CONV_HALO, :]
        return (buf[CONV_HALO - 2:CONV_HALO - 2 + tm, :] * wdw_ref[0:1, cols]
                + buf[CONV_HALO - 1:CONV_HALO - 1 + tm, :] * wdw_ref[1:2, cols]
                + up * wdw_ref[2:3, cols] + bdw_ref[:, cols])

    for j in range(D_FF // FF_CHUNK):
        a = conv_chunk(slice(j * FF_CHUNK, (j + 1) * FF_CHUNK), bufa)
        b = conv_chunk(slice(D_FF + j * FF_CHUNK, D_FF + (j + 1) * FF_CHUNK), bufb)
        hbuf[:, j * FF_CHUNK:(j + 1) * FF_CHUNK] = (a * jax.nn.sigmoid(a) * b).astype(BF16)
    ffn = jnp.dot(hbuf[...], wdown_ref[...], preferred_element_type=F32)
    clast_ref[...] = cbuf[...]

    ple = (jax.nn.sigmoid(jnp.dot(x1b, wpg_ref[...], preferred_element_type=F32))
           * jnp.dot(pe_ref[...].astype(BF16), wpe_ref[...], preferred_element_type=F32))
    y_ref[...] = _layer_norm(ALPHA * x1 + ffn + ple, ln2g_ref[...], ln2b_ref[...])


def _const_spec(shape):
    nd = len(shape)
    return pl.BlockSpec(shape, lambda *_: (0,) * nd)


def _params(semantics):
    return pltpu.CompilerParams(dimension_semantics=semantics, vmem_limit_bytes=VMEM_LIMIT)


def _pad_heads(w, per_head, used):
    k = w.shape[0]
    w = w.reshape(k, N_HEADS, per_head)[:, :, :used]
    return jnp.pad(w, ((0, 0), (0, 0), (0, HEAD_PAD - used))).reshape(k, N_HEADS * HEAD_PAD)


def _prep_weights(w_in, w_q_b, w_kv_b, w_pool, w_o, w_up, w_down, w_pg, w_pe):
    w_kr = jnp.pad(w_in[:, 512:512 + QK_ROPE], ((0, 0), (ROPE_LO, LANES - ROPE_HI)))
    win = jnp.concatenate([w_in[:, :512], w_kr, w_in[:, 512 + QK_ROPE:]], axis=1).astype(BF16)
    wq_pad = _pad_heads(w_q_b, QK_NOPE + QK_ROPE, QK_NOPE + QK_ROPE).astype(BF16)
    w_kv = w_kv_b.reshape(KV_LORA, N_HEADS, QK_NOPE + V_DIM)
    w_uk, w_uv = w_kv[..., :QK_NOPE], w_kv[..., QK_NOPE:]
    wuk_pad = jnp.pad(w_uk, ((0, 0), (0, 0), (0, HEAD_PAD - QK_NOPE))).reshape(KV_LORA, -1).astype(BF16)
    wuvt = w_uv.reshape(KV_LORA, N_HEADS * V_DIM).T.astype(BF16)
    wukt = jnp.transpose(w_uk, (1, 2, 0)).astype(BF16)
    eye = jnp.eye(N_HEADS, dtype=w_uv.dtype)
    wuv_bd = jnp.einsum('lhv,hg->hlgv', w_uv, eye).reshape(N_HEADS * KV_LORA, N_HEADS * V_DIM).astype(BF16)
    return dict(win=win, wq_pad=wq_pad, wqt=wq_pad.T, wuk_pad=wuk_pad, wuvt=wuvt, wukt=wukt, wuv_bd=wuv_bd,
                wpool=w_pool.astype(BF16), wo=w_o.astype(BF16), wup=w_up.astype(BF16),
                wdown=w_down.astype(BF16), wpg=w_pg.astype(BF16), wpe=w_pe.astype(BF16))


def _rope_inv():
    inv = 1.0 / (ROPE_THETA ** (jnp.arange(0, QK_ROPE, 2, dtype=F32) / QK_ROPE))
    inv_lane = jnp.zeros((1, LANES), F32).at[0, ROPE_LO:ROPE_HI].set(jnp.concatenate([inv, inv]))
    return inv_lane, inv[:, None]


def _project_prompt(x, wb, g_q, g_kv, tm, tk):
    s = x.shape[0]
    inv_lane, inv_sub = _rope_inv()
    n = s // tm
    outs = pl.pallas_call(
        functools.partial(_proj_prompt_kernel, tm=tm, tk=tk, pos0=0),
        grid=(n,),
        in_specs=[pl.BlockSpec((tm, D_MODEL), lambda i: (i, 0)),
                  _const_spec(wb['win'].shape), _const_spec((1, Q_LORA)), _const_spec((1, KV_LORA)),
                  _const_spec(wb['wqt'].shape), _const_spec(wb['wuk_pad'].shape), _const_spec(wb['wuvt'].shape),
                  _const_spec((1, LANES)), _const_spec((QK_ROPE // 2, 1))],
        out_specs=[pl.BlockSpec((N_HEADS * HEAD_PAD, tm), lambda i: (0, i)),
                   pl.BlockSpec((tm, N_HEADS * HEAD_PAD), lambda i: (i, 0)),
                   pl.BlockSpec((tm // tk, MLA_WIDTH, tk), lambda i: (i, 0, 0)),
                   pl.BlockSpec((tm, KV_LORA), lambda i: (i, 0)),
                   pl.BlockSpec((tm, QK_ROPE), lambda i: (i, 0)),
                   pl.BlockSpec((tm, POOL_WIDTH), lambda i: (i, 0))],
        out_shape=[jax.ShapeDtypeStruct((N_HEADS * HEAD_PAD, s), BF16),
                   jax.ShapeDtypeStruct((s, N_HEADS * HEAD_PAD), BF16),
                   jax.ShapeDtypeStruct((s // tk, MLA_WIDTH, tk), BF16),
                   jax.ShapeDtypeStruct((s, KV_LORA), F32),
                   jax.ShapeDtypeStruct((s, QK_ROPE), F32),
                   jax.ShapeDtypeStruct((s, POOL_WIDTH), F32)],
        compiler_params=_params(("arbitrary",)),
        name="proj_prompt",
    )(x, wb['win'], g_q.reshape(1, -1), g_kv.reshape(1, -1), wb['wqt'], wb['wuk_pad'], wb['wuvt'],
      inv_lane, inv_sub)
    return outs


def _project_sample(x, wb, g_q, g_kv, seq_len, pos0):
    rows = x.shape[0]
    assert seq_len & (seq_len - 1) == 0
    inv_lane, _ = _rope_inv()
    return pl.pallas_call(
        functools.partial(_proj_sample_kernel, tm=rows, seq_len=seq_len, pos0=pos0),
        grid=(1,),
        in_specs=[_const_spec((rows, D_MODEL)), _const_spec(wb['win'].shape), _const_spec((1, Q_LORA)),
                  _const_spec((1, KV_LORA)), _const_spec(wb['wq_pad'].shape), _const_spec((1, LANES))],
        out_specs=[_const_spec((rows, N_HEADS * HEAD_PAD)), _const_spec((rows, KV_LORA)),
                   _const_spec((rows, QK_ROPE)), _const_spec((rows, POOL_WIDTH))],
        out_shape=[jax.ShapeDtypeStruct((rows, N_HEADS * HEAD_PAD), F32),
                   jax.ShapeDtypeStruct((rows, KV_LORA), F32),
                   jax.ShapeDtypeStruct((rows, QK_ROPE), F32),
                   jax.ShapeDtypeStruct((rows, POOL_WIDTH), F32)],
        compiler_params=_params(("arbitrary",)),
        name="proj_sample",
    )(x, wb['win'], g_q.reshape(1, -1), g_kv.reshape(1, -1), wb['wq_pad'], inv_lane)


def _attend_prompt(qt, k, vt, tq, tk):
    s = k.shape[0]
    nkv = s // tk
    k3 = k.reshape(nkv, tk, N_HEADS * HEAD_PAD)
    return pl.pallas_call(
        functools.partial(_attn_prompt_kernel, tq=tq, tk=tk),
        grid=(N_HEADS, s // tq),
        in_specs=[pl.BlockSpec((HEAD_PAD, tq), lambda h, i: (h, i)),
                  pl.BlockSpec((nkv, tk, HEAD_PAD), lambda h, i: (0, 0, h)),
                  pl.BlockSpec((nkv, V_DIM, tk), lambda h, i: (0, h, 0))],
        out_specs=pl.BlockSpec((V_DIM, tq), lambda h, i: (h, i)),
        out_shape=jax.ShapeDtypeStruct((MLA_WIDTH, s), F32),
        compiler_params=_params(("arbitrary", "arbitrary")),
        name="attn_prompt",
    )(qt, k3, vt)


def _attend_sample(q, c_new, kr_new, c_hist, kr_hist, wb, t):
    nb, n_past, _ = c_hist.shape
    assert t & (t - 1) == 0
    return pl.pallas_call(
        functools.partial(_attn_sample_kernel, t=t, n_past=n_past),
        grid=(nb,),
        in_specs=[pl.BlockSpec((t, N_HEADS * HEAD_PAD), lambda b: (b, 0)),
                  pl.BlockSpec((t, KV_LORA), lambda b: (b, 0)),
                  pl.BlockSpec((t, QK_ROPE), lambda b: (b, 0)),
                  pl.BlockSpec((None, n_past, KV_LORA), lambda b: (b, 0, 0)),
                  pl.BlockSpec((None, n_past, QK_ROPE), lambda b: (b, 0, 0)),
                  _const_spec(wb['wukt'].shape), _const_spec(wb['wuv_bd'].shape)],
        out_specs=pl.BlockSpec((t, MLA_WIDTH), lambda b: (b, 0)),
        out_shape=jax.ShapeDtypeStruct((nb * t, MLA_WIDTH), F32),
        compiler_params=_params(("arbitrary",)),
        name="attn_sample",
    )(q, c_new, kr_new, c_hist, kr_hist, wb['wukt'], wb['wuv_bd'])


def _post(x, o, u, uprev, uprev_map, pe, chist, chist_map, wb, small, *, tm, carry, o_transposed, pos0):
    rows = x.shape[0]
    n = rows // tm
    s_pool, ln1_g, ln1_b, w_dw, b_dw, ln2_g, ln2_b = small
    o_spec = (pl.BlockSpec((MLA_WIDTH, tm), lambda i: (0, i)) if o_transposed
              else pl.BlockSpec((tm, MLA_WIDTH), lambda i: (i, 0)))
    clast_spec = (pl.BlockSpec((None, CONV_HALO, 2 * D_FF), lambda i: (0, 0, 0)) if carry
                  else pl.BlockSpec((None, CONV_HALO, 2 * D_FF), lambda i: (i, 0, 0)))
    row = lambda v: v.reshape(1, -1)
    return pl.pallas_call(
        functools.partial(_post_kernel, tm=tm, carry=carry, o_transposed=o_transposed, pos0=pos0),
        grid=(n,),
        in_specs=[pl.BlockSpec((tm, D_MODEL), lambda i: (i, 0)),
                  o_spec,
                  pl.BlockSpec((tm, POOL_WIDTH), lambda i: (i, 0)),
                  pl.BlockSpec((None, HALO, POOL_WIDTH), uprev_map),
                  pl.BlockSpec((tm, PLE_DIM), lambda i: (i, 0)),
                  pl.BlockSpec((None, CONV_W - 1, 2 * D_FF), chist_map),
                  _const_spec(wb['wpool'].shape), _const_spec((1, POOL_WIDTH)), _const_spec(wb['wo'].shape),
                  _const_spec((1, D_MODEL)), _const_spec((1, D_MODEL)),
                  _const_spec(wb['wup'].shape), _const_spec((CONV_W, 2 * D_FF)), _const_spec((1, 2 * D_FF)),
                  _const_spec(wb['wdown'].shape), _const_spec(wb['wpg'].shape), _const_spec(wb['wpe'].shape),
                  _const_spec((1, D_MODEL)), _const_spec((1, D_MODEL))],
        out_specs=[pl.BlockSpec((tm, D_MODEL), lambda i: (i, 0)), clast_spec],
        out_shape=[jax.ShapeDtypeStruct((rows, D_MODEL), F32),
                   jax.ShapeDtypeStruct((1 if carry else n, CONV_HALO, 2 * D_FF), F32)],
        scratch_shapes=[pltpu.VMEM((tm + HALO, POOL_WIDTH), F32),
                        pltpu.VMEM((tm + CONV_HALO, FF_CHUNK), F32),
                        pltpu.VMEM((tm + CONV_HALO, FF_CHUNK), F32),
                        pltpu.VMEM((tm, D_FF), BF16),
                        pltpu.VMEM((CONV_HALO, 2 * D_FF), F32)],
        compiler_params=_params(("arbitrary",)),
        name="post_prompt" if carry else "post_sample",
    )(x, o, u, uprev, pe, chist,
      wb['wpool'], row(s_pool), wb['wo'], row(ln1_g), row(ln1_b), wb['wup'], w_dw, row(b_dw), wb['wdown'],
      wb['wpg'], wb['wpe'], row(ln2_g), row(ln2_b))


def _layer_prompt(x, pe, wb, g_q, g_kv, small, *, tm_proj, tq, tk, tm_post):
    s = x.shape[0]
    qt, k, vt, c, kr, u = _project_prompt(x, wb, g_q, g_kv, tm_proj, tk)
    ot = _attend_prompt(qt, k, vt, tq, tk)
    per = tm_post // HALO
    y, clast = _post(x, ot, u, u.reshape(s // HALO, HALO, POOL_WIDTH),
                     lambda i: (jnp.maximum(i * per - 1, 0), 0, 0),
                     pe, jnp.zeros((1, CONV_W - 1, 2 * D_FF), F32), lambda i: (0, 0, 0),
                     wb, small, tm=tm_post, carry=True, o_transposed=True, pos0=0)
    return y, c, kr, u[s - POOL_HIST:], clast[0, CONV_HALO - (CONV_W - 1):]


def _layer_sample(x, pe, c_hist, kr_hist, pool_hist, conv_hist, wb, g_q, g_kv, small):
    nb, t, _ = x.shape
    n_past = c_hist.shape[1]
    xf = x.reshape(nb * t, D_MODEL)
    q, c, kr, u = _project_sample(xf, wb, g_q, g_kv, t, n_past)
    o = _attend_sample(q, c, kr, c_hist, kr_hist, wb, t)
    uprev = jnp.pad(pool_hist, ((0, 0), (HALO - POOL_HIST, 0), (0, 0)))
    y, clast = _post(xf, o, u, uprev, lambda i: (i, 0, 0), pe.reshape(nb * t, PLE_DIM),
                     conv_hist, lambda i: (i, 0, 0),
                     wb, small, tm=t, carry=False, o_transposed=False, pos0=n_past)
    u3 = u.reshape(nb, t, POOL_WIDTH)
    new_pool = jnp.concatenate([pool_hist, u3], axis=1)[:, -POOL_HIST:]
    return (y.reshape(nb, t, D_MODEL), c.reshape(nb, t, KV_LORA), kr.reshape(nb, t, QK_ROPE), new_pool,
            clast[:, CONV_HALO - (CONV_W - 1):])


def kernel(x_prompt, x_sample, cache_ckv, cache_krope, state_pool, state_ffn_conv, p_prompt, p_sample,
           w_in, g_q, w_q_b, g_kv, w_kv_b, w_pool, s_pool, w_o, ln1_g, ln1_b,
           w_up, w_dw, b_dw, w_down, w_pg, w_pe, ln2_g, ln2_b):
    assert x_prompt.shape[0] == 1 and w_in.shape[0] == DEPTH
    wb = _prep_weights(w_in[0], w_q_b[0], w_kv_b[0], w_pool[0], w_o[0], w_up[0], w_down[0], w_pg[0], w_pe[0])
    small = (s_pool[0], ln1_g[0], ln1_b[0], w_dw[0], b_dw[0], ln2_g[0], ln2_b[0])
    s = x_prompt.shape[1]
    tm_proj = min(512, s)
    tq = tk = min(256, s)
    yp, cp, krp, poolp, convp = _layer_prompt(x_prompt[0], p_prompt[0, 0], wb, g_q[0], g_kv[0], small,
                                              tm_proj=tm_proj, tq=tq, tk=tk, tm_post=min(256, s))
    ys, cs, krs, pools, convs = _layer_sample(x_sample, p_sample[0], cache_ckv[0], cache_krope[0],
                                              state_pool[0], state_ffn_conv[0], wb, g_q[0], g_kv[0], small)
    return (yp[None], ys, cp[None, None], krp[None, None], poolp[None, None], convp[None, None],
            cs[None], krs[None], pools[None], convs[None])
```

```python
import functools
import math

import jax
import jax.numpy as jnp
from jax import lax
from jax.experimental import pallas as pl
from jax.experimental.pallas import tpu as pltpu

F32 = jnp.float32
BF16 = jnp.bfloat16

D_MODEL = 1024
CHUNK = 64
CHUNK_SHIFT = 6
ATTN_HEAD_GROUP = 4
N_HEADS = 8
QK_NOPE = 64
QK_ROPE = 32
V_DIM = 64
Q_LORA = 256
KV_LORA = 256
ROPE_THETA = 10000.0
MLA_WIDTH = N_HEADS * V_DIM
POOL_WINDOWS = (2, 4, 8, 16)
POOL_GROUP_W = 128
POOL_WIDTH = D_MODEL - MLA_WIDTH
POOL_HIST = max(POOL_WINDOWS) - 1
D_FF = 2816
CONV_W = 3
PLE_DIM = 256
DEPTH = 1
ALPHA = (2 * DEPTH) ** 0.25
LN_EPS = 1e-5
RMS_EPS = 1e-6
NEG = -1e30
ATTN_SCALE = 1.0 / math.sqrt(QK_NOPE + QK_ROPE)

LANES = 128
HEAD_PAD = 128
ROPE_LO = QK_NOPE
ROPE_MID = QK_NOPE + QK_ROPE // 2
ROPE_HI = QK_NOPE + QK_ROPE
HALO = 16
CONV_HALO = 8
FF_CHUNK = 256
Q_SCALE = ATTN_SCALE * math.log2(math.e)
VMEM_LIMIT = 56 * 1024 * 1024

NT_DIMS = (((1,), (1,)), ((), ()))


def _rms(x, g):
    return x * lax.rsqrt(jnp.mean(x * x, axis=-1, keepdims=True) + RMS_EPS) * g


def _layer_norm(x, g, b):
    mu = jnp.mean(x, axis=-1, keepdims=True)
    xc = x - mu
    var = jnp.mean(xc * xc, axis=-1, keepdims=True)
    return xc * lax.rsqrt(var + LN_EPS) * g + b


def _token_rope_tables(pos_col, inv_lane):
    ang = pos_col.astype(F32) * inv_lane
    lane = lax.broadcasted_iota(jnp.int32, ang.shape, 1)
    first = lane < ROPE_MID
    sin = jnp.sin(ang)
    return jnp.cos(ang), jnp.where(first, -sin, sin), first


def _token_rope(x, cos, sin_signed, first):
    partner = jnp.where(first, pltpu.roll(x, LANES - QK_ROPE // 2, 1), pltpu.roll(x, QK_ROPE // 2, 1))
    return x * cos + partner * sin_signed


def _proj_prompt_kernel(x_ref, win_ref, gq_ref, gkv_ref, wqt_ref, wuk_ref, wuvt_ref, invl_ref, invs_ref,
                        qt_ref, k_ref, vt_ref, c_ref, kr_ref, u_ref, *, tm, tk, pos0):
    i = pl.program_id(0)
    z = jnp.dot(x_ref[...].astype(BF16), win_ref[...], preferred_element_type=F32)
    q_a = z[:, 0:Q_LORA]
    c_raw = z[:, Q_LORA:Q_LORA + KV_LORA]
    kr_pad = z[:, 512:640]
    u_ref[...] = z[:, 640:640 + POOL_WIDTH]

    qn = _rms(q_a, gq_ref[...]).astype(BF16)
    c = _rms(c_raw, gkv_ref[...])
    c_ref[...] = c
    cb = c.astype(BF16)

    pos_col = pos0 + i * tm + lax.broadcasted_iota(jnp.int32, (tm, LANES), 0)
    cos, sin_signed, first = _token_rope_tables(pos_col, invl_ref[...])
    kr_rot = _token_rope(kr_pad, cos, sin_signed, first)
    kr_ref[...] = kr_rot[:, ROPE_LO:ROPE_HI]

    k_nope = jnp.dot(cb, wuk_ref[...], preferred_element_type=F32)
    for h in range(N_HEADS):
        sl = slice(h * HEAD_PAD, (h + 1) * HEAD_PAD)
        k_ref[:, sl] = (k_nope[:, sl] + kr_rot).astype(BF16)

    vt = lax.dot_general(wuvt_ref[...], cb, NT_DIMS, preferred_element_type=F32)
    for s in range(tm // tk):
        vt_ref[s] = vt[:, s * tk:(s + 1) * tk].astype(BF16)

    qt = lax.dot_general(wqt_ref[...], qn, NT_DIMS, preferred_element_type=F32)
    pos_row = pos0 + i * tm + lax.broadcasted_iota(jnp.int32, (QK_ROPE // 2, tm), 1)
    ang = pos_row.astype(F32) * invs_ref[...]
    cos_t, sin_t = jnp.cos(ang), jnp.sin(ang)
    for h in range(N_HEADS):
        b0 = h * HEAD_PAD
        qt_ref[b0:b0 + ROPE_LO, :] = (qt[b0:b0 + ROPE_LO] * Q_SCALE).astype(BF16)
        x1 = qt[b0 + ROPE_LO:b0 + ROPE_MID]
        x2 = qt[b0 + ROPE_MID:b0 + ROPE_HI]
        qt_ref[b0 + ROPE_LO:b0 + ROPE_MID, :] = ((x1 * cos_t - x2 * sin_t) * Q_SCALE).astype(BF16)
        qt_ref[b0 + ROPE_MID:b0 + ROPE_HI, :] = ((x2 * cos_t + x1 * sin_t) * Q_SCALE).astype(BF16)
        qt_ref[b0 + ROPE_HI:b0 + HEAD_PAD, :] = jnp.zeros((HEAD_PAD - ROPE_HI, tm), BF16)


def _proj_sample_kernel(x_ref, win_ref, gq_ref, gkv_ref, wq_ref, invl_ref,
                        q_ref, c_ref, kr_ref, u_ref, *, tm, seq_len, pos0):
    z = jnp.dot(x_ref[...].astype(BF16), win_ref[...], preferred_element_type=F32)
    q_a = z[:, 0:Q_LORA]
    c_raw = z[:, Q_LORA:Q_LORA + KV_LORA]
    kr_pad = z[:, 512:640]
    u_ref[...] = z[:, 640:640 + POOL_WIDTH]

    qn = _rms(q_a, gq_ref[...]).astype(BF16)
    c_ref[...] = _rms(c_raw, gkv_ref[...])

    row = lax.broadcasted_iota(jnp.int32, (tm, LANES), 0)
    pos_col = pos0 + (row & (seq_len - 1))
    cos, sin_signed, first = _token_rope_tables(pos_col, invl_ref[...])
    kr_ref[...] = _token_rope(kr_pad, cos, sin_signed, first)[:, ROPE_LO:ROPE_HI]

    q = jnp.dot(qn, wq_ref[...], preferred_element_type=F32)
    for h in range(N_HEADS):
        sl = slice(h * HEAD_PAD, (h + 1) * HEAD_PAD)
        q_ref[:, sl] = _token_rope(q[:, sl], cos, sin_signed, first) * Q_SCALE


def _attn_prompt_kernel(qt_ref, k_ref, vt_ref, o_ref, s_ref, m_ref, l_ref, acc_ref, *, tq, hg):
    i = pl.program_id(1)
    m_ref[...] = jnp.full(m_ref.shape, NEG, F32)
    l_ref[...] = jnp.zeros(l_ref.shape, F32)
    acc_ref[...] = jnp.zeros(acc_ref.shape, F32)

    def scores(j, h):
        return jnp.dot(k_ref[j, :, h * HEAD_PAD:(h + 1) * HEAD_PAD], qt_ref[h * HEAD_PAD:(h + 1) * HEAD_PAD, :],
                       preferred_element_type=F32)

    def consume(h, jv, j_next):
        s = s_ref[h]
        m_old = m_ref[h]
        m_new = jnp.maximum(m_old, jnp.max(s, axis=0, keepdims=True))
        alpha = jnp.exp2(m_old - m_new)
        p = jnp.exp2(s - m_new)
        m_ref[h] = m_new
        l_ref[h] = alpha * l_ref[h] + jnp.sum(p, axis=0, keepdims=True)
        if j_next is not None:
            s_ref[h] = scores(j_next, h)
        acc_ref[h] = alpha * acc_ref[h] + jnp.dot(vt_ref[jv, h * V_DIM:(h + 1) * V_DIM, :], p.astype(BF16),
                                                  preferred_element_type=F32)

    kchunk = lax.broadcasted_iota(jnp.int32, (tq, tq), 0) >> CHUNK_SHIFT
    qchunk = lax.broadcasted_iota(jnp.int32, (tq, tq), 1) >> CHUNK_SHIFT
    visible = kchunk <= qchunk
    for h in range(hg):
        s_ref[h] = jnp.where(visible, scores(i, h), NEG)

    def body(j, c):
        jv = jnp.where(j == 0, i, j - 1)
        for h in range(hg):
            consume(h, jv, j)
        return c

    lax.fori_loop(0, i, body, 0)
    for h in range(hg):
        consume(h, jnp.maximum(i - 1, 0), None)
        o_ref[h * V_DIM:(h + 1) * V_DIM, :] = acc_ref[h] / l_ref[h]


def _attn_sample_kernel(q_ref, cn_ref, krn_ref, ch_ref, krh_ref, wukt_ref, wuvbd_ref, o_ref, *, t, n_past):
    q = q_ref[...]
    qlat, qrope = [], []
    for h in range(N_HEADS):
        b0 = h * HEAD_PAD
        qn = q[:, b0:b0 + QK_NOPE].astype(BF16)
        qlat.append(jnp.dot(qn, wukt_ref[h], preferred_element_type=F32))
        qrope.append(q[:, b0 + ROPE_LO:b0 + ROPE_HI])
    ql = jnp.concatenate(qlat, axis=0).astype(BF16)
    qr = jnp.concatenate(qrope, axis=0).astype(BF16)

    chb = ch_ref[...].astype(BF16)
    krhb = krh_ref[...].astype(BF16)
    cnb = cn_ref[...].astype(BF16)
    krnb = krn_ref[...].astype(BF16)
    s_h = (lax.dot_general(ql, chb, NT_DIMS, preferred_element_type=F32)
           + lax.dot_general(qr, krhb, NT_DIMS, preferred_element_type=F32))
    s_n = (lax.dot_general(ql, cnb, NT_DIMS, preferred_element_type=F32)
           + lax.dot_general(qr, krnb, NT_DIMS, preferred_element_type=F32))

    rows = N_HEADS * t
    qchunk_h = (n_past + (lax.broadcasted_iota(jnp.int32, (rows, n_past), 0) & (t - 1))) // CHUNK
    s_h = jnp.where((lax.broadcasted_iota(jnp.int32, (rows, n_past), 1) // CHUNK) <= qchunk_h, s_h, NEG)
    qchunk_n = (n_past + (lax.broadcasted_iota(jnp.int32, (rows, t), 0) & (t - 1))) // CHUNK
    s_n = jnp.where(((n_past + lax.broadcasted_iota(jnp.int32, (rows, t), 1)) // CHUNK) <= qchunk_n, s_n, NEG)

    m = jnp.maximum(jnp.max(s_h, axis=-1, keepdims=True), jnp.max(s_n, axis=-1, keepdims=True))
    p_h = jnp.exp2(s_h - m)
    p_n = jnp.exp2(s_n - m)
    l = jnp.sum(p_h, axis=-1, keepdims=True) + jnp.sum(p_n, axis=-1, keepdims=True)
    olat = (jnp.dot(p_h.astype(BF16), chb, preferred_element_type=F32)
            + jnp.dot(p_n.astype(BF16), cnb, preferred_element_type=F32)) / l
    wide = jnp.concatenate([olat[h * t:(h + 1) * t] for h in range(N_HEADS)], axis=1).astype(BF16)
    o_ref[...] = jnp.dot(wide, wuvbd_ref[...], preferred_element_type=F32)


def _post_kernel(x_ref, o_ref, u_ref, uprev_ref, pe_ref, chist_ref,
                 wpool_ref, spool_ref, wo_ref, ln1g_ref, ln1b_ref, wup_ref, wdw_ref, bdw_ref, wdown_ref,
                 wpg_ref, wpe_ref, ln2g_ref, ln2b_ref,
                 y_ref, clast_ref,
                 ubuf, bufa, bufb, hbuf, cbuf, *, tm, carry, o_transposed, pos0):
    i = pl.program_id(0)

    if carry:
        @pl.when(i == 0)
        def _():
            cbuf[...] = jnp.zeros_like(cbuf)
            cbuf[CONV_HALO - (CONV_W - 1):CONV_HALO, :] = chist_ref[...]
    else:
        cbuf[...] = jnp.zeros_like(cbuf)
        cbuf[CONV_HALO - (CONV_W - 1):CONV_HALO, :] = chist_ref[...]

    u = u_ref[...]
    uprev = uprev_ref[...]
    if carry:
        uprev = jnp.where(i > 0, uprev, 0.0)
    ubuf[0:HALO, :] = uprev
    ubuf[HALO:HALO + tm, :] = u
    frame = lax.broadcasted_iota(jnp.int32, (tm, POOL_GROUP_W), 0) + (i * tm if carry else 0)
    pooled = []
    for g, w in enumerate(POOL_WINDOWS):
        cols = slice(g * POOL_GROUP_W, (g + 1) * POOL_GROUP_W)
        win = u[:, cols]
        for k in range(1, w):
            win = win + ubuf[HALO - k:HALO - k + tm, cols]
        cnt = jnp.minimum(w, pos0 + frame + 1).astype(F32)
        d = win / cnt - u[:, cols]
        yg = jnp.dot(d.astype(BF16), wpool_ref[g], preferred_element_type=F32) * spool_ref[:, cols]
        pooled.append(yg.astype(BF16))
    pooled = jnp.concatenate(pooled, axis=1)

    attn = o_ref[...].T if o_transposed else o_ref[...]
    mix = (jnp.dot(attn.astype(BF16), wo_ref[0:MLA_WIDTH, :], preferred_element_type=F32)
           + jnp.dot(pooled, wo_ref[MLA_WIDTH:D_MODEL, :], preferred_element_type=F32))
    x1 = _layer_norm(ALPHA * x_ref[...] + mix, ln1g_ref[...], ln1b_ref[...])
    x1b = x1.astype(BF16)

    def conv_chunk(cols, buf):
        up = jnp.dot(x1b, wup_ref[:, cols], preferred_element_type=F32)
        buf[0:CONV_HALO, :] = cbuf[:, cols]
        buf[CONV_HALO:CONV_HALO + tm, :] = up
        cbuf[:, cols] = buf[tm:tm + CONV_HALO, :]
        return (buf[CONV_HALO - 2:CONV_HALO - 2 + tm, :] * wdw_ref[0:1, cols]
                + buf[CONV_HALO - 1:CONV_HALO - 1 + tm, :] * wdw_ref[1:2, cols]
                + up * wdw_ref[2:3, cols] + bdw_ref[:, cols])

    for j in range(D_FF // FF_CHUNK):
        a = conv_chunk(slice(j * FF_CHUNK, (j + 1) * FF_CHUNK), bufa)
        b = conv_chunk(slice(D_FF + j * FF_CHUNK, D_FF + (j + 1) * FF_CHUNK), bufb)
        hbuf[:, j * FF_CHUNK:(j + 1) * FF_CHUNK] = (a * jax.nn.sigmoid(a) * b).astype(BF16)
    ffn = jnp.dot(hbuf[...], wdown_ref[...], preferred_element_type=F32)
    clast_ref[...] = cbuf[...]

    ple = (jax.nn.sigmoid(jnp.dot(x1b, wpg_ref[...], preferred_element_type=F32))
           * jnp.dot(pe_ref[...].astype(BF16), wpe_ref[...], preferred_element_type=F32))
    y_ref[...] = _layer_norm(ALPHA * x1 + ffn + ple, ln2g_ref[...], ln2b_ref[...])


def _const_spec(shape):
    nd = len(shape)
    return pl.BlockSpec(shape, lambda *_: (0,) * nd)


def _params(semantics):
    return pltpu.CompilerParams(dimension_semantics=semantics, vmem_limit_bytes=VMEM_LIMIT)


def _pad_heads(w, per_head, used):
    k = w.shape[0]
    w = w.reshape(k, N_HEADS, per_head)[:, :, :used]
    return jnp.pad(w, ((0, 0), (0, 0), (0, HEAD_PAD - used))).reshape(k, N_HEADS * HEAD_PAD)


def _prep_weights(w_in, w_q_b, w_kv_b, w_pool, w_o, w_up, w_down, w_pg, w_pe):
    w_kr = jnp.pad(w_in[:, 512:512 + QK_ROPE], ((0, 0), (ROPE_LO, LANES - ROPE_HI)))
    win = jnp.concatenate([w_in[:, :512], w_kr, w_in[:, 512 + QK_ROPE:]], axis=1).astype(BF16)
    wq_pad = _pad_heads(w_q_b, QK_NOPE + QK_ROPE, QK_NOPE + QK_ROPE).astype(BF16)
    w_kv = w_kv_b.reshape(KV_LORA, N_HEADS, QK_NOPE + V_DIM)
    w_uk, w_uv = w_kv[..., :QK_NOPE], w_kv[..., QK_NOPE:]
    wuk_pad = jnp.pad(w_uk, ((0, 0), (0, 0), (0, HEAD_PAD - QK_NOPE))).reshape(KV_LORA, -1).astype(BF16)
    wuvt = w_uv.reshape(KV_LORA, N_HEADS * V_DIM).T.astype(BF16)
    wukt = jnp.transpose(w_uk, (1, 2, 0)).astype(BF16)
    eye = jnp.eye(N_HEADS, dtype=w_uv.dtype)
    wuv_bd = jnp.einsum('lhv,hg->hlgv', w_uv, eye).reshape(N_HEADS * KV_LORA, N_HEADS * V_DIM).astype(BF16)
    return dict(win=win, wq_pad=wq_pad, wqt=wq_pad.T, wuk_pad=wuk_pad, wuvt=wuvt, wukt=wukt, wuv_bd=wuv_bd,
                wpool=w_pool.astype(BF16), wo=w_o.astype(BF16), wup=w_up.astype(BF16),
                wdown=w_down.astype(BF16), wpg=w_pg.astype(BF16), wpe=w_pe.astype(BF16))


def _rope_inv():
    inv = 1.0 / (ROPE_THETA ** (jnp.arange(0, QK_ROPE, 2, dtype=F32) / QK_ROPE))
    inv_lane = jnp.zeros((1, LANES), F32).at[0, ROPE_LO:ROPE_HI].set(jnp.concatenate([inv, inv]))
    return inv_lane, inv[:, None]


def _project_prompt(x, wb, g_q, g_kv, tm, tk):
    s = x.shape[0]
    inv_lane, inv_sub = _rope_inv()
    n = s // tm
    outs = pl.pallas_call(
        functools.partial(_proj_prompt_kernel, tm=tm, tk=tk, pos0=0),
        grid=(n,),
        in_specs=[pl.BlockSpec((tm, D_MODEL), lambda i: (i, 0)),
                  _const_spec(wb['win'].shape), _const_spec((1, Q_LORA)), _const_spec((1, KV_LORA)),
                  _const_spec(wb['wqt'].shape), _const_spec(wb['wuk_pad'].shape), _const_spec(wb['wuvt'].shape),
                  _const_spec((1, LANES)), _const_spec((QK_ROPE // 2, 1))],
        out_specs=[pl.BlockSpec((N_HEADS * HEAD_PAD, tm), lambda i: (0, i)),
                   pl.BlockSpec((tm, N_HEADS * HEAD_PAD), lambda i: (i, 0)),
                   pl.BlockSpec((tm // tk, MLA_WIDTH, tk), lambda i: (i, 0, 0)),
                   pl.BlockSpec((tm, KV_LORA), lambda i: (i, 0)),
                   pl.BlockSpec((tm, QK_ROPE), lambda i: (i, 0)),
                   pl.BlockSpec((tm, POOL_WIDTH), lambda i: (i, 0))],
        out_shape=[jax.ShapeDtypeStruct((N_HEADS * HEAD_PAD, s), BF16),
                   jax.ShapeDtypeStruct((s, N_HEADS * HEAD_PAD), BF16),
                   jax.ShapeDtypeStruct((s // tk, MLA_WIDTH, tk), BF16),
                   jax.ShapeDtypeStruct((s, KV_LORA), F32),
                   jax.ShapeDtypeStruct((s, QK_ROPE), F32),
                   jax.ShapeDtypeStruct((s, POOL_WIDTH), F32)],
        compiler_params=_params(("arbitrary",)),
        name="proj_prompt",
    )(x, wb['win'], g_q.reshape(1, -1), g_kv.reshape(1, -1), wb['wqt'], wb['wuk_pad'], wb['wuvt'],
      inv_lane, inv_sub)
    return outs


def _project_sample(x, wb, g_q, g_kv, seq_len, pos0):
    rows = x.shape[0]
    assert seq_len & (seq_len - 1) == 0
    inv_lane, _ = _rope_inv()
    return pl.pallas_call(
        functools.partial(_proj_sample_kernel, tm=rows, seq_len=seq_len, pos0=pos0),
        grid=(1,),
        in_specs=[_const_spec((rows, D_MODEL)), _const_spec(wb['win'].shape), _const_spec((1, Q_LORA)),
                  _const_spec((1, KV_LORA)), _const_spec(wb['wq_pad'].shape), _const_spec((1, LANES))],
        out_specs=[_const_spec((rows, N_HEADS * HEAD_PAD)), _const_spec((rows, KV_LORA)),
                   _const_spec((rows, QK_ROPE)), _const_spec((rows, POOL_WIDTH))],
        out_shape=[jax.ShapeDtypeStruct((rows, N_HEADS * HEAD_PAD), F32),
                   jax.ShapeDtypeStruct((rows, KV_LORA), F32),
                   jax.ShapeDtypeStruct((rows, QK_ROPE), F32),
                   jax.ShapeDtypeStruct((rows, POOL_WIDTH), F32)],
        compiler_params=_params(("arbitrary",)),
        name="proj_sample",
    )(x, wb['win'], g_q.reshape(1, -1), g_kv.reshape(1, -1), wb['wq_pad'], inv_lane)


def _attend_prompt(qt, k, vt, tq, tk, hg):
    s = k.shape[0]
    nkv = s // tk
    k3 = k.reshape(nkv, tk, N_HEADS * HEAD_PAD)
    resident = pl.Buffered(1)
    assert tq == tk
    return pl.pallas_call(
        functools.partial(_attn_prompt_kernel, tq=tq, hg=hg),
        grid=(N_HEADS // hg, s // tq),
        in_specs=[pl.BlockSpec((hg * HEAD_PAD, tq), lambda g, i: (g, i)),
                  pl.BlockSpec((nkv, tk, hg * HEAD_PAD), lambda g, i: (0, 0, g), pipeline_mode=resident),
                  pl.BlockSpec((nkv, hg * V_DIM, tk), lambda g, i: (0, g, 0), pipeline_mode=resident)],
        out_specs=pl.BlockSpec((hg * V_DIM, tq), lambda g, i: (g, i)),
        out_shape=jax.ShapeDtypeStruct((MLA_WIDTH, s), F32),
        scratch_shapes=[pltpu.VMEM((hg, tk, tq), F32), pltpu.VMEM((hg, 1, tq), F32), pltpu.VMEM((hg, 1, tq), F32),
                        pltpu.VMEM((hg, V_DIM, tq), F32)],
        compiler_params=_params(("arbitrary", "arbitrary")),
        name="attn_prompt",
    )(qt, k3, vt)


def _attend_sample(q, c_new, kr_new, c_hist, kr_hist, wb, t):
    nb, n_past, _ = c_hist.shape
    assert t & (t - 1) == 0
    return pl.pallas_call(
        functools.partial(_attn_sample_kernel, t=t, n_past=n_past),
        grid=(nb,),
        in_specs=[pl.BlockSpec((t, N_HEADS * HEAD_PAD), lambda b: (b, 0)),
                  pl.BlockSpec((t, KV_LORA), lambda b: (b, 0)),
                  pl.BlockSpec((t, QK_ROPE), lambda b: (b, 0)),
                  pl.BlockSpec((None, n_past, KV_LORA), lambda b: (b, 0, 0)),
                  pl.BlockSpec((None, n_past, QK_ROPE), lambda b: (b, 0, 0)),
                  _const_spec(wb['wukt'].shape), _const_spec(wb['wuv_bd'].shape)],
        out_specs=pl.BlockSpec((t, MLA_WIDTH), lambda b: (b, 0)),
        out_shape=jax.ShapeDtypeStruct((nb * t, MLA_WIDTH), F32),
        compiler_params=_params(("arbitrary",)),
        name="attn_sample",
    )(q, c_new, kr_new, c_hist, kr_hist, wb['wukt'], wb['wuv_bd'])


def _post(x, o, u, uprev, uprev_map, pe, chist, chist_map, wb, small, *, tm, carry, o_transposed, pos0):
    rows = x.shape[0]
    n = rows // tm
    s_pool, ln1_g, ln1_b, w_dw, b_dw, ln2_g, ln2_b = small
    o_spec = (pl.BlockSpec((MLA_WIDTH, tm), lambda i: (0, i)) if o_transposed
              else pl.BlockSpec((tm, MLA_WIDTH), lambda i: (i, 0)))
    clast_spec = (pl.BlockSpec((None, CONV_HALO, 2 * D_FF), lambda i: (0, 0, 0)) if carry
                  else pl.BlockSpec((None, CONV_HALO, 2 * D_FF), lambda i: (i, 0, 0)))
    row = lambda v: v.reshape(1, -1)
    return pl.pallas_call(
        functools.partial(_post_kernel, tm=tm, carry=carry, o_transposed=o_transposed, pos0=pos0),
        grid=(n,),
        in_specs=[pl.BlockSpec((tm, D_MODEL), lambda i: (i, 0)),
                  o_spec,
                  pl.BlockSpec((tm, POOL_WIDTH), lambda i: (i, 0)),
                  pl.BlockSpec((None, HALO, POOL_WIDTH), uprev_map),
                  pl.BlockSpec((tm, PLE_DIM), lambda i: (i, 0)),
                  pl.BlockSpec((None, CONV_W - 1, 2 * D_FF), chist_map),
                  _const_spec(wb['wpool'].shape), _const_spec((1, POOL_WIDTH)), _const_spec(wb['wo'].shape),
                  _const_spec((1, D_MODEL)), _const_spec((1, D_MODEL)),
                  _const_spec(wb['wup'].shape), _const_spec((CONV_W, 2 * D_FF)), _const_spec((1, 2 * D_FF)),
                  _const_spec(wb['wdown'].shape), _const_spec(wb['wpg'].shape), _const_spec(wb['wpe'].shape),
                  _const_spec((1, D_MODEL)), _const_spec((1, D_MODEL))],
        out_specs=[pl.BlockSpec((tm, D_MODEL), lambda i: (i, 0)), clast_spec],
        out_shape=[jax.ShapeDtypeStruct((rows, D_MODEL), F32),
                   jax.ShapeDtypeStruct((1 if carry else n, CONV_HALO, 2 * D_FF), F32)],
        scratch_shapes=[pltpu.VMEM((tm + HALO, POOL_WIDTH), F32),
                        pltpu.VMEM((tm + CONV_HALO, FF_CHUNK), F32),
                        pltpu.VMEM((tm + CONV_HALO, FF_CHUNK), F32),
                        pltpu.VMEM((tm, D_FF), BF16),
                        pltpu.VMEM((CONV_HALO, 2 * D_FF), F32)],
        compiler_params=_params(("arbitrary",)),
        name="post_prompt" if carry else "post_sample",
    )(x, o, u, uprev, pe, chist,
      wb['wpool'], row(s_pool), wb['wo'], row(ln1_g), row(ln1_b), wb['wup'], w_dw, row(b_dw), wb['wdown'],
      wb['wpg'], wb['wpe'], row(ln2_g), row(ln2_b))


def _layer_prompt(x, pe, wb, g_q, g_kv, small, *, tm_proj, tq, tk, hg, tm_post):
    s = x.shape[0]
    qt, k, vt, c, kr, u = _project_prompt(x, wb, g_q, g_kv, tm_proj, tk)
    ot = _attend_prompt(qt, k, vt, tq, tk, hg)
    per = tm_post // HALO
    y, clast = _post(x, ot, u, u.reshape(s // HALO, HALO, POOL_WIDTH),
                     lambda i: (jnp.maximum(i * per - 1, 0), 0, 0),
                     pe, jnp.zeros((1, CONV_W - 1, 2 * D_FF), F32), lambda i: (0, 0, 0),
                     wb, small, tm=tm_post, carry=True, o_transposed=True, pos0=0)
    return y, c, kr, u[s - POOL_HIST:], clast[0, CONV_HALO - (CONV_W - 1):]


def _layer_sample(x, pe, c_hist, kr_hist, pool_hist, conv_hist, wb, g_q, g_kv, small):
    nb, t, _ = x.shape
    n_past = c_hist.shape[1]
    xf = x.reshape(nb * t, D_MODEL)
    q, c, kr, u = _project_sample(xf, wb, g_q, g_kv, t, n_past)
    o = _attend_sample(q, c, kr, c_hist, kr_hist, wb, t)
    uprev = jnp.pad(pool_hist, ((0, 0), (HALO - POOL_HIST, 0), (0, 0)))
    y, clast = _post(xf, o, u, uprev, lambda i: (i, 0, 0), pe.reshape(nb * t, PLE_DIM),
                     conv_hist, lambda i: (i, 0, 0),
                     wb, small, tm=t, carry=False, o_transposed=False, pos0=n_past)
    u3 = u.reshape(nb, t, POOL_WIDTH)
    new_pool = jnp.concatenate([pool_hist, u3], axis=1)[:, -POOL_HIST:]
    return (y.reshape(nb, t, D_MODEL), c.reshape(nb, t, KV_LORA), kr.reshape(nb, t, QK_ROPE), new_pool,
            clast[:, CONV_HALO - (CONV_W - 1):])


def kernel(x_prompt, x_sample, cache_ckv, cache_krope, state_pool, state_ffn_conv, p_prompt, p_sample,
           w_in, g_q, w_q_b, g_kv, w_kv_b, w_pool, s_pool, w_o, ln1_g, ln1_b,
           w_up, w_dw, b_dw, w_down, w_pg, w_pe, ln2_g, ln2_b):
    assert x_prompt.shape[0] == 1 and w_in.shape[0] == DEPTH
    wb = _prep_weights(w_in[0], w_q_b[0], w_kv_b[0], w_pool[0], w_o[0], w_up[0], w_down[0], w_pg[0], w_pe[0])
    small = (s_pool[0], ln1_g[0], ln1_b[0], w_dw[0], b_dw[0], ln2_g[0], ln2_b[0])
    s = x_prompt.shape[1]
    tm_proj = min(512, s)
    tq = tk = min(512, s)
    yp, cp, krp, poolp, convp = _layer_prompt(x_prompt[0], p_prompt[0, 0], wb, g_q[0], g_kv[0], small,
                                              tm_proj=tm_proj, tq=tq, tk=tk, hg=ATTN_HEAD_GROUP, tm_post=min(256, s))
    ys, cs, krs, pools, convs = _layer_sample(x_sample, p_sample[0], cache_ckv[0], cache_krope[0],
                                              state_pool[0], state_ffn_conv[0], wb, g_q[0], g_kv[0], small)
    return (yp[None], ys, cp[None, None], krp[None, None], poolp[None, None], convp[None, None],
            cs[None], krs[None], pools[None], convs[None])
```

```python
import functools
import math

import jax
import jax.numpy as jnp
from jax import lax
from jax.experimental import pallas as pl
from jax.experimental.pallas import tpu as pltpu

F32 = jnp.float32
BF16 = jnp.bfloat16

D_MODEL = 1024
CHUNK = 64
CHUNK_SHIFT = 6
ATTN_HEAD_GROUP = 4
N_HEADS = 8
QK_NOPE = 64
QK_ROPE = 32
V_DIM = 64
V_ROWS = V_DIM + 16
Q_LORA = 256
KV_LORA = 256
ROPE_THETA = 10000.0
MLA_WIDTH = N_HEADS * V_DIM
POOL_WINDOWS = (2, 4, 8, 16)
POOL_GROUP_W = 128
POOL_WIDTH = D_MODEL - MLA_WIDTH
POOL_HIST = max(POOL_WINDOWS) - 1
D_FF = 2816
CONV_W = 3
PLE_DIM = 256
DEPTH = 1
ALPHA = (2 * DEPTH) ** 0.25
LN_EPS = 1e-5
RMS_EPS = 1e-6
NEG = -1e30
ATTN_SCALE = 1.0 / math.sqrt(QK_NOPE + QK_ROPE)

LANES = 128
HEAD_PAD = 128
ROPE_LO = QK_NOPE
ROPE_MID = QK_NOPE + QK_ROPE // 2
ROPE_HI = QK_NOPE + QK_ROPE
HALO = 16
CONV_HALO = 8
FF_CHUNK = 256
Q_SCALE = ATTN_SCALE * math.log2(math.e)
VMEM_LIMIT = 56 * 1024 * 1024

NT_DIMS = (((1,), (1,)), ((), ()))


def _rms(x, g):
    return x * lax.rsqrt(jnp.mean(x * x, axis=-1, keepdims=True) + RMS_EPS) * g


def _layer_norm(x, g, b):
    mu = jnp.mean(x, axis=-1, keepdims=True)
    xc = x - mu
    var = jnp.mean(xc * xc, axis=-1, keepdims=True)
    return xc * lax.rsqrt(var + LN_EPS) * g + b


def _token_rope_tables(pos_col, inv_lane):
    ang = pos_col.astype(F32) * inv_lane
    lane = lax.broadcasted_iota(jnp.int32, ang.shape, 1)
    first = lane < ROPE_MID
    sin = jnp.sin(ang)
    return jnp.cos(ang), jnp.where(first, -sin, sin), first


def _token_rope(x, cos, sin_signed, first):
    partner = jnp.where(first, pltpu.roll(x, LANES - QK_ROPE // 2, 1), pltpu.roll(x, QK_ROPE // 2, 1))
    return x * cos + partner * sin_signed


def _proj_prompt_kernel(x_ref, win_ref, gq_ref, gkv_ref, wqt_ref, wuk_ref, wuvt_ref, invl_ref, invs_ref,
                        qt_ref, k_ref, vt_ref, c_ref, kr_ref, u_ref, *, tm, tk, pos0):
    i = pl.program_id(0)
    z = jnp.dot(x_ref[...].astype(BF16), win_ref[...], preferred_element_type=F32)
    q_a = z[:, 0:Q_LORA]
    c_raw = z[:, Q_LORA:Q_LORA + KV_LORA]
    kr_pad = z[:, 512:640]
    u_ref[...] = z[:, 640:640 + POOL_WIDTH]

    qn = _rms(q_a, gq_ref[...]).astype(BF16)
    c = _rms(c_raw, gkv_ref[...])
    c_ref[...] = c
    cb = c.astype(BF16)

    pos_col = pos0 + i * tm + lax.broadcasted_iota(jnp.int32, (tm, LANES), 0)
    cos, sin_signed, first = _token_rope_tables(pos_col, invl_ref[...])
    kr_rot = _token_rope(kr_pad, cos, sin_signed, first)
    kr_ref[...] = kr_rot[:, ROPE_LO:ROPE_HI]

    k_nope = jnp.dot(cb, wuk_ref[...], preferred_element_type=F32)
    for h in range(N_HEADS):
        sl = slice(h * HEAD_PAD, (h + 1) * HEAD_PAD)
        k_ref[:, sl] = (k_nope[:, sl] + kr_rot).astype(BF16)

    vt = lax.dot_general(wuvt_ref[...], cb, NT_DIMS, preferred_element_type=F32)
    ones = jnp.ones((V_ROWS - V_DIM, tk), BF16)
    for s in range(tm // tk):
        for h in range(N_HEADS):
            vt_ref[s, h * V_ROWS:h * V_ROWS + V_DIM, :] = vt[h * V_DIM:(h + 1) * V_DIM, s * tk:(s + 1) * tk].astype(BF16)
            vt_ref[s, h * V_ROWS + V_DIM:(h + 1) * V_ROWS, :] = ones

    qt = lax.dot_general(wqt_ref[...], qn, NT_DIMS, preferred_element_type=F32)
    pos_row = pos0 + i * tm + lax.broadcasted_iota(jnp.int32, (QK_ROPE // 2, tm), 1)
    ang = pos_row.astype(F32) * invs_ref[...]
    cos_t, sin_t = jnp.cos(ang), jnp.sin(ang)
    for h in range(N_HEADS):
        b0 = h * HEAD_PAD
        qt_ref[b0:b0 + ROPE_LO, :] = (qt[b0:b0 + ROPE_LO] * Q_SCALE).astype(BF16)
        x1 = qt[b0 + ROPE_LO:b0 + ROPE_MID]
        x2 = qt[b0 + ROPE_MID:b0 + ROPE_HI]
        qt_ref[b0 + ROPE_LO:b0 + ROPE_MID, :] = ((x1 * cos_t - x2 * sin_t) * Q_SCALE).astype(BF16)
        qt_ref[b0 + ROPE_MID:b0 + ROPE_HI, :] = ((x2 * cos_t + x1 * sin_t) * Q_SCALE).astype(BF16)
        qt_ref[b0 + ROPE_HI:b0 + HEAD_PAD, :] = jnp.zeros((HEAD_PAD - ROPE_HI, tm), BF16)


def _proj_sample_kernel(x_ref, win_ref, gq_ref, gkv_ref, wq_ref, invl_ref,
                        q_ref, c_ref, kr_ref, u_ref, *, tm, seq_len, pos0):
    z = jnp.dot(x_ref[...].astype(BF16), win_ref[...], preferred_element_type=F32)
    q_a = z[:, 0:Q_LORA]
    c_raw = z[:, Q_LORA:Q_LORA + KV_LORA]
    kr_pad = z[:, 512:640]
    u_ref[...] = z[:, 640:640 + POOL_WIDTH]

    qn = _rms(q_a, gq_ref[...]).astype(BF16)
    c_ref[...] = _rms(c_raw, gkv_ref[...])

    row = lax.broadcasted_iota(jnp.int32, (tm, LANES), 0)
    pos_col = pos0 + (row & (seq_len - 1))
    cos, sin_signed, first = _token_rope_tables(pos_col, invl_ref[...])
    kr_ref[...] = _token_rope(kr_pad, cos, sin_signed, first)[:, ROPE_LO:ROPE_HI]

    q = jnp.dot(qn, wq_ref[...], preferred_element_type=F32)
    for h in range(N_HEADS):
        sl = slice(h * HEAD_PAD, (h + 1) * HEAD_PAD)
        q_ref[:, sl] = _token_rope(q[:, sl], cos, sin_signed, first) * Q_SCALE


def _attn_prompt_kernel(qt_ref, k_ref, vt_ref, o_ref, s_ref, m_ref, acc_ref, *, tq, hg):
    i = pl.program_id(1)
    m_ref[...] = jnp.full(m_ref.shape, NEG, F32)
    acc_ref[...] = jnp.zeros(acc_ref.shape, F32)

    def scores(j, h):
        return jnp.dot(k_ref[j, :, h * HEAD_PAD:(h + 1) * HEAD_PAD], qt_ref[h * HEAD_PAD:(h + 1) * HEAD_PAD, :],
                       preferred_element_type=F32)

    def consume(h, jv, j_next):
        s = s_ref[h]
        m_old = m_ref[h]
        m_new = jnp.maximum(m_old, jnp.max(s, axis=0, keepdims=True))
        alpha = jnp.exp2(m_old - m_new)
        p = jnp.exp2(s - m_new)
        m_ref[h] = m_new
        if j_next is not None:
            s_ref[h] = scores(j_next, h)
        acc_ref[h] = alpha * acc_ref[h] + jnp.dot(vt_ref[jv, h * V_ROWS:(h + 1) * V_ROWS, :], p.astype(BF16),
                                                  preferred_element_type=F32)

    kchunk = lax.broadcasted_iota(jnp.int32, (tq, tq), 0) >> CHUNK_SHIFT
    qchunk = lax.broadcasted_iota(jnp.int32, (tq, tq), 1) >> CHUNK_SHIFT
    visible = kchunk <= qchunk
    for h in range(hg):
        s_ref[h] = jnp.where(visible, scores(i, h), NEG)

    def body(j, c):
        jv = jnp.where(j == 0, i, j - 1)
        for h in range(hg):
            consume(h, jv, j)
        return c

    lax.fori_loop(0, i, body, 0)
    for h in range(hg):
        consume(h, jnp.maximum(i - 1, 0), None)
        o_ref[h * V_DIM:(h + 1) * V_DIM, :] = acc_ref[h, 0:V_DIM, :] / acc_ref[h, V_DIM:V_DIM + 1, :]


def _attn_sample_kernel(q_ref, cn_ref, krn_ref, ch_ref, krh_ref, wukt_ref, wuvbd_ref, o_ref, *, t, n_past):
    q = q_ref[...]
    qlat, qrope = [], []
    for h in range(N_HEADS):
        b0 = h * HEAD_PAD
        qn = q[:, b0:b0 + QK_NOPE].astype(BF16)
        qlat.append(jnp.dot(qn, wukt_ref[h], preferred_element_type=F32))
        qrope.append(q[:, b0 + ROPE_LO:b0 + ROPE_HI])
    ql = jnp.concatenate(qlat, axis=0).astype(BF16)
    qr = jnp.concatenate(qrope, axis=0).astype(BF16)

    chb = ch_ref[...].astype(BF16)
    krhb = krh_ref[...].astype(BF16)
    cnb = cn_ref[...].astype(BF16)
    krnb = krn_ref[...].astype(BF16)
    s_h = (lax.dot_general(ql, chb, NT_DIMS, preferred_element_type=F32)
           + lax.dot_general(qr, krhb, NT_DIMS, preferred_element_type=F32))
    s_n = (lax.dot_general(ql, cnb, NT_DIMS, preferred_element_type=F32)
           + lax.dot_general(qr, krnb, NT_DIMS, preferred_element_type=F32))

    rows = N_HEADS * t
    qchunk_h = (n_past + (lax.broadcasted_iota(jnp.int32, (rows, n_past), 0) & (t - 1))) // CHUNK
    s_h = jnp.where((lax.broadcasted_iota(jnp.int32, (rows, n_past), 1) // CHUNK) <= qchunk_h, s_h, NEG)
    qchunk_n = (n_past + (lax.broadcasted_iota(jnp.int32, (rows, t), 0) & (t - 1))) // CHUNK
    s_n = jnp.where(((n_past + lax.broadcasted_iota(jnp.int32, (rows, t), 1)) // CHUNK) <= qchunk_n, s_n, NEG)

    m = jnp.maximum(jnp.max(s_h, axis=-1, keepdims=True), jnp.max(s_n, axis=-1, keepdims=True))
    p_h = jnp.exp2(s_h - m)
    p_n = jnp.exp2(s_n - m)
    l = jnp.sum(p_h, axis=-1, keepdims=True) + jnp.sum(p_n, axis=-1, keepdims=True)
    olat = (jnp.dot(p_h.astype(BF16), chb, preferred_element_type=F32)
            + jnp.dot(p_n.astype(BF16), cnb, preferred_element_type=F32)) / l
    wide = jnp.concatenate([olat[h * t:(h + 1) * t] for h in range(N_HEADS)], axis=1).astype(BF16)
    o_ref[...] = jnp.dot(wide, wuvbd_ref[...], preferred_element_type=F32)


def _post_kernel(x_ref, o_ref, u_ref, uprev_ref, pe_ref, chist_ref,
                 wpool_ref, spool_ref, wo_ref, ln1g_ref, ln1b_ref, wup_ref, wdw_ref, bdw_ref, wdown_ref,
                 wpg_ref, wpe_ref, ln2g_ref, ln2b_ref,
                 y_ref, clast_ref,
                 ubuf, bufa, bufb, hbuf, cbuf, *, tm, carry, o_transposed, pos0):
    i = pl.program_id(0)

    if carry:
        @pl.when(i == 0)
        def _():
            cbuf[...] = jnp.zeros_like(cbuf)
            cbuf[CONV_HALO - (CONV_W - 1):CONV_HALO, :] = chist_ref[...]
    else:
        cbuf[...] = jnp.zeros_like(cbuf)
        cbuf[CONV_HALO - (CONV_W - 1):CONV_HALO, :] = chist_ref[...]

    u = u_ref[...]
    uprev = uprev_ref[...]
    if carry:
        uprev = jnp.where(i > 0, uprev, 0.0)
    ubuf[0:HALO, :] = uprev
    ubuf[HALO:HALO + tm, :] = u
    frame = lax.broadcasted_iota(jnp.int32, (tm, POOL_GROUP_W), 0) + (i * tm if carry else 0)
    pooled = []
    for g, w in enumerate(POOL_WINDOWS):
        cols = slice(g * POOL_GROUP_W, (g + 1) * POOL_GROUP_W)
        win = u[:, cols]
        for k in range(1, w):
            win = win + ubuf[HALO - k:HALO - k + tm, cols]
        cnt = jnp.minimum(w, pos0 + frame + 1).astype(F32)
        d = win / cnt - u[:, cols]
        yg = jnp.dot(d.astype(BF16), wpool_ref[g], preferred_element_type=F32) * spool_ref[:, cols]
        pooled.append(yg.astype(BF16))
    pooled = jnp.concatenate(pooled, axis=1)

    attn = o_ref[...].T if o_transposed else o_ref[...]
    mix = (jnp.dot(attn.astype(BF16), wo_ref[0:MLA_WIDTH, :], preferred_element_type=F32)
           + jnp.dot(pooled, wo_ref[MLA_WIDTH:D_MODEL, :], preferred_element_type=F32))
    x1 = _layer_norm(ALPHA * x_ref[...] + mix, ln1g_ref[...], ln1b_ref[...])
    x1b = x1.astype(BF16)

    def conv_chunk(cols, buf):
        up = jnp.dot(x1b, wup_ref[:, cols], preferred_element_type=F32)
        buf[0:CONV_HALO, :] = cbuf[:, cols]
        buf[CONV_HALO:CONV_HALO + tm, :] = up
        cbuf[:, cols] = buf[tm:tm + CONV_HALO, :]
        return (buf[CONV_HALO - 2:CONV_HALO - 2 + tm, :] * wdw_ref[0:1, cols]
                + buf[CONV_HALO - 1:CONV_HALO - 1 + tm, :] * wdw_ref[1:2, cols]
                + up * wdw_ref[2:3, cols] + bdw_ref[:, cols])

    for j in range(D_FF // FF_CHUNK):
        a = conv_chunk(slice(j * FF_CHUNK, (j + 1) * FF_CHUNK), bufa)
        b = conv_chunk(slice(D_FF + j * FF_CHUNK, D_FF + (j + 1) * FF_CHUNK), bufb)
        hbuf[:, j * FF_CHUNK:(j + 1) * FF_CHUNK] = (a * jax.nn.sigmoid(a) * b).astype(BF16)
    ffn = jnp.dot(hbuf[...], wdown_ref[...], preferred_element_type=F32)
    clast_ref[...] = cbuf[...]

    ple = (jax.nn.sigmoid(jnp.dot(x1b, wpg_ref[...], preferred_element_type=F32))
           * jnp.dot(pe_ref[...].astype(BF16), wpe_ref[...], preferred_element_type=F32))
    y_ref[...] = _layer_norm(ALPHA * x1 + ffn + ple, ln2g_ref[...], ln2b_ref[...])


def _const_spec(shape):
    nd = len(shape)
    return pl.BlockSpec(shape, lambda *_: (0,) * nd)


def _params(semantics):
    return pltpu.CompilerParams(dimension_semantics=semantics, vmem_limit_bytes=VMEM_LIMIT)


def _pad_heads(w, per_head, used):
    k = w.shape[0]
    w = w.reshape(k, N_HEADS, per_head)[:, :, :used]
    return jnp.pad(w, ((0, 0), (0, 0), (0, HEAD_PAD - used))).reshape(k, N_HEADS * HEAD_PAD)


def _prep_weights(w_in, w_q_b, w_kv_b, w_pool, w_o, w_up, w_down, w_pg, w_pe):
    w_kr = jnp.pad(w_in[:, 512:512 + QK_ROPE], ((0, 0), (ROPE_LO, LANES - ROPE_HI)))
    win = jnp.concatenate([w_in[:, :512], w_kr, w_in[:, 512 + QK_ROPE:]], axis=1).astype(BF16)
    wq_pad = _pad_heads(w_q_b, QK_NOPE + QK_ROPE, QK_NOPE + QK_ROPE).astype(BF16)
    w_kv = w_kv_b.reshape(KV_LORA, N_HEADS, QK_NOPE + V_DIM)
    w_uk, w_uv = w_kv[..., :QK_NOPE], w_kv[..., QK_NOPE:]
    wuk_pad = jnp.pad(w_uk, ((0, 0), (0, 0), (0, HEAD_PAD - QK_NOPE))).reshape(KV_LORA, -1).astype(BF16)
    wuvt = w_uv.reshape(KV_LORA, N_HEADS * V_DIM).T.astype(BF16)
    wukt = jnp.transpose(w_uk, (1, 2, 0)).astype(BF16)
    eye = jnp.eye(N_HEADS, dtype=w_uv.dtype)
    wuv_bd = jnp.einsum('lhv,hg->hlgv', w_uv, eye).reshape(N_HEADS * KV_LORA, N_HEADS * V_DIM).astype(BF16)
    return dict(win=win, wq_pad=wq_pad, wqt=wq_pad.T, wuk_pad=wuk_pad, wuvt=wuvt, wukt=wukt, wuv_bd=wuv_bd,
                wpool=w_pool.astype(BF16), wo=w_o.astype(BF16), wup=w_up.astype(BF16),
                wdown=w_down.astype(BF16), wpg=w_pg.astype(BF16), wpe=w_pe.astype(BF16))


def _rope_inv():
    inv = 1.0 / (ROPE_THETA ** (jnp.arange(0, QK_ROPE, 2, dtype=F32) / QK_ROPE))
    inv_lane = jnp.zeros((1, LANES), F32).at[0, ROPE_LO:ROPE_HI].set(jnp.concatenate([inv, inv]))
    return inv_lane, inv[:, None]


def _project_prompt(x, wb, g_q, g_kv, tm, tk):
    s = x.shape[0]
    inv_lane, inv_sub = _rope_inv()
    n = s // tm
    outs = pl.pallas_call(
        functools.partial(_proj_prompt_kernel, tm=tm, tk=tk, pos0=0),
        grid=(n,),
        in_specs=[pl.BlockSpec((tm, D_MODEL), lambda i: (i, 0)),
                  _const_spec(wb['win'].shape), _const_spec((1, Q_LORA)), _const_spec((1, KV_LORA)),
                  _const_spec(wb['wqt'].shape), _const_spec(wb['wuk_pad'].shape), _const_spec(wb['wuvt'].shape),
                  _const_spec((1, LANES)), _const_spec((QK_ROPE // 2, 1))],
        out_specs=[pl.BlockSpec((N_HEADS * HEAD_PAD, tm), lambda i: (0, i)),
                   pl.BlockSpec((tm, N_HEADS * HEAD_PAD), lambda i: (i, 0)),
                   pl.BlockSpec((tm // tk, N_HEADS * V_ROWS, tk), lambda i: (i, 0, 0)),
                   pl.BlockSpec((tm, KV_LORA), lambda i: (i, 0)),
                   pl.BlockSpec((tm, QK_ROPE), lambda i: (i, 0)),
                   pl.BlockSpec((tm, POOL_WIDTH), lambda i: (i, 0))],
        out_shape=[jax.ShapeDtypeStruct((N_HEADS * HEAD_PAD, s), BF16),
                   jax.ShapeDtypeStruct((s, N_HEADS * HEAD_PAD), BF16),
                   jax.ShapeDtypeStruct((s // tk, N_HEADS * V_ROWS, tk), BF16),
                   jax.ShapeDtypeStruct((s, KV_LORA), F32),
                   jax.ShapeDtypeStruct((s, QK_ROPE), F32),
                   jax.ShapeDtypeStruct((s, POOL_WIDTH), F32)],
        compiler_params=_params(("arbitrary",)),
        name="proj_prompt",
    )(x, wb['win'], g_q.reshape(1, -1), g_kv.reshape(1, -1), wb['wqt'], wb['wuk_pad'], wb['wuvt'],
      inv_lane, inv_sub)
    return outs


def _project_sample(x, wb, g_q, g_kv, seq_len, pos0):
    rows = x.shape[0]
    assert seq_len & (seq_len - 1) == 0
    inv_lane, _ = _rope_inv()
    return pl.pallas_call(
        functools.partial(_proj_sample_kernel, tm=rows, seq_len=seq_len, pos0=pos0),
        grid=(1,),
        in_specs=[_const_spec((rows, D_MODEL)), _const_spec(wb['win'].shape), _const_spec((1, Q_LORA)),
                  _const_spec((1, KV_LORA)), _const_spec(wb['wq_pad'].shape), _const_spec((1, LANES))],
        out_specs=[_const_spec((rows, N_HEADS * HEAD_PAD)), _const_spec((rows, KV_LORA)),
                   _const_spec((rows, QK_ROPE)), _const_spec((rows, POOL_WIDTH))],
        out_shape=[jax.ShapeDtypeStruct((rows, N_HEADS * HEAD_PAD), F32),
                   jax.ShapeDtypeStruct((rows, KV_LORA), F32),
                   jax.ShapeDtypeStruct((rows, QK_ROPE), F32),
                   jax.ShapeDtypeStruct((rows, POOL_WIDTH), F32)],
        compiler_params=_params(("arbitrary",)),
        name="proj_sample",
    )(x, wb['win'], g_q.reshape(1, -1), g_kv.reshape(1, -1), wb['wq_pad'], inv_lane)


def _attend_prompt(qt, k, vt, tq, tk, hg):
    s = k.shape[0]
    nkv = s // tk
    k3 = k.reshape(nkv, tk, N_HEADS * HEAD_PAD)
    resident = pl.Buffered(1)
    assert tq == tk
    return pl.pallas_call(
        functools.partial(_attn_prompt_kernel, tq=tq, hg=hg),
        grid=(N_HEADS // hg, s // tq),
        in_specs=[pl.BlockSpec((hg * HEAD_PAD, tq), lambda g, i: (g, i)),
                  pl.BlockSpec((nkv, tk, hg * HEAD_PAD), lambda g, i: (0, 0, g), pipeline_mode=resident),
                  pl.BlockSpec((nkv, hg * V_ROWS, tk), lambda g, i: (0, g, 0), pipeline_mode=resident)],
        out_specs=pl.BlockSpec((hg * V_DIM, tq), lambda g, i: (g, i)),
        out_shape=jax.ShapeDtypeStruct((MLA_WIDTH, s), F32),
        scratch_shapes=[pltpu.VMEM((hg, tk, tq), F32), pltpu.VMEM((hg, 1, tq), F32),
                        pltpu.VMEM((hg, V_ROWS, tq), F32)],
        compiler_params=_params(("arbitrary", "arbitrary")),
        name="attn_prompt",
    )(qt, k3, vt)


def _attend_sample(q, c_new, kr_new, c_hist, kr_hist, wb, t):
    nb, n_past, _ = c_hist.shape
    assert t & (t - 1) == 0
    return pl.pallas_call(
        functools.partial(_attn_sample_kernel, t=t, n_past=n_past),
        grid=(nb,),
        in_specs=[pl.BlockSpec((t, N_HEADS * HEAD_PAD), lambda b: (b, 0)),
                  pl.BlockSpec((t, KV_LORA), lambda b: (b, 0)),
                  pl.BlockSpec((t, QK_ROPE), lambda b: (b, 0)),
                  pl.BlockSpec((None, n_past, KV_LORA), lambda b: (b, 0, 0)),
                  pl.BlockSpec((None, n_past, QK_ROPE), lambda b: (b, 0, 0)),
                  _const_spec(wb['wukt'].shape), _const_spec(wb['wuv_bd'].shape)],
        out_specs=pl.BlockSpec((t, MLA_WIDTH), lambda b: (b, 0)),
        out_shape=jax.ShapeDtypeStruct((nb * t, MLA_WIDTH), F32),
        compiler_params=_params(("arbitrary",)),
        name="attn_sample",
    )(q, c_new, kr_new, c_hist, kr_hist, wb['wukt'], wb['wuv_bd'])


def _post(x, o, u, uprev, uprev_map, pe, chist, chist_map, wb, small, *, tm, carry, o_transposed, pos0):
    rows = x.shape[0]
    n = rows // tm
    s_pool, ln1_g, ln1_b, w_dw, b_dw, ln2_g, ln2_b = small
    o_spec = (pl.BlockSpec((MLA_WIDTH, tm), lambda i: (0, i)) if o_transposed
              else pl.BlockSpec((tm, MLA_WIDTH), lambda i: (i, 0)))
    clast_spec = (pl.BlockSpec((None, CONV_HALO, 2 * D_FF), lambda i: (0, 0, 0)) if carry
                  else pl.BlockSpec((None, CONV_HALO, 2 * D_FF), lambda i: (i, 0, 0)))
    row = lambda v: v.reshape(1, -1)
    return pl.pallas_call(
        functools.partial(_post_kernel, tm=tm, carry=carry, o_transposed=o_transposed, pos0=pos0),
        grid=(n,),
        in_specs=[pl.BlockSpec((tm, D_MODEL), lambda i: (i, 0)),
                  o_spec,
                  pl.BlockSpec((tm, POOL_WIDTH), lambda i: (i, 0)),
                  pl.BlockSpec((None, HALO, POOL_WIDTH), uprev_map),
                  pl.BlockSpec((tm, PLE_DIM), lambda i: (i, 0)),
                  pl.BlockSpec((None, CONV_W - 1, 2 * D_FF), chist_map),
                  _const_spec(wb['wpool'].shape), _const_spec((1, POOL_WIDTH)), _const_spec(wb['wo'].shape),
                  _const_spec((1, D_MODEL)), _const_spec((1, D_MODEL)),
                  _const_spec(wb['wup'].shape), _const_spec((CONV_W, 2 * D_FF)), _const_spec((1, 2 * D_FF)),
                  _const_spec(wb['wdown'].shape), _const_spec(wb['wpg'].shape), _const_spec(wb['wpe'].shape),
                  _const_spec((1, D_MODEL)), _const_spec((1, D_MODEL))],
        out_specs=[pl.BlockSpec((tm, D_MODEL), lambda i: (i, 0)), clast_spec],
        out_shape=[jax.ShapeDtypeStruct((rows, D_MODEL), F32),
                   jax.ShapeDtypeStruct((1 if carry else n, CONV_HALO, 2 * D_FF), F32)],
        scratch_shapes=[pltpu.VMEM((tm + HALO, POOL_WIDTH), F32),
                        pltpu.VMEM((tm + CONV_HALO, FF_CHUNK), F32),
                        pltpu.VMEM((tm + CONV_HALO, FF_CHUNK), F32),
                        pltpu.VMEM((tm, D_FF), BF16),
                        pltpu.VMEM((CONV_HALO, 2 * D_FF), F32)],
        compiler_params=_params(("arbitrary",)),
        name="post_prompt" if carry else "post_sample",
    )(x, o, u, uprev, pe, chist,
      wb['wpool'], row(s_pool), wb['wo'], row(ln1_g), row(ln1_b), wb['wup'], w_dw, row(b_dw), wb['wdown'],
      wb['wpg'], wb['wpe'], row(ln2_g), row(ln2_b))


def _layer_prompt(x, pe, wb, g_q, g_kv, small, *, tm_proj, tq, tk, hg, tm_post):
    s = x.shape[0]
    qt, k, vt, c, kr, u = _project_prompt(x, wb, g_q, g_kv, tm_proj, tk)
    ot = _attend_prompt(qt, k, vt, tq, tk, hg)
    per = tm_post // HALO
    y, clast = _post(x, ot, u, u.reshape(s // HALO, HALO, POOL_WIDTH),
                     lambda i: (jnp.maximum(i * per - 1, 0), 0, 0),
                     pe, jnp.zeros((1, CONV_W - 1, 2 * D_FF), F32), lambda i: (0, 0, 0),
                     wb, small, tm=tm_post, carry=True, o_transposed=True, pos0=0)
    return y, c, kr, u[s - POOL_HIST:], clast[0, CONV_HALO - (CONV_W - 1):]


def _layer_sample(x, pe, c_hist, kr_hist, pool_hist, conv_hist, wb, g_q, g_kv, small):
    nb, t, _ = x.shape
    n_past = c_hist.shape[1]
    xf = x.reshape(nb * t, D_MODEL)
    q, c, kr, u = _project_sample(xf, wb, g_q, g_kv, t, n_past)
    o = _attend_sample(q, c, kr, c_hist, kr_hist, wb, t)
    uprev = jnp.pad(pool_hist, ((0, 0), (HALO - POOL_HIST, 0), (0, 0)))
    y, clast = _post(xf, o, u, uprev, lambda i: (i, 0, 0), pe.reshape(nb * t, PLE_DIM),
                     conv_hist, lambda i: (i, 0, 0),
                     wb, small, tm=t, carry=False, o_transposed=False, pos0=n_past)
    u3 = u.reshape(nb, t, POOL_WIDTH)
    new_pool = jnp.concatenate([pool_hist, u3], axis=1)[:, -POOL_HIST:]
    return (y.reshape(nb, t, D_MODEL), c.reshape(nb, t, KV_LORA), kr.reshape(nb, t, QK_ROPE), new_pool,
            clast[:, CONV_HALO - (CONV_W - 1):])


def kernel(x_prompt, x_sample, cache_ckv, cache_krope, state_pool, state_ffn_conv, p_prompt, p_sample,
           w_in, g_q, w_q_b, g_kv, w_kv_b, w_pool, s_pool, w_o, ln1_g, ln1_b,
           w_up, w_dw, b_dw, w_down, w_pg, w_pe, ln2_g, ln2_b):
    assert x_prompt.shape[0] == 1 and w_in.shape[0] == DEPTH
    wb = _prep_weights(w_in[0], w_q_b[0], w_kv_b[0], w_pool[0], w_o[0], w_up[0], w_down[0], w_pg[0], w_pe[0])
    small = (s_pool[0], ln1_g[0], ln1_b[0], w_dw[0], b_dw[0], ln2_g[0], ln2_b[0])
    s = x_prompt.shape[1]
    tm_proj = min(512, s)
    tq = tk = min(512, s)
    yp, cp, krp, poolp, convp = _layer_prompt(x_prompt[0], p_prompt[0, 0], wb, g_q[0], g_kv[0], small,
                                              tm_proj=tm_proj, tq=tq, tk=tk, hg=ATTN_HEAD_GROUP, tm_post=min(256, s))
    ys, cs, krs, pools, convs = _layer_sample(x_sample, p_sample[0], cache_ckv[0], cache_krope[0],
                                              state_pool[0], state_ffn_conv[0], wb, g_q[0], g_kv[0], small)
    return (yp[None], ys, cp[None, None], krp[None, None], poolp[None, None], convp[None, None],
            cs[None], krs[None], pools[None], convs[None])
```

```python
import functools
import math

import jax
import jax.numpy as jnp
from jax import lax
from jax.experimental import pallas as pl
from jax.experimental.pallas import tpu as pltpu

F32 = jnp.float32
BF16 = jnp.bfloat16

D_MODEL = 1024
CHUNK = 64
CHUNK_SHIFT = 6
ATTN_HEAD_GROUP = 4
N_HEADS = 8
QK_NOPE = 64
QK_ROPE = 32
V_DIM = 64
V_ROWS = V_DIM + 16
Q_LORA = 256
KV_LORA = 256
ROPE_THETA = 10000.0
MLA_WIDTH = N_HEADS * V_DIM
POOL_WINDOWS = (2, 4, 8, 16)
POOL_GROUP_W = 128
POOL_WIDTH = D_MODEL - MLA_WIDTH
POOL_HIST = max(POOL_WINDOWS) - 1
D_FF = 2816
CONV_W = 3
PLE_DIM = 256
DEPTH = 1
ALPHA = (2 * DEPTH) ** 0.25
LN_EPS = 1e-5
RMS_EPS = 1e-6
NEG = -1e30
ATTN_SCALE = 1.0 / math.sqrt(QK_NOPE + QK_ROPE)

LANES = 128
HEAD_PAD = 128
ROPE_LO = QK_NOPE
ROPE_MID = QK_NOPE + QK_ROPE // 2
ROPE_HI = QK_NOPE + QK_ROPE
HALO = 16
CONV_HALO = 8
FF_CHUNK = 256
Q_SCALE = ATTN_SCALE * math.log2(math.e)
VMEM_LIMIT = 56 * 1024 * 1024

NT_DIMS = (((1,), (1,)), ((), ()))


def _rms(x, g):
    return x * lax.rsqrt(jnp.mean(x * x, axis=-1, keepdims=True) + RMS_EPS) * g


def _layer_norm(x, g, b):
    mu = jnp.mean(x, axis=-1, keepdims=True)
    xc = x - mu
    var = jnp.mean(xc * xc, axis=-1, keepdims=True)
    return xc * lax.rsqrt(var + LN_EPS) * g + b


def _token_rope_tables(pos_col, inv_lane):
    ang = pos_col.astype(F32) * inv_lane
    lane = lax.broadcasted_iota(jnp.int32, ang.shape, 1)
    first = lane < ROPE_MID
    sin = jnp.sin(ang)
    return jnp.cos(ang), jnp.where(first, -sin, sin), first


def _token_rope(x, cos, sin_signed, first):
    partner = jnp.where(first, pltpu.roll(x, LANES - QK_ROPE // 2, 1), pltpu.roll(x, QK_ROPE // 2, 1))
    return x * cos + partner * sin_signed


def _proj_prompt_kernel(x_ref, win_ref, gq_ref, gkv_ref, wqt_ref, wuk_ref, wuvt_ref, invs_ref,
                        qt_ref, k_ref, vt_ref, c_ref, kr_ref, u_ref, *, tm, tk, pos0):
    i = pl.program_id(0)
    z = jnp.dot(x_ref[...].astype(BF16), win_ref[...], preferred_element_type=F32)
    q_a = z[:, 0:Q_LORA]
    c_raw = z[:, Q_LORA:Q_LORA + KV_LORA]
    kr_pad = z[:, 512:640]
    u_ref[...] = z[:, 640:640 + POOL_WIDTH]

    qn = _rms(q_a, gq_ref[...]).astype(BF16)
    c = _rms(c_raw, gkv_ref[...])
    c_ref[...] = c
    cb = c.astype(BF16)

    pos_row = pos0 + i * tm + lax.broadcasted_iota(jnp.int32, (QK_ROPE // 2, tm), 1)
    ang = pos_row.astype(F32) * invs_ref[...]
    cos_t, sin_t = jnp.cos(ang), jnp.sin(ang)
    lo, hi = (ROPE_LO, tm), (HEAD_PAD - ROPE_HI, tm)
    cos = jnp.concatenate([jnp.ones(lo, F32), cos_t, cos_t, jnp.ones(hi, F32)], axis=0).T
    sin_signed = jnp.concatenate([jnp.zeros(lo, F32), -sin_t, sin_t, jnp.zeros(hi, F32)], axis=0).T
    first = lax.broadcasted_iota(jnp.int32, (tm, LANES), 1) < ROPE_MID
    kr_rot = _token_rope(kr_pad, cos, sin_signed, first)
    kr_ref[...] = kr_rot.T[ROPE_LO:ROPE_HI, :]

    k_nope = jnp.dot(cb, wuk_ref[...], preferred_element_type=F32)
    for h in range(N_HEADS):
        sl = slice(h * HEAD_PAD, (h + 1) * HEAD_PAD)
        k_ref[:, sl] = (k_nope[:, sl] + kr_rot).astype(BF16)

    vt = lax.dot_general(wuvt_ref[...], cb, NT_DIMS, preferred_element_type=F32)
    ones = jnp.ones((V_ROWS - V_DIM, tk), BF16)
    for s in range(tm // tk):
        for h in range(N_HEADS):
            vt_ref[s, h * V_ROWS:h * V_ROWS + V_DIM, :] = vt[h * V_DIM:(h + 1) * V_DIM, s * tk:(s + 1) * tk].astype(BF16)
            vt_ref[s, h * V_ROWS + V_DIM:(h + 1) * V_ROWS, :] = ones

    qt = lax.dot_general(wqt_ref[...], qn, NT_DIMS, preferred_element_type=F32)
    for h in range(N_HEADS):
        b0 = h * HEAD_PAD
        qt_ref[b0:b0 + ROPE_LO, :] = (qt[b0:b0 + ROPE_LO] * Q_SCALE).astype(BF16)
        x1 = qt[b0 + ROPE_LO:b0 + ROPE_MID]
        x2 = qt[b0 + ROPE_MID:b0 + ROPE_HI]
        qt_ref[b0 + ROPE_LO:b0 + ROPE_MID, :] = ((x1 * cos_t - x2 * sin_t) * Q_SCALE).astype(BF16)
        qt_ref[b0 + ROPE_MID:b0 + ROPE_HI, :] = ((x2 * cos_t + x1 * sin_t) * Q_SCALE).astype(BF16)
        qt_ref[b0 + ROPE_HI:b0 + HEAD_PAD, :] = jnp.zeros((HEAD_PAD - ROPE_HI, tm), BF16)


def _proj_sample_kernel(x_ref, win_ref, gq_ref, gkv_ref, wq_ref, invl_ref,
                        q_ref, c_ref, kr_ref, u_ref, *, tm, seq_len, pos0):
    z = jnp.dot(x_ref[...].astype(BF16), win_ref[...], preferred_element_type=F32)
    q_a = z[:, 0:Q_LORA]
    c_raw = z[:, Q_LORA:Q_LORA + KV_LORA]
    kr_pad = z[:, 512:640]
    u_ref[...] = z[:, 640:640 + POOL_WIDTH]

    qn = _rms(q_a, gq_ref[...]).astype(BF16)
    c_ref[...] = _rms(c_raw, gkv_ref[...])

    row = lax.broadcasted_iota(jnp.int32, (tm, LANES), 0)
    pos_col = pos0 + (row & (seq_len - 1))
    cos, sin_signed, first = _token_rope_tables(pos_col, invl_ref[...])
    kr_ref[...] = _token_rope(kr_pad, cos, sin_signed, first)[:, ROPE_LO:ROPE_HI]

    q = jnp.dot(qn, wq_ref[...], preferred_element_type=F32)
    for h in range(N_HEADS):
        sl = slice(h * HEAD_PAD, (h + 1) * HEAD_PAD)
        q_ref[:, sl] = _token_rope(q[:, sl], cos, sin_signed, first) * Q_SCALE


def _attn_prompt_kernel(qt_ref, k_ref, vt_ref, o_ref, sa_ref, sb_ref, m_ref, acc_ref, *, tq, hg):
    i = pl.program_id(1)
    m_ref[...] = jnp.full(m_ref.shape, NEG, F32)
    acc_ref[...] = jnp.zeros(acc_ref.shape, F32)

    def scores(j, h):
        return jnp.dot(k_ref[j, :, h * HEAD_PAD:(h + 1) * HEAD_PAD], qt_ref[h * HEAD_PAD:(h + 1) * HEAD_PAD, :],
                       preferred_element_type=F32)

    def step(cur_ref, jv, nxt_ref, j_next):
        for h in range(hg):
            s = cur_ref[h]
            m_old = m_ref[h]
            m_new = jnp.maximum(m_old, jnp.max(s, axis=0, keepdims=True))
            alpha = jnp.exp2(m_old - m_new)
            p = jnp.exp2(s - m_new)
            m_ref[h] = m_new
            if nxt_ref is not None:
                nxt_ref[h] = scores(j_next, h)
            acc_ref[h] = alpha * acc_ref[h] + jnp.dot(vt_ref[jv, h * V_ROWS:(h + 1) * V_ROWS, :], p.astype(BF16),
                                                      preferred_element_type=F32)

    def finish():
        for h in range(hg):
            o_ref[h * V_DIM:(h + 1) * V_DIM, :] = acc_ref[h, 0:V_DIM, :] / acc_ref[h, V_DIM:V_DIM + 1, :]

    kchunk = lax.broadcasted_iota(jnp.int32, (tq, tq), 0) >> CHUNK_SHIFT
    qchunk = lax.broadcasted_iota(jnp.int32, (tq, tq), 1) >> CHUNK_SHIFT
    visible = kchunk <= qchunk
    for h in range(hg):
        sa_ref[h] = jnp.where(visible, scores(i, h), NEG)

    def pair(t, c):
        step(sa_ref, jnp.where(t == 0, i, 2 * t - 1), sb_ref, 2 * t)
        step(sb_ref, 2 * t, sa_ref, 2 * t + 1)
        return c

    lax.fori_loop(0, i // 2, pair, 0)

    @pl.when(i % 2 == 0)
    def _():
        step(sa_ref, jnp.maximum(i - 1, 0), None, None)
        finish()

    @pl.when(i % 2 == 1)
    def _():
        step(sa_ref, jnp.where(i == 1, i, i - 2), sb_ref, i - 1)
        step(sb_ref, i - 1, None, None)
        finish()


def _attn_sample_kernel(q_ref, cn_ref, krn_ref, ch_ref, krht_ref, wukt_ref, wuvbd_ref, o_ref, *, t, n_past):
    q = q_ref[...]
    qlat, qrope = [], []
    for h in range(N_HEADS):
        b0 = h * HEAD_PAD
        qn = q[:, b0:b0 + QK_NOPE].astype(BF16)
        qlat.append(jnp.dot(qn, wukt_ref[h], preferred_element_type=F32))
        qrope.append(q[:, b0 + ROPE_LO:b0 + ROPE_HI])
    ql = jnp.concatenate(qlat, axis=0).astype(BF16)
    qr = jnp.concatenate(qrope, axis=0).astype(BF16)

    chb = ch_ref[...].astype(BF16)
    krhb = krht_ref[...].astype(BF16)
    cnb = cn_ref[...].astype(BF16)
    krnb = krn_ref[...].astype(BF16)
    s_h = (lax.dot_general(ql, chb, NT_DIMS, preferred_element_type=F32)
           + jnp.dot(qr, krhb, preferred_element_type=F32))
    s_n = (lax.dot_general(ql, cnb, NT_DIMS, preferred_element_type=F32)
           + lax.dot_general(qr, krnb, NT_DIMS, preferred_element_type=F32))

    rows = N_HEADS * t
    qchunk = (n_past + (lax.broadcasted_iota(jnp.int32, (rows, 1), 0) & (t - 1))) >> CHUNK_SHIFT
    s_h = jnp.where((lax.broadcasted_iota(jnp.int32, (rows, n_past), 1) >> CHUNK_SHIFT) <= qchunk, s_h, NEG)
    s_n = jnp.where(((n_past + lax.broadcasted_iota(jnp.int32, (rows, t), 1)) >> CHUNK_SHIFT) <= qchunk, s_n, NEG)

    m = jnp.maximum(jnp.max(s_h, axis=-1, keepdims=True), jnp.max(s_n, axis=-1, keepdims=True))
    p_h = jnp.exp2(s_h - m)
    p_n = jnp.exp2(s_n - m)
    l = jnp.sum(p_h, axis=-1, keepdims=True) + jnp.sum(p_n, axis=-1, keepdims=True)
    olat = (jnp.dot(p_h.astype(BF16), chb, preferred_element_type=F32)
            + jnp.dot(p_n.astype(BF16), cnb, preferred_element_type=F32)) / l
    wide = jnp.concatenate([olat[h * t:(h + 1) * t] for h in range(N_HEADS)], axis=1).astype(BF16)
    o_ref[...] = jnp.dot(wide, wuvbd_ref[...], preferred_element_type=F32)


def _post_kernel(x_ref, o_ref, u_ref, uprev_ref, pe_ref, chist_ref,
                 wpool_ref, spool_ref, wo_ref, ln1g_ref, ln1b_ref, wup_ref, wdw_ref, bdw_ref, wdown_ref,
                 wpg_ref, wpe_ref, ln2g_ref, ln2b_ref,
                 y_ref, clast_ref,
                 ubuf, bufa, bufb, hbuf, cbuf, *, nseq, seq_len, carry, o_transposed, pos0):
    i = pl.program_id(0)
    tm = nseq * seq_len
    hist_rows = slice(CONV_HALO - (CONV_W - 1), CONV_HALO)

    def init_conv_history():
        cbuf[...] = jnp.zeros_like(cbuf)
        cbuf[:, hist_rows, :] = chist_ref[...]

    if carry:
        pl.when(i == 0)(init_conv_history)
    else:
        init_conv_history()

    def gather(buf, halo, shift, cols):
        stride = halo + seq_len
        pieces = [buf[q * stride + halo - shift:q * stride + halo - shift + seq_len, cols] for q in range(nseq)]
        return pieces[0] if nseq == 1 else jnp.concatenate(pieces, axis=0)

    u = u_ref[...]
    for q in range(nseq):
        uprev = uprev_ref[q]
        if carry:
            uprev = jnp.where(i > 0, uprev, 0.0)
        ubuf[q * (HALO + seq_len):q * (HALO + seq_len) + HALO, :] = uprev
        ubuf[q * (HALO + seq_len) + HALO:(q + 1) * (HALO + seq_len), :] = u[q * seq_len:(q + 1) * seq_len]
    row = lax.broadcasted_iota(jnp.int32, (tm, POOL_GROUP_W), 0)
    frame = row + i * tm if carry else row & (seq_len - 1)
    pooled = []
    for g, w in enumerate(POOL_WINDOWS):
        cols = slice(g * POOL_GROUP_W, (g + 1) * POOL_GROUP_W)
        win = u[:, cols]
        for k in range(1, w):
            win = win + gather(ubuf, HALO, k, cols)
        cnt = jnp.minimum(w, pos0 + frame + 1).astype(F32)
        d = win / cnt - u[:, cols]
        yg = jnp.dot(d.astype(BF16), wpool_ref[g], preferred_element_type=F32) * spool_ref[:, cols]
        pooled.append(yg.astype(BF16))
    pooled = jnp.concatenate(pooled, axis=1)

    attn = o_ref[...].T if o_transposed else o_ref[...]
    mix = (jnp.dot(attn.astype(BF16), wo_ref[0:MLA_WIDTH, :], preferred_element_type=F32)
           + jnp.dot(pooled, wo_ref[MLA_WIDTH:D_MODEL, :], preferred_element_type=F32))
    x1 = _layer_norm(ALPHA * x_ref[...] + mix, ln1g_ref[...], ln1b_ref[...])
    x1b = x1.astype(BF16)

    def conv_chunk(cols, buf):
        up = jnp.dot(x1b, wup_ref[:, cols], preferred_element_type=F32)
        stride = CONV_HALO + seq_len
        for q in range(nseq):
            buf[q * stride:q * stride + CONV_HALO, :] = cbuf[q, :, cols]
            buf[q * stride + CONV_HALO:(q + 1) * stride, :] = up[q * seq_len:(q + 1) * seq_len]
            cbuf[q, :, cols] = buf[q * stride + seq_len:(q + 1) * stride, :]
        everything = slice(None)
        return (gather(buf, CONV_HALO, 2, everything) * wdw_ref[0:1, cols]
                + gather(buf, CONV_HALO, 1, everything) * wdw_ref[1:2, cols]
                + up * wdw_ref[2:3, cols] + bdw_ref[:, cols])

    for j in range(D_FF // FF_CHUNK):
        a = conv_chunk(slice(j * FF_CHUNK, (j + 1) * FF_CHUNK), bufa)
        b = conv_chunk(slice(D_FF + j * FF_CHUNK, D_FF + (j + 1) * FF_CHUNK), bufb)
        hbuf[:, j * FF_CHUNK:(j + 1) * FF_CHUNK] = (a * jax.nn.sigmoid(a) * b).astype(BF16)
    ffn = jnp.dot(hbuf[...], wdown_ref[...], preferred_element_type=F32)
    clast_ref[...] = cbuf[...]

    ple = (jax.nn.sigmoid(jnp.dot(x1b, wpg_ref[...], preferred_element_type=F32))
           * jnp.dot(pe_ref[...].astype(BF16), wpe_ref[...], preferred_element_type=F32))
    y_ref[...] = _layer_norm(ALPHA * x1 + ffn + ple, ln2g_ref[...], ln2b_ref[...])


def _const_spec(shape):
    nd = len(shape)
    return pl.BlockSpec(shape, lambda *_: (0,) * nd)


def _params(semantics):
    return pltpu.CompilerParams(dimension_semantics=semantics, vmem_limit_bytes=VMEM_LIMIT)


def _pad_heads(w, per_head, used):
    k = w.shape[0]
    w = w.reshape(k, N_HEADS, per_head)[:, :, :used]
    return jnp.pad(w, ((0, 0), (0, 0), (0, HEAD_PAD - used))).reshape(k, N_HEADS * HEAD_PAD)


def _prep_weights(w_in, w_q_b, w_kv_b, w_pool, w_o, w_up, w_down, w_pg, w_pe):
    w_kr = jnp.pad(w_in[:, 512:512 + QK_ROPE], ((0, 0), (ROPE_LO, LANES - ROPE_HI)))
    win = jnp.concatenate([w_in[:, :512], w_kr, w_in[:, 512 + QK_ROPE:]], axis=1).astype(BF16)
    wq_pad = _pad_heads(w_q_b, QK_NOPE + QK_ROPE, QK_NOPE + QK_ROPE).astype(BF16)
    w_kv = w_kv_b.reshape(KV_LORA, N_HEADS, QK_NOPE + V_DIM)
    w_uk, w_uv = w_kv[..., :QK_NOPE], w_kv[..., QK_NOPE:]
    wuk_pad = jnp.pad(w_uk, ((0, 0), (0, 0), (0, HEAD_PAD - QK_NOPE))).reshape(KV_LORA, -1).astype(BF16)
    wuvt = w_uv.reshape(KV_LORA, N_HEADS * V_DIM).T.astype(BF16)
    wukt = jnp.transpose(w_uk, (1, 2, 0)).astype(BF16)
    eye = jnp.eye(N_HEADS, dtype=w_uv.dtype)
    wuv_bd = jnp.einsum('lhv,hg->hlgv', w_uv, eye).reshape(N_HEADS * KV_LORA, N_HEADS * V_DIM).astype(BF16)
    return dict(win=win, wq_pad=wq_pad, wqt=wq_pad.T, wuk_pad=wuk_pad, wuvt=wuvt, wukt=wukt, wuv_bd=wuv_bd,
                wpool=w_pool.astype(BF16), wo=w_o.astype(BF16), wup=w_up.astype(BF16),
                wdown=w_down.astype(BF16), wpg=w_pg.astype(BF16), wpe=w_pe.astype(BF16))


def _rope_inv():
    inv = 1.0 / (ROPE_THETA ** (jnp.arange(0, QK_ROPE, 2, dtype=F32) / QK_ROPE))
    inv_lane = jnp.zeros((1, LANES), F32).at[0, ROPE_LO:ROPE_HI].set(jnp.concatenate([inv, inv]))
    return inv_lane, inv[:, None]


def _project_prompt(x, wb, g_q, g_kv, tm, tk):
    s = x.shape[0]
    _, inv_sub = _rope_inv()
    n = s // tm
    outs = pl.pallas_call(
        functools.partial(_proj_prompt_kernel, tm=tm, tk=tk, pos0=0),
        grid=(n,),
        in_specs=[pl.BlockSpec((tm, D_MODEL), lambda i: (i, 0)),
                  _const_spec(wb['win'].shape), _const_spec((1, Q_LORA)), _const_spec((1, KV_LORA)),
                  _const_spec(wb['wqt'].shape), _const_spec(wb['wuk_pad'].shape), _const_spec(wb['wuvt'].shape),
                  _const_spec((QK_ROPE // 2, 1))],
        out_specs=[pl.BlockSpec((N_HEADS * HEAD_PAD, tm), lambda i: (0, i)),
                   pl.BlockSpec((tm, N_HEADS * HEAD_PAD), lambda i: (i, 0)),
                   pl.BlockSpec((tm // tk, N_HEADS * V_ROWS, tk), lambda i: (i, 0, 0)),
                   pl.BlockSpec((tm, KV_LORA), lambda i: (i, 0)),
                   pl.BlockSpec((QK_ROPE, tm), lambda i: (0, i)),
                   pl.BlockSpec((tm, POOL_WIDTH), lambda i: (i, 0))],
        out_shape=[jax.ShapeDtypeStruct((N_HEADS * HEAD_PAD, s), BF16),
                   jax.ShapeDtypeStruct((s, N_HEADS * HEAD_PAD), BF16),
                   jax.ShapeDtypeStruct((s // tk, N_HEADS * V_ROWS, tk), BF16),
                   jax.ShapeDtypeStruct((s, KV_LORA), F32),
                   jax.ShapeDtypeStruct((QK_ROPE, s), F32),
                   jax.ShapeDtypeStruct((s, POOL_WIDTH), F32)],
        compiler_params=_params(("arbitrary",)),
        name="proj_prompt",
    )(x, wb['win'], g_q.reshape(1, -1), g_kv.reshape(1, -1), wb['wqt'], wb['wuk_pad'], wb['wuvt'], inv_sub)
    return outs


def _project_sample(x, wb, g_q, g_kv, seq_len, pos0):
    rows = x.shape[0]
    assert seq_len & (seq_len - 1) == 0
    inv_lane, _ = _rope_inv()
    return pl.pallas_call(
        functools.partial(_proj_sample_kernel, tm=rows, seq_len=seq_len, pos0=pos0),
        grid=(1,),
        in_specs=[_const_spec((rows, D_MODEL)), _const_spec(wb['win'].shape), _const_spec((1, Q_LORA)),
                  _const_spec((1, KV_LORA)), _const_spec(wb['wq_pad'].shape), _const_spec((1, LANES))],
        out_specs=[_const_spec((rows, N_HEADS * HEAD_PAD)), _const_spec((rows, KV_LORA)),
                   _const_spec((rows, QK_ROPE)), _const_spec((rows, POOL_WIDTH))],
        out_shape=[jax.ShapeDtypeStruct((rows, N_HEADS * HEAD_PAD), F32),
                   jax.ShapeDtypeStruct((rows, KV_LORA), F32),
                   jax.ShapeDtypeStruct((rows, QK_ROPE), F32),
                   jax.ShapeDtypeStruct((rows, POOL_WIDTH), F32)],
        compiler_params=_params(("arbitrary",)),
        name="proj_sample",
    )(x, wb['win'], g_q.reshape(1, -1), g_kv.reshape(1, -1), wb['wq_pad'], inv_lane)


def _attend_prompt(qt, k, vt, tq, tk, hg):
    s = k.shape[0]
    nkv = s // tk
    k3 = k.reshape(nkv, tk, N_HEADS * HEAD_PAD)
    resident = pl.Buffered(1)
    assert tq == tk
    return pl.pallas_call(
        functools.partial(_attn_prompt_kernel, tq=tq, hg=hg),
        grid=(N_HEADS // hg, s // tq),
        in_specs=[pl.BlockSpec((hg * HEAD_PAD, tq), lambda g, i: (g, i)),
                  pl.BlockSpec((nkv, tk, hg * HEAD_PAD), lambda g, i: (0, 0, g), pipeline_mode=resident),
                  pl.BlockSpec((nkv, hg * V_ROWS, tk), lambda g, i: (0, g, 0), pipeline_mode=resident)],
        out_specs=pl.BlockSpec((hg * V_DIM, tq), lambda g, i: (g, i)),
        out_shape=jax.ShapeDtypeStruct((MLA_WIDTH, s), F32),
        scratch_shapes=[pltpu.VMEM((hg, tk, tq), F32), pltpu.VMEM((hg, tk, tq), F32),
                        pltpu.VMEM((hg, 1, tq), F32), pltpu.VMEM((hg, V_ROWS, tq), F32)],
        compiler_params=_params(("arbitrary", "arbitrary")),
        name="attn_prompt",
    )(qt, k3, vt)


def _attend_sample(q, c_new, kr_new, c_hist, kr_hist, wb, t):
    nb, n_past, _ = c_hist.shape
    assert t & (t - 1) == 0
    return pl.pallas_call(
        functools.partial(_attn_sample_kernel, t=t, n_past=n_past),
        grid=(nb,),
        in_specs=[pl.BlockSpec((t, N_HEADS * HEAD_PAD), lambda b: (b, 0)),
                  pl.BlockSpec((t, KV_LORA), lambda b: (b, 0)),
                  pl.BlockSpec((t, QK_ROPE), lambda b: (b, 0)),
                  pl.BlockSpec((None, n_past, KV_LORA), lambda b: (b, 0, 0)),
                  pl.BlockSpec((None, QK_ROPE, n_past), lambda b: (b, 0, 0)),
                  _const_spec(wb['wukt'].shape), _const_spec(wb['wuv_bd'].shape)],
        out_specs=pl.BlockSpec((t, MLA_WIDTH), lambda b: (b, 0)),
        out_shape=jax.ShapeDtypeStruct((nb * t, MLA_WIDTH), F32),
        compiler_params=_params(("arbitrary",)),
        name="attn_sample",
    )(q, c_new, kr_new, c_hist, jnp.swapaxes(kr_hist, 1, 2), wb['wukt'], wb['wuv_bd'])


def _post(x, o, u, uprev, uprev_map, pe, chist, wb, small, *, nseq, seq_len, carry, o_transposed, pos0):
    rows = x.shape[0]
    tm = nseq * seq_len
    n = rows // tm
    assert nseq == 1 if carry else seq_len & (seq_len - 1) == 0
    s_pool, ln1_g, ln1_b, w_dw, b_dw, ln2_g, ln2_b = small
    o_spec = (pl.BlockSpec((MLA_WIDTH, tm), lambda i: (0, i)) if o_transposed
              else pl.BlockSpec((tm, MLA_WIDTH), lambda i: (i, 0)))
    per_tile = (lambda i: (0, 0, 0)) if carry else (lambda i: (i, 0, 0))
    row = lambda v: v.reshape(1, -1)
    return pl.pallas_call(
        functools.partial(_post_kernel, nseq=nseq, seq_len=seq_len, carry=carry, o_transposed=o_transposed,
                          pos0=pos0),
        grid=(n,),
        in_specs=[pl.BlockSpec((tm, D_MODEL), lambda i: (i, 0)),
                  o_spec,
                  pl.BlockSpec((tm, POOL_WIDTH), lambda i: (i, 0)),
                  pl.BlockSpec((nseq, HALO, POOL_WIDTH), uprev_map),
                  pl.BlockSpec((tm, PLE_DIM), lambda i: (i, 0)),
                  pl.BlockSpec((nseq, CONV_W - 1, 2 * D_FF), per_tile),
                  _const_spec(wb['wpool'].shape), _const_spec((1, POOL_WIDTH)), _const_spec(wb['wo'].shape),
                  _const_spec((1, D_MODEL)), _const_spec((1, D_MODEL)),
                  _const_spec(wb['wup'].shape), _const_spec((CONV_W, 2 * D_FF)), _const_spec((1, 2 * D_FF)),
                  _const_spec(wb['wdown'].shape), _const_spec(wb['wpg'].shape), _const_spec(wb['wpe'].shape),
                  _const_spec((1, D_MODEL)), _const_spec((1, D_MODEL))],
        out_specs=[pl.BlockSpec((tm, D_MODEL), lambda i: (i, 0)),
                   pl.BlockSpec((nseq, CONV_HALO, 2 * D_FF), per_tile)],
        out_shape=[jax.ShapeDtypeStruct((rows, D_MODEL), F32),
                   jax.ShapeDtypeStruct((nseq if carry else n * nseq, CONV_HALO, 2 * D_FF), F32)],
        scratch_shapes=[pltpu.VMEM((nseq * (seq_len + HALO), POOL_WIDTH), F32),
                        pltpu.VMEM((nseq * (seq_len + CONV_HALO), FF_CHUNK), F32),
                        pltpu.VMEM((nseq * (seq_len + CONV_HALO), FF_CHUNK), F32),
                        pltpu.VMEM((tm, D_FF), BF16),
                        pltpu.VMEM((nseq, CONV_HALO, 2 * D_FF), F32)],
        compiler_params=_params(("arbitrary",)),
        name="post_prompt" if carry else "post_sample",
    )(x, o, u, uprev, pe, chist,
      wb['wpool'], row(s_pool), wb['wo'], row(ln1_g), row(ln1_b), wb['wup'], w_dw, row(b_dw), wb['wdown'],
      wb['wpg'], wb['wpe'], row(ln2_g), row(ln2_b))


def _layer_prompt(x, pe, wb, g_q, g_kv, small, *, tm_proj, tq, tk, hg, tm_post):
    s = x.shape[0]
    qt, k, vt, c, kr, u = _project_prompt(x, wb, g_q, g_kv, tm_proj, tk)
    ot = _attend_prompt(qt, k, vt, tq, tk, hg)
    per = tm_post // HALO
    y, clast = _post(x, ot, u, u.reshape(s // HALO, HALO, POOL_WIDTH),
                     lambda i: (jnp.maximum(i * per - 1, 0), 0, 0),
                     pe, jnp.zeros((1, CONV_W - 1, 2 * D_FF), F32),
                     wb, small, nseq=1, seq_len=tm_post, carry=True, o_transposed=True, pos0=0)
    return y, c, kr.T, u[s - POOL_HIST:], clast[0, CONV_HALO - (CONV_W - 1):]


def _layer_sample(x, pe, c_hist, kr_hist, pool_hist, conv_hist, wb, g_q, g_kv, small):
    nb, t, _ = x.shape
    n_past = c_hist.shape[1]
    xf = x.reshape(nb * t, D_MODEL)
    q, c, kr, u = _project_sample(xf, wb, g_q, g_kv, t, n_past)
    o = _attend_sample(q, c, kr, c_hist, kr_hist, wb, t)
    uprev = jnp.pad(pool_hist, ((0, 0), (HALO - POOL_HIST, 0), (0, 0)))
    y, clast = _post(xf, o, u, uprev, lambda i: (i, 0, 0), pe.reshape(nb * t, PLE_DIM), conv_hist,
                     wb, small, nseq=nb, seq_len=t, carry=False, o_transposed=False, pos0=n_past)
    u3 = u.reshape(nb, t, POOL_WIDTH)
    new_pool = jnp.concatenate([pool_hist, u3], axis=1)[:, -POOL_HIST:]
    return (y.reshape(nb, t, D_MODEL), c.reshape(nb, t, KV_LORA), kr.reshape(nb, t, QK_ROPE), new_pool,
            clast[:, CONV_HALO - (CONV_W - 1):])


def kernel(x_prompt, x_sample, cache_ckv, cache_krope, state_pool, state_ffn_conv, p_prompt, p_sample,
           w_in, g_q, w_q_b, g_kv, w_kv_b, w_pool, s_pool, w_o, ln1_g, ln1_b,
           w_up, w_dw, b_dw, w_down, w_pg, w_pe, ln2_g, ln2_b):
    assert x_prompt.shape[0] == 1 and w_in.shape[0] == DEPTH
    wb = _prep_weights(w_in[0], w_q_b[0], w_kv_b[0], w_pool[0], w_o[0], w_up[0], w_down[0], w_pg[0], w_pe[0])
    small = (s_pool[0], ln1_g[0], ln1_b[0], w_dw[0], b_dw[0], ln2_g[0], ln2_b[0])
    s = x_prompt.shape[1]
    tm_proj = min(512, s)
    tq = tk = min(512, s)
    yp, cp, krp, poolp, convp = _layer_prompt(x_prompt[0], p_prompt[0, 0], wb, g_q[0], g_kv[0], small,
                                              tm_proj=tm_proj, tq=tq, tk=tk, hg=ATTN_HEAD_GROUP, tm_post=min(256, s))
    ys, cs, krs, pools, convs = _layer_sample(x_sample, p_sample[0], cache_ckv[0], cache_krope[0],
                                              state_pool[0], state_ffn_conv[0], wb, g_q[0], g_kv[0], small)
    return (yp[None], ys, cp[None, None], krp[None, None], poolp[None, None], convp[None, None],
            cs[None], krs[None], pools[None], convs[None])
```

```python
import functools
import math

import jax
import jax.numpy as jnp
from jax import lax
from jax.experimental import pallas as pl
from jax.experimental.pallas import tpu as pltpu

F32 = jnp.float32
BF16 = jnp.bfloat16

D_MODEL = 1024
CHUNK = 64
CHUNK_SHIFT = 6
ATTN_HEAD_GROUP = 4
N_HEADS = 8
QK_NOPE = 64
QK_ROPE = 32
V_DIM = 64
V_ROWS = V_DIM + 16
Q_LORA = 256
KV_LORA = 256
ROPE_THETA = 10000.0
MLA_WIDTH = N_HEADS * V_DIM
POOL_WINDOWS = (2, 4, 8, 16)
POOL_GROUP_W = 128
POOL_WIDTH = D_MODEL - MLA_WIDTH
POOL_HIST = max(POOL_WINDOWS) - 1
D_FF = 2816
CONV_W = 3
PLE_DIM = 256
DEPTH = 1
ALPHA = (2 * DEPTH) ** 0.25
LN_EPS = 1e-5
RMS_EPS = 1e-6
NEG = -1e30
ATTN_SCALE = 1.0 / math.sqrt(QK_NOPE + QK_ROPE)

LANES = 128
HEAD_PAD = 128
ROPE_LO = QK_NOPE
ROPE_MID = QK_NOPE + QK_ROPE // 2
ROPE_HI = QK_NOPE + QK_ROPE
HALO = 16
CONV_HALO = 8
FF_CHUNK = 256
Q_SCALE = ATTN_SCALE * math.log2(math.e)
VMEM_LIMIT = 56 * 1024 * 1024

NT_DIMS = (((1,), (1,)), ((), ()))


def _rms(x, g):
    return x * lax.rsqrt(jnp.mean(x * x, axis=-1, keepdims=True) + RMS_EPS) * g


def _layer_norm(x, g, b):
    mu = jnp.mean(x, axis=-1, keepdims=True)
    xc = x - mu
    var = jnp.mean(xc * xc, axis=-1, keepdims=True)
    return xc * lax.rsqrt(var + LN_EPS) * g + b


def _token_rope_tables(pos_col, inv_lane):
    ang = pos_col.astype(F32) * inv_lane
    lane = lax.broadcasted_iota(jnp.int32, ang.shape, 1)
    first = lane < ROPE_MID
    sin = jnp.sin(ang)
    return jnp.cos(ang), jnp.where(first, -sin, sin), first


def _token_rope(x, cos, sin_signed, first):
    partner = jnp.where(first, pltpu.roll(x, LANES - QK_ROPE // 2, 1), pltpu.roll(x, QK_ROPE // 2, 1))
    return x * cos + partner * sin_signed


def _proj_prompt_kernel(x_ref, win_ref, gq_ref, gkv_ref, wqt_ref, wuk_ref, wuvt_ref, invs_ref,
                        qt_ref, k_ref, vt_ref, c_ref, kr_ref, u_ref, *, tm, tk, pos0):
    i = pl.program_id(0)
    z = jnp.dot(x_ref[...].astype(BF16), win_ref[...], preferred_element_type=F32)
    q_a = z[:, 0:Q_LORA]
    c_raw = z[:, Q_LORA:Q_LORA + KV_LORA]
    kr_pad = z[:, 512:640]
    u_ref[...] = z[:, 640:640 + POOL_WIDTH]

    qn = _rms(q_a, gq_ref[...]).astype(BF16)
    c = _rms(c_raw, gkv_ref[...])
    c_ref[...] = c
    cb = c.astype(BF16)

    pos_row = pos0 + i * tm + lax.broadcasted_iota(jnp.int32, (QK_ROPE // 2, tm), 1)
    ang = pos_row.astype(F32) * invs_ref[...]
    cos_t, sin_t = jnp.cos(ang), jnp.sin(ang)
    lo, hi = (ROPE_LO, tm), (HEAD_PAD - ROPE_HI, tm)
    cos = jnp.concatenate([jnp.ones(lo, F32), cos_t, cos_t, jnp.ones(hi, F32)], axis=0).T
    sin_signed = jnp.concatenate([jnp.zeros(lo, F32), -sin_t, sin_t, jnp.zeros(hi, F32)], axis=0).T
    first = lax.broadcasted_iota(jnp.int32, (tm, LANES), 1) < ROPE_MID
    kr_rot = _token_rope(kr_pad, cos, sin_signed, first)
    kr_ref[...] = kr_rot.T[ROPE_LO:ROPE_HI, :]

    k_nope = jnp.dot(cb, wuk_ref[...], preferred_element_type=F32)
    for h in range(N_HEADS):
        sl = slice(h * HEAD_PAD, (h + 1) * HEAD_PAD)
        k_ref[:, sl] = (k_nope[:, sl] + kr_rot).astype(BF16)

    vt = lax.dot_general(wuvt_ref[...], cb, NT_DIMS, preferred_element_type=F32)
    ones = jnp.ones((V_ROWS - V_DIM, tk), BF16)
    for s in range(tm // tk):
        for h in range(N_HEADS):
            vt_ref[s, h * V_ROWS:h * V_ROWS + V_DIM, :] = vt[h * V_DIM:(h + 1) * V_DIM, s * tk:(s + 1) * tk].astype(BF16)
            vt_ref[s, h * V_ROWS + V_DIM:(h + 1) * V_ROWS, :] = ones

    qt = lax.dot_general(wqt_ref[...], qn, NT_DIMS, preferred_element_type=F32)
    for h in range(N_HEADS):
        b0 = h * HEAD_PAD
        qt_ref[b0:b0 + ROPE_LO, :] = (qt[b0:b0 + ROPE_LO] * Q_SCALE).astype(BF16)
        x1 = qt[b0 + ROPE_LO:b0 + ROPE_MID]
        x2 = qt[b0 + ROPE_MID:b0 + ROPE_HI]
        qt_ref[b0 + ROPE_LO:b0 + ROPE_MID, :] = ((x1 * cos_t - x2 * sin_t) * Q_SCALE).astype(BF16)
        qt_ref[b0 + ROPE_MID:b0 + ROPE_HI, :] = ((x2 * cos_t + x1 * sin_t) * Q_SCALE).astype(BF16)
        qt_ref[b0 + ROPE_HI:b0 + HEAD_PAD, :] = jnp.zeros((HEAD_PAD - ROPE_HI, tm), BF16)


def _proj_sample_kernel(x_ref, win_ref, gq_ref, gkv_ref, wq_ref, invl_ref,
                        q_ref, c_ref, kr_ref, u_ref, *, tm, seq_len, pos0):
    z = jnp.dot(x_ref[...].astype(BF16), win_ref[...], preferred_element_type=F32)
    q_a = z[:, 0:Q_LORA]
    c_raw = z[:, Q_LORA:Q_LORA + KV_LORA]
    kr_pad = z[:, 512:640]
    u_ref[...] = z[:, 640:640 + POOL_WIDTH]

    qn = _rms(q_a, gq_ref[...]).astype(BF16)
    c_ref[...] = _rms(c_raw, gkv_ref[...])

    row = lax.broadcasted_iota(jnp.int32, (tm, LANES), 0)
    pos_col = pos0 + (row & (seq_len - 1))
    cos, sin_signed, first = _token_rope_tables(pos_col, invl_ref[...])
    kr_ref[...] = _token_rope(kr_pad, cos, sin_signed, first)[:, ROPE_LO:ROPE_HI]

    q = jnp.dot(qn, wq_ref[...], preferred_element_type=F32)
    for h in range(N_HEADS):
        sl = slice(h * HEAD_PAD, (h + 1) * HEAD_PAD)
        q_ref[:, sl] = _token_rope(q[:, sl], cos, sin_signed, first) * Q_SCALE


def _attn_prompt_kernel(qt_ref, k_ref, vt_ref, o_ref, sa_ref, sb_ref, m_ref, acc_ref, *, tq, hg):
    i = pl.program_id(1)
    m_ref[...] = jnp.full(m_ref.shape, NEG, F32)
    acc_ref[...] = jnp.zeros(acc_ref.shape, F32)

    def scores(j, h):
        return jnp.dot(k_ref[j, :, h * HEAD_PAD:(h + 1) * HEAD_PAD], qt_ref[h * HEAD_PAD:(h + 1) * HEAD_PAD, :],
                       preferred_element_type=F32)

    def step(cur_ref, jv, nxt_ref, j_next):
        for h in range(hg):
            s = cur_ref[h]
            m_old = m_ref[h]
            m_new = jnp.maximum(m_old, jnp.max(s, axis=0, keepdims=True))
            alpha = jnp.exp2(m_old - m_new)
            p = jnp.exp2(s - m_new)
            m_ref[h] = m_new
            if nxt_ref is not None:
                nxt_ref[h] = scores(j_next, h)
            acc_ref[h] = alpha * acc_ref[h] + jnp.dot(vt_ref[jv, h * V_ROWS:(h + 1) * V_ROWS, :], p.astype(BF16),
                                                      preferred_element_type=F32)

    def finish():
        for h in range(hg):
            o_ref[h * V_DIM:(h + 1) * V_DIM, :] = acc_ref[h, 0:V_DIM, :] / acc_ref[h, V_DIM:V_DIM + 1, :]

    kchunk = lax.broadcasted_iota(jnp.int32, (tq, tq), 0) >> CHUNK_SHIFT
    qchunk = lax.broadcasted_iota(jnp.int32, (tq, tq), 1) >> CHUNK_SHIFT
    visible = kchunk <= qchunk
    for h in range(hg):
        sa_ref[h] = jnp.where(visible, scores(i, h), NEG)

    def pair(t, c):
        step(sa_ref, jnp.where(t == 0, i, 2 * t - 1), sb_ref, 2 * t)
        step(sb_ref, 2 * t, sa_ref, 2 * t + 1)
        return c

    lax.fori_loop(0, i // 2, pair, 0)

    @pl.when(i % 2 == 0)
    def _():
        step(sa_ref, jnp.maximum(i - 1, 0), None, None)
        finish()

    @pl.when(i % 2 == 1)
    def _():
        step(sa_ref, jnp.where(i == 1, i, i - 2), sb_ref, i - 1)
        step(sb_ref, i - 1, None, None)
        finish()


def _attn_sample_kernel(q_ref, cn_ref, krn_ref, ch_ref, krht_ref, wukt_ref, wuvbd_ref, o_ref, *, t, n_past):
    q = q_ref[...]
    qlat, qrope = [], []
    for h in range(N_HEADS):
        b0 = h * HEAD_PAD
        qn = q[:, b0:b0 + QK_NOPE].astype(BF16)
        qlat.append(jnp.dot(qn, wukt_ref[h], preferred_element_type=F32))
        qrope.append(q[:, b0 + ROPE_LO:b0 + ROPE_HI])
    ql = jnp.concatenate(qlat, axis=0).astype(BF16)
    qr = jnp.concatenate(qrope, axis=0).astype(BF16)

    chb = ch_ref[...].astype(BF16)
    krhb = krht_ref[...].astype(BF16)
    cnb = cn_ref[...].astype(BF16)
    krnb = krn_ref[...].astype(BF16)
    s_h = (lax.dot_general(ql, chb, NT_DIMS, preferred_element_type=F32)
           + jnp.dot(qr, krhb, preferred_element_type=F32))
    s_n = (lax.dot_general(ql, cnb, NT_DIMS, preferred_element_type=F32)
           + lax.dot_general(qr, krnb, NT_DIMS, preferred_element_type=F32))

    rows = N_HEADS * t
    qchunk = (n_past + (lax.broadcasted_iota(jnp.int32, (rows, 1), 0) & (t - 1))) >> CHUNK_SHIFT
    s_h = jnp.where((lax.broadcasted_iota(jnp.int32, (rows, n_past), 1) >> CHUNK_SHIFT) <= qchunk, s_h, NEG)
    s_n = jnp.where(((n_past + lax.broadcasted_iota(jnp.int32, (rows, t), 1)) >> CHUNK_SHIFT) <= qchunk, s_n, NEG)

    m = jnp.maximum(jnp.max(s_h, axis=-1, keepdims=True), jnp.max(s_n, axis=-1, keepdims=True))
    p_h = jnp.exp2(s_h - m)
    p_n = jnp.exp2(s_n - m)
    l = jnp.sum(p_h, axis=-1, keepdims=True) + jnp.sum(p_n, axis=-1, keepdims=True)
    olat = (jnp.dot(p_h.astype(BF16), chb, preferred_element_type=F32)
            + jnp.dot(p_n.astype(BF16), cnb, preferred_element_type=F32)) / l
    wide = jnp.concatenate([olat[h * t:(h + 1) * t] for h in range(N_HEADS)], axis=1).astype(BF16)
    o_ref[...] = jnp.dot(wide, wuvbd_ref[...], preferred_element_type=F32)


def _post_kernel(x_ref, o_ref, u_ref, uprev_ref, pe_ref, chist_ref,
                 wpool_ref, spool_ref, wo_ref, ln1g_ref, ln1b_ref, wup_ref, wdw_ref, bdw_ref, wdown_ref,
                 wpg_ref, wpe_ref, ln2g_ref, ln2b_ref,
                 y_ref, clast_ref,
                 ubuf, bufa, bufb, hbuf, cbuf, *, nseq, seq_len, carry, o_transposed, pos0):
    i = pl.program_id(0)
    tm = nseq * seq_len
    hist_rows = slice(CONV_HALO - (CONV_W - 1), CONV_HALO)

    def init_conv_history():
        cbuf[...] = jnp.zeros_like(cbuf)
        cbuf[:, hist_rows, :] = chist_ref[...]

    if carry:
        pl.when(i == 0)(init_conv_history)
    else:
        init_conv_history()

    def gather(buf, halo, shift, cols):
        stride = halo + seq_len
        pieces = [buf[q * stride + halo - shift:q * stride + halo - shift + seq_len, cols] for q in range(nseq)]
        return pieces[0] if nseq == 1 else jnp.concatenate(pieces, axis=0)

    u = u_ref[...]
    for q in range(nseq):
        uprev = uprev_ref[q]
        if carry:
            uprev = jnp.where(i > 0, uprev, 0.0)
        ubuf[q * (HALO + seq_len):q * (HALO + seq_len) + HALO, :] = uprev
        ubuf[q * (HALO + seq_len) + HALO:(q + 1) * (HALO + seq_len), :] = u[q * seq_len:(q + 1) * seq_len]
    row = lax.broadcasted_iota(jnp.int32, (tm, POOL_GROUP_W), 0)
    frame = row + i * tm if carry else row & (seq_len - 1)
    pooled = []
    for g, w in enumerate(POOL_WINDOWS):
        cols = slice(g * POOL_GROUP_W, (g + 1) * POOL_GROUP_W)
        win = u[:, cols]
        for k in range(1, w):
            win = win + gather(ubuf, HALO, k, cols)
        cnt = jnp.minimum(w, pos0 + frame + 1).astype(F32)
        d = win / cnt - u[:, cols]
        yg = jnp.dot(d.astype(BF16), wpool_ref[g], preferred_element_type=F32) * spool_ref[:, cols]
        pooled.append(yg.astype(BF16))
    pooled = jnp.concatenate(pooled, axis=1)

    attn = o_ref[...].T if o_transposed else o_ref[...]
    mix = (jnp.dot(attn.astype(BF16), wo_ref[0:MLA_WIDTH, :], preferred_element_type=F32)
           + jnp.dot(pooled, wo_ref[MLA_WIDTH:D_MODEL, :], preferred_element_type=F32))
    x1 = _layer_norm(ALPHA * x_ref[...] + mix, ln1g_ref[...], ln1b_ref[...])
    x1b = x1.astype(BF16)

    def conv_chunk(cols, buf):
        up = jnp.dot(x1b, wup_ref[:, cols], preferred_element_type=F32)
        stride = CONV_HALO + seq_len
        for q in range(nseq):
            buf[q * stride:q * stride + CONV_HALO, :] = cbuf[q, :, cols]
            buf[q * stride + CONV_HALO:(q + 1) * stride, :] = up[q * seq_len:(q + 1) * seq_len]
            cbuf[q, :, cols] = buf[q * stride + seq_len:(q + 1) * stride, :]
        everything = slice(None)
        return (gather(buf, CONV_HALO, 2, everything) * wdw_ref[0:1, cols]
                + gather(buf, CONV_HALO, 1, everything) * wdw_ref[1:2, cols]
                + up * wdw_ref[2:3, cols] + bdw_ref[:, cols])

    for j in range(D_FF // FF_CHUNK):
        a = conv_chunk(slice(j * FF_CHUNK, (j + 1) * FF_CHUNK), bufa)
        b = conv_chunk(slice(D_FF + j * FF_CHUNK, D_FF + (j + 1) * FF_CHUNK), bufb)
        hbuf[:, j * FF_CHUNK:(j + 1) * FF_CHUNK] = (a * jax.nn.sigmoid(a) * b).astype(BF16)
    ffn = jnp.dot(hbuf[...], wdown_ref[...], preferred_element_type=F32)
    clast_ref[...] = cbuf[...]

    ple = (jax.nn.sigmoid(jnp.dot(x1b, wpg_ref[...], preferred_element_type=F32))
           * jnp.dot(pe_ref[...].astype(BF16), wpe_ref[...], preferred_element_type=F32))
    y_ref[...] = _layer_norm(ALPHA * x1 + ffn + ple, ln2g_ref[...], ln2b_ref[...])


def _const_spec(shape):
    nd = len(shape)
    return pl.BlockSpec(shape, lambda *_: (0,) * nd)


def _params(semantics):
    return pltpu.CompilerParams(dimension_semantics=semantics, vmem_limit_bytes=VMEM_LIMIT)


def _pad_heads(w, per_head, used):
    k = w.shape[0]
    w = w.reshape(k, N_HEADS, per_head)[:, :, :used]
    return jnp.pad(w, ((0, 0), (0, 0), (0, HEAD_PAD - used))).reshape(k, N_HEADS * HEAD_PAD)


def _prep_weights(w_in, w_q_b, w_kv_b, w_pool, w_o, w_up, w_down, w_pg, w_pe):
    w_kr = jnp.pad(w_in[:, 512:512 + QK_ROPE], ((0, 0), (ROPE_LO, LANES - ROPE_HI)))
    win = jnp.concatenate([w_in[:, :512], w_kr, w_in[:, 512 + QK_ROPE:]], axis=1).astype(BF16)
    wq_pad = _pad_heads(w_q_b, QK_NOPE + QK_ROPE, QK_NOPE + QK_ROPE).astype(BF16)
    w_kv = w_kv_b.reshape(KV_LORA, N_HEADS, QK_NOPE + V_DIM)
    w_uk, w_uv = w_kv[..., :QK_NOPE], w_kv[..., QK_NOPE:]
    wuk_pad = jnp.pad(w_uk, ((0, 0), (0, 0), (0, HEAD_PAD - QK_NOPE))).reshape(KV_LORA, -1).astype(BF16)
    wuvt = w_uv.reshape(KV_LORA, N_HEADS * V_DIM).T.astype(BF16)
    wukt = jnp.transpose(w_uk, (1, 2, 0)).astype(BF16)
    eye = jnp.eye(N_HEADS, dtype=w_uv.dtype)
    wuv_bd = jnp.einsum('lhv,hg->hlgv', w_uv, eye).reshape(N_HEADS * KV_LORA, N_HEADS * V_DIM).astype(BF16)
    return dict(win=win, wq_pad=wq_pad, wqt=wq_pad.T, wuk_pad=wuk_pad, wuvt=wuvt, wukt=wukt, wuv_bd=wuv_bd,
                wpool=w_pool.astype(BF16), wo=w_o.astype(BF16), wup=w_up.astype(BF16),
                wdown=w_down.astype(BF16), wpg=w_pg.astype(BF16), wpe=w_pe.astype(BF16))


def _rope_inv():
    inv = 1.0 / (ROPE_THETA ** (jnp.arange(0, QK_ROPE, 2, dtype=F32) / QK_ROPE))
    inv_lane = jnp.zeros((1, LANES), F32).at[0, ROPE_LO:ROPE_HI].set(jnp.concatenate([inv, inv]))
    return inv_lane, inv[:, None]


def _project_prompt(x, wb, g_q, g_kv, tm, tk):
    s = x.shape[0]
    _, inv_sub = _rope_inv()
    n = s // tm
    outs = pl.pallas_call(
        functools.partial(_proj_prompt_kernel, tm=tm, tk=tk, pos0=0),
        grid=(n,),
        in_specs=[pl.BlockSpec((tm, D_MODEL), lambda i: (i, 0)),
                  _const_spec(wb['win'].shape), _const_spec((1, Q_LORA)), _const_spec((1, KV_LORA)),
                  _const_spec(wb['wqt'].shape), _const_spec(wb['wuk_pad'].shape), _const_spec(wb['wuvt'].shape),
                  _const_spec((QK_ROPE // 2, 1))],
        out_specs=[pl.BlockSpec((N_HEADS * HEAD_PAD, tm), lambda i: (0, i)),
                   pl.BlockSpec((tm, N_HEADS * HEAD_PAD), lambda i: (i, 0)),
                   pl.BlockSpec((tm // tk, N_HEADS * V_ROWS, tk), lambda i: (i, 0, 0)),
                   pl.BlockSpec((tm, KV_LORA), lambda i: (i, 0)),
                   pl.BlockSpec((QK_ROPE, tm), lambda i: (0, i)),
                   pl.BlockSpec((tm, POOL_WIDTH), lambda i: (i, 0))],
        out_shape=[jax.ShapeDtypeStruct((N_HEADS * HEAD_PAD, s), BF16),
                   jax.ShapeDtypeStruct((s, N_HEADS * HEAD_PAD), BF16),
                   jax.ShapeDtypeStruct((s // tk, N_HEADS * V_ROWS, tk), BF16),
                   jax.ShapeDtypeStruct((s, KV_LORA), F32),
                   jax.ShapeDtypeStruct((QK_ROPE, s), F32),
                   jax.ShapeDtypeStruct((s, POOL_WIDTH), F32)],
        compiler_params=_params(("arbitrary",)),
        name="proj_prompt",
    )(x, wb['win'], g_q.reshape(1, -1), g_kv.reshape(1, -1), wb['wqt'], wb['wuk_pad'], wb['wuvt'], inv_sub)
    return outs


def _project_sample(x, wb, g_q, g_kv, seq_len, pos0):
    rows = x.shape[0]
    assert seq_len & (seq_len - 1) == 0
    inv_lane, _ = _rope_inv()
    return pl.pallas_call(
        functools.partial(_proj_sample_kernel, tm=rows, seq_len=seq_len, pos0=pos0),
        grid=(1,),
        in_specs=[_const_spec((rows, D_MODEL)), _const_spec(wb['win'].shape), _const_spec((1, Q_LORA)),
                  _const_spec((1, KV_LORA)), _const_spec(wb['wq_pad'].shape), _const_spec((1, LANES))],
        out_specs=[_const_spec((rows, N_HEADS * HEAD_PAD)), _const_spec((rows, KV_LORA)),
                   _const_spec((rows, QK_ROPE)), _const_spec((rows, POOL_WIDTH))],
        out_shape=[jax.ShapeDtypeStruct((rows, N_HEADS * HEAD_PAD), F32),
                   jax.ShapeDtypeStruct((rows, KV_LORA), F32),
                   jax.ShapeDtypeStruct((rows, QK_ROPE), F32),
                   jax.ShapeDtypeStruct((rows, POOL_WIDTH), F32)],
        compiler_params=_params(("arbitrary",)),
        name="proj_sample",
    )(x, wb['win'], g_q.reshape(1, -1), g_kv.reshape(1, -1), wb['wq_pad'], inv_lane)


def _attend_prompt(qt, k, vt, tq, tk, hg):
    s = k.shape[0]
    nkv = s // tk
    k3 = k.reshape(nkv, tk, N_HEADS * HEAD_PAD)
    resident = pl.Buffered(1)
    assert tq == tk
    return pl.pallas_call(
        functools.partial(_attn_prompt_kernel, tq=tq, hg=hg),
        grid=(N_HEADS // hg, s // tq),
        in_specs=[pl.BlockSpec((hg * HEAD_PAD, tq), lambda g, i: (g, i)),
                  pl.BlockSpec((nkv, tk, hg * HEAD_PAD), lambda g, i: (0, 0, g), pipeline_mode=resident),
                  pl.BlockSpec((nkv, hg * V_ROWS, tk), lambda g, i: (0, g, 0), pipeline_mode=resident)],
        out_specs=pl.BlockSpec((hg * V_DIM, tq), lambda g, i: (g, i)),
        out_shape=jax.ShapeDtypeStruct((MLA_WIDTH, s), F32),
        scratch_shapes=[pltpu.VMEM((hg, tk, tq), F32), pltpu.VMEM((hg, tk, tq), F32),
                        pltpu.VMEM((hg, 1, tq), F32), pltpu.VMEM((hg, V_ROWS, tq), F32)],
        compiler_params=_params(("arbitrary", "arbitrary")),
        name="attn_prompt",
    )(qt, k3, vt)


def _attend_sample(q, c_new, kr_new, c_hist, kr_hist, wb, t):
    nb, n_past, _ = c_hist.shape
    assert t & (t - 1) == 0
    return pl.pallas_call(
        functools.partial(_attn_sample_kernel, t=t, n_past=n_past),
        grid=(nb,),
        in_specs=[pl.BlockSpec((t, N_HEADS * HEAD_PAD), lambda b: (b, 0)),
                  pl.BlockSpec((t, KV_LORA), lambda b: (b, 0)),
                  pl.BlockSpec((t, QK_ROPE), lambda b: (b, 0)),
                  pl.BlockSpec((None, n_past, KV_LORA), lambda b: (b, 0, 0)),
                  pl.BlockSpec((None, QK_ROPE, n_past), lambda b: (b, 0, 0)),
                  _const_spec(wb['wukt'].shape), _const_spec(wb['wuv_bd'].shape)],
        out_specs=pl.BlockSpec((t, MLA_WIDTH), lambda b: (b, 0)),
        out_shape=jax.ShapeDtypeStruct((nb * t, MLA_WIDTH), F32),
        compiler_params=_params(("arbitrary",)),
        name="attn_sample",
    )(q, c_new, kr_new, c_hist, jnp.swapaxes(kr_hist, 1, 2), wb['wukt'], wb['wuv_bd'])


def _post(x, o, u, uprev, uprev_map, pe, chist, wb, small, *, nseq, seq_len, carry, o_transposed, pos0):
    rows = x.shape[0]
    tm = nseq * seq_len
    n = rows // tm
    assert nseq == 1 if carry else seq_len & (seq_len - 1) == 0
    s_pool, ln1_g, ln1_b, w_dw, b_dw, ln2_g, ln2_b = small
    o_spec = (pl.BlockSpec((MLA_WIDTH, tm), lambda i: (0, i)) if o_transposed
              else pl.BlockSpec((tm, MLA_WIDTH), lambda i: (i, 0)))
    per_tile = (lambda i: (0, 0, 0)) if carry else (lambda i: (i, 0, 0))
    row = lambda v: v.reshape(1, -1)
    return pl.pallas_call(
        functools.partial(_post_kernel, nseq=nseq, seq_len=seq_len, carry=carry, o_transposed=o_transposed,
                          pos0=pos0),
        grid=(n,),
        in_specs=[pl.BlockSpec((tm, D_MODEL), lambda i: (i, 0)),
                  o_spec,
                  pl.BlockSpec((tm, POOL_WIDTH), lambda i: (i, 0)),
                  pl.BlockSpec((nseq, HALO, POOL_WIDTH), uprev_map),
                  pl.BlockSpec((tm, PLE_DIM), lambda i: (i, 0)),
                  pl.BlockSpec((nseq, CONV_W - 1, 2 * D_FF), per_tile),
                  _const_spec(wb['wpool'].shape), _const_spec((1, POOL_WIDTH)), _const_spec(wb['wo'].shape),
                  _const_spec((1, D_MODEL)), _const_spec((1, D_MODEL)),
                  _const_spec(wb['wup'].shape), _const_spec((CONV_W, 2 * D_FF)), _const_spec((1, 2 * D_FF)),
                  _const_spec(wb['wdown'].shape), _const_spec(wb['wpg'].shape), _const_spec(wb['wpe'].shape),
                  _const_spec((1, D_MODEL)), _const_spec((1, D_MODEL))],
        out_specs=[pl.BlockSpec((tm, D_MODEL), lambda i: (i, 0)),
                   pl.BlockSpec((nseq, CONV_HALO, 2 * D_FF), per_tile)],
        out_shape=[jax.ShapeDtypeStruct((rows, D_MODEL), F32),
                   jax.ShapeDtypeStruct((nseq if carry else n * nseq, CONV_HALO, 2 * D_FF), F32)],
        scratch_shapes=[pltpu.VMEM((nseq * (seq_len + HALO), POOL_WIDTH), F32),
                        pltpu.VMEM((nseq * (seq_len + CONV_HALO), FF_CHUNK), F32),
                        pltpu.VMEM((nseq * (seq_len + CONV_HALO), FF_CHUNK), F32),
                        pltpu.VMEM((tm, D_FF), BF16),
                        pltpu.VMEM((nseq, CONV_HALO, 2 * D_FF), F32)],
        compiler_params=_params(("arbitrary",)),
        name="post_prompt" if carry else "post_sample",
    )(x, o, u, uprev, pe, chist,
      wb['wpool'], row(s_pool), wb['wo'], row(ln1_g), row(ln1_b), wb['wup'], w_dw, row(b_dw), wb['wdown'],
      wb['wpg'], wb['wpe'], row(ln2_g), row(ln2_b))


def _layer_prompt(x, pe, wb, g_q, g_kv, small, *, tm_proj, tq, tk, hg, tm_post):
    s = x.shape[0]
    qt, k, vt, c, kr, u = _project_prompt(x, wb, g_q, g_kv, tm_proj, tk)
    ot = _attend_prompt(qt, k, vt, tq, tk, hg)
    per = tm_post // HALO
    y, clast = _post(x, ot, u, u.reshape(s // HALO, HALO, POOL_WIDTH),
                     lambda i: (jnp.maximum(i * per - 1, 0), 0, 0),
                     pe, jnp.zeros((1, CONV_W - 1, 2 * D_FF), F32),
                     wb, small, nseq=1, seq_len=tm_post, carry=True, o_transposed=True, pos0=0)
    return y, c, kr.T, u[s - POOL_HIST:], clast[0, CONV_HALO - (CONV_W - 1):]


def _layer_sample(x, pe, c_hist, kr_hist, pool_hist, conv_hist, wb, g_q, g_kv, small):
    nb, t, _ = x.shape
    n_past = c_hist.shape[1]
    xf = x.reshape(nb * t, D_MODEL)
    q, c, kr, u = _project_sample(xf, wb, g_q, g_kv, t, n_past)
    o = _attend_sample(q, c, kr, c_hist, kr_hist, wb, t)
    uprev = jnp.pad(pool_hist, ((0, 0), (HALO - POOL_HIST, 0), (0, 0)))
    y, clast = _post(xf, o, u, uprev, lambda i: (i, 0, 0), pe.reshape(nb * t, PLE_DIM), conv_hist,
                     wb, small, nseq=nb, seq_len=t, carry=False, o_transposed=False, pos0=n_past)
    u3 = u.reshape(nb, t, POOL_WIDTH)
    new_pool = jnp.concatenate([pool_hist, u3], axis=1)[:, -POOL_HIST:]
    return (y.reshape(nb, t, D_MODEL), c.reshape(nb, t, KV_LORA), kr.reshape(nb, t, QK_ROPE), new_pool,
            clast[:, CONV_HALO - (CONV_W - 1):])


def kernel(x_prompt, x_sample, cache_ckv, cache_krope, state_pool, state_ffn_conv, p_prompt, p_sample,
           w_in, g_q, w_q_b, g_kv, w_kv_b, w_pool, s_pool, w_o, ln1_g, ln1_b,
           w_up, w_dw, b_dw, w_down, w_pg, w_pe, ln2_g, ln2_b):
    assert x_prompt.shape[0] == 1 and w_in.shape[0] == DEPTH
    wb = _prep_weights(w_in[0], w_q_b[0], w_kv_b[0], w_pool[0], w_o[0], w_up[0], w_down[0], w_pg[0], w_pe[0])
    small = (s_pool[0], ln1_g[0], ln1_b[0], w_dw[0], b_dw[0], ln2_g[0], ln2_b[0])
    s = x_prompt.shape[1]
    tm_proj = min(512, s)
    tq = tk = min(512, s)
    yp, cp, krp, poolp, convp = _layer_prompt(x_prompt[0], p_prompt[0, 0], wb, g_q[0], g_kv[0], small,
                                              tm_proj=tm_proj, tq=tq, tk=tk, hg=ATTN_HEAD_GROUP, tm_post=min(512, s))
    ys, cs, krs, pools, convs = _layer_sample(x_sample, p_sample[0], cache_ckv[0], cache_krope[0],
                                              state_pool[0], state_ffn_conv[0], wb, g_q[0], g_kv[0], small)
    return (yp[None], ys, cp[None, None], krp[None, None], poolp[None, None], convp[None, None],
            cs[None], krs[None], pools[None], convs[None])
```

```python
import functools
import math

import jax
import jax.numpy as jnp
from jax import lax
from jax.experimental import pallas as pl
from jax.experimental.pallas import tpu as pltpu

F32 = jnp.float32
BF16 = jnp.bfloat16

D_MODEL = 1024
CHUNK = 64
CHUNK_SHIFT = 6
ATTN_HEAD_GROUP = 4
N_HEADS = 8
QK_NOPE = 64
QK_ROPE = 32
V_DIM = 64
V_ROWS = V_DIM + 16
Q_LORA = 256
KV_LORA = 256
ROPE_THETA = 10000.0
MLA_WIDTH = N_HEADS * V_DIM
POOL_WINDOWS = (2, 4, 8, 16)
POOL_GROUP_W = 128
POOL_WIDTH = D_MODEL - MLA_WIDTH
POOL_HIST = max(POOL_WINDOWS) - 1
D_FF = 2816
CONV_W = 3
PLE_DIM = 256
DEPTH = 1
ALPHA = (2 * DEPTH) ** 0.25
LN_EPS = 1e-5
RMS_EPS = 1e-6
NEG = -1e30
ATTN_SCALE = 1.0 / math.sqrt(QK_NOPE + QK_ROPE)

LANES = 128
HEAD_PAD = 128
ROPE_LO = QK_NOPE
ROPE_MID = QK_NOPE + QK_ROPE // 2
ROPE_HI = QK_NOPE + QK_ROPE
SHIFT_LANE = ROPE_HI
HALO = 16
CONV_HALO = 8
FF_CHUNK = 256
Q_SCALE = ATTN_SCALE * math.log2(math.e)
VMEM_LIMIT = 56 * 1024 * 1024

NT_DIMS = (((1,), (1,)), ((), ()))


def _rms(x, g):
    return x * lax.rsqrt(jnp.mean(x * x, axis=-1, keepdims=True) + RMS_EPS) * g


def _layer_norm(x, g, b):
    mu = jnp.mean(x, axis=-1, keepdims=True)
    xc = x - mu
    var = jnp.mean(xc * xc, axis=-1, keepdims=True)
    return xc * lax.rsqrt(var + LN_EPS) * g + b


def _token_rope_tables(pos_col, inv_lane):
    ang = pos_col.astype(F32) * inv_lane
    lane = lax.broadcasted_iota(jnp.int32, ang.shape, 1)
    first = lane < ROPE_MID
    sin = jnp.sin(ang)
    return jnp.cos(ang), jnp.where(first, -sin, sin), first


def _token_rope(x, cos, sin_signed, first):
    partner = jnp.where(first, pltpu.roll(x, LANES - QK_ROPE // 2, 1), pltpu.roll(x, QK_ROPE // 2, 1))
    return x * cos + partner * sin_signed


def _proj_prompt_kernel(x_ref, win_ref, gq_ref, gkv_ref, wqt_ref, wuk_ref, wuvt_ref, invs_ref,
                        qt_ref, k_ref, vt_ref, c_ref, kr_ref, u_ref, *, tm, tk, pos0):
    i = pl.program_id(0)
    z = jnp.dot(x_ref[...].astype(BF16), win_ref[...], preferred_element_type=F32)
    q_a = z[:, 0:Q_LORA]
    c_raw = z[:, Q_LORA:Q_LORA + KV_LORA]
    kr_pad = z[:, 512:640]
    u_ref[...] = z[:, 640:640 + POOL_WIDTH]

    qn = _rms(q_a, gq_ref[...]).astype(BF16)
    c = _rms(c_raw, gkv_ref[...])
    c_ref[...] = c
    cb = c.astype(BF16)

    pos_row = pos0 + i * tm + lax.broadcasted_iota(jnp.int32, (QK_ROPE // 2, tm), 1)
    ang = pos_row.astype(F32) * invs_ref[...]
    cos_t, sin_t = jnp.cos(ang), jnp.sin(ang)
    lo, hi = (ROPE_LO, tm), (HEAD_PAD - ROPE_HI, tm)
    cos = jnp.concatenate([jnp.ones(lo, F32), cos_t, cos_t, jnp.ones(hi, F32)], axis=0).T
    sin_signed = jnp.concatenate([jnp.zeros(lo, F32), -sin_t, sin_t, jnp.zeros(hi, F32)], axis=0).T
    first = lax.broadcasted_iota(jnp.int32, (tm, LANES), 1) < ROPE_MID
    kr_rot = _token_rope(kr_pad, cos, sin_signed, first)
    kr_ref[...] = kr_rot.T[ROPE_LO:ROPE_HI, :]

    k_nope = jnp.dot(cb, wuk_ref[...], preferred_element_type=F32)
    k_tail = kr_rot + jnp.where(lax.broadcasted_iota(jnp.int32, (tm, LANES), 1) == SHIFT_LANE, 1.0, 0.0)
    for h in range(N_HEADS):
        sl = slice(h * HEAD_PAD, (h + 1) * HEAD_PAD)
        k_ref[:, sl] = (k_nope[:, sl] + k_tail).astype(BF16)

    vt = lax.dot_general(wuvt_ref[...], cb, NT_DIMS, preferred_element_type=F32)
    ones = jnp.ones((V_ROWS - V_DIM, tk), BF16)
    for s in range(tm // tk):
        for h in range(N_HEADS):
            vt_ref[s, h * V_ROWS:h * V_ROWS + V_DIM, :] = vt[h * V_DIM:(h + 1) * V_DIM, s * tk:(s + 1) * tk].astype(BF16)
            vt_ref[s, h * V_ROWS + V_DIM:(h + 1) * V_ROWS, :] = ones

    qt = lax.dot_general(wqt_ref[...], qn, NT_DIMS, preferred_element_type=F32)
    for h in range(N_HEADS):
        b0 = h * HEAD_PAD
        qt_ref[b0:b0 + ROPE_LO, :] = (qt[b0:b0 + ROPE_LO] * Q_SCALE).astype(BF16)
        x1 = qt[b0 + ROPE_LO:b0 + ROPE_MID]
        x2 = qt[b0 + ROPE_MID:b0 + ROPE_HI]
        qt_ref[b0 + ROPE_LO:b0 + ROPE_MID, :] = ((x1 * cos_t - x2 * sin_t) * Q_SCALE).astype(BF16)
        qt_ref[b0 + ROPE_MID:b0 + ROPE_HI, :] = ((x2 * cos_t + x1 * sin_t) * Q_SCALE).astype(BF16)
        qt_ref[b0 + ROPE_HI:b0 + HEAD_PAD, :] = jnp.zeros((HEAD_PAD - ROPE_HI, tm), BF16)


def _proj_sample_kernel(x_ref, win_ref, gq_ref, gkv_ref, wq_ref, invl_ref,
                        q_ref, c_ref, kr_ref, u_ref, *, tm, seq_len, pos0):
    z = jnp.dot(x_ref[...].astype(BF16), win_ref[...], preferred_element_type=F32)
    q_a = z[:, 0:Q_LORA]
    c_raw = z[:, Q_LORA:Q_LORA + KV_LORA]
    kr_pad = z[:, 512:640]
    u_ref[...] = z[:, 640:640 + POOL_WIDTH]

    qn = _rms(q_a, gq_ref[...]).astype(BF16)
    c_ref[...] = _rms(c_raw, gkv_ref[...])

    row = lax.broadcasted_iota(jnp.int32, (tm, LANES), 0)
    pos_col = pos0 + (row & (seq_len - 1))
    cos, sin_signed, first = _token_rope_tables(pos_col, invl_ref[...])
    kr_ref[...] = _token_rope(kr_pad, cos, sin_signed, first)[:, ROPE_LO:ROPE_HI]

    q = jnp.dot(qn, wq_ref[...], preferred_element_type=F32)
    for h in range(N_HEADS):
        sl = slice(h * HEAD_PAD, (h + 1) * HEAD_PAD)
        q_ref[:, sl] = _token_rope(q[:, sl], cos, sin_signed, first) * Q_SCALE


def _attn_prompt_kernel(qt_ref, k_ref, vt_ref, o_ref, sa_ref, sb_ref, qa_ref, m_ref, acc_ref, *, tq, hg):
    i = pl.program_id(1)

    def scores(q_ref, j, h):
        return jnp.dot(k_ref[j, :, h * HEAD_PAD:(h + 1) * HEAD_PAD], q_ref[h * HEAD_PAD:(h + 1) * HEAD_PAD, :],
                       preferred_element_type=F32)

    kchunk = lax.broadcasted_iota(jnp.int32, (tq, tq), 0) >> CHUNK_SHIFT
    qchunk = lax.broadcasted_iota(jnp.int32, (tq, tq), 1) >> CHUNK_SHIFT
    visible = kchunk <= qchunk

    def values(jv, h, p):
        return jnp.dot(vt_ref[jv, h * V_ROWS:(h + 1) * V_ROWS, :], p, preferred_element_type=F32)

    def fixed_shift_step(cur_ref, jv, nxt_ref, j_next):
        for h in range(hg):
            p = jnp.exp2(cur_ref[h]).astype(BF16)
            if nxt_ref is not None:
                nxt_ref[h] = scores(qa_ref, j_next, h)
            acc_ref[h] += values(jv, h, p)

    def running_max_step(cur_ref, jv, nxt_ref, j_next):
        for h in range(hg):
            s = cur_ref[h]
            m_old = m_ref[h]
            m_new = jnp.maximum(m_old, jnp.max(s, axis=0, keepdims=True))
            alpha = jnp.exp2(m_old - m_new)
            p = jnp.exp2(s - m_new).astype(BF16)
            m_ref[h] = m_new
            if nxt_ref is not None:
                nxt_ref[h] = scores(qt_ref, j_next, h)
            acc_ref[h] = alpha * acc_ref[h] + values(jv, h, p)

    def sweep(step):
        def pair(t, c):
            step(sa_ref, jnp.where(t == 0, i, 2 * t - 1), sb_ref, 2 * t)
            step(sb_ref, 2 * t, sa_ref, 2 * t + 1)
            return c

        lax.fori_loop(0, i // 2, pair, 0)

        @pl.when(i % 2 == 0)
        def _():
            step(sa_ref, jnp.maximum(i - 1, 0), None, None)

        @pl.when(i % 2 == 1)
        def _():
            step(sa_ref, jnp.where(i == 1, i, i - 2), sb_ref, i - 1)
            step(sb_ref, i - 1, None, None)

    shift_row = lax.broadcasted_iota(jnp.int32, (HEAD_PAD - SHIFT_LANE, tq), 0) == 0
    for h in range(hg):
        rows = slice(h * HEAD_PAD, (h + 1) * HEAD_PAD)
        lead = jnp.dot(k_ref[i, 0:CHUNK, rows], qt_ref[rows, :], preferred_element_type=F32)
        shift = jnp.max(lead, axis=0, keepdims=True).astype(BF16).astype(F32)
        qa_ref[h * HEAD_PAD:h * HEAD_PAD + SHIFT_LANE, :] = qt_ref[h * HEAD_PAD:h * HEAD_PAD + SHIFT_LANE, :]
        qa_ref[h * HEAD_PAD + SHIFT_LANE:(h + 1) * HEAD_PAD, :] = jnp.where(shift_row, -shift, 0.0).astype(BF16)
    acc_ref[...] = jnp.zeros(acc_ref.shape, F32)
    for h in range(hg):
        sa_ref[h] = jnp.where(visible, scores(qa_ref, i, h), NEG)
    sweep(fixed_shift_step)

    bad = jnp.where(jnp.isfinite(acc_ref[...]), 0.0, 1.0)
    bad = jnp.max(jnp.max(jnp.max(bad, axis=0), axis=0, keepdims=True), axis=1, keepdims=True)

    @pl.when(bad[0, 0] > 0.0)
    def _():
        m_ref[...] = jnp.full(m_ref.shape, NEG, F32)
        acc_ref[...] = jnp.zeros(acc_ref.shape, F32)
        for h in range(hg):
            sa_ref[h] = jnp.where(visible, scores(qt_ref, i, h), NEG)
        sweep(running_max_step)

    for h in range(hg):
        o_ref[h * V_DIM:(h + 1) * V_DIM, :] = acc_ref[h, 0:V_DIM, :] / acc_ref[h, V_DIM:V_DIM + 1, :]


def _attn_sample_kernel(q_ref, cn_ref, krn_ref, ch_ref, krht_ref, wukt_ref, wuvbd_ref, o_ref, *, t, n_past):
    q = q_ref[...]
    qlat, qrope = [], []
    for h in range(N_HEADS):
        b0 = h * HEAD_PAD
        qn = q[:, b0:b0 + QK_NOPE].astype(BF16)
        qlat.append(jnp.dot(qn, wukt_ref[h], preferred_element_type=F32))
        qrope.append(q[:, b0 + ROPE_LO:b0 + ROPE_HI])
    ql = jnp.concatenate(qlat, axis=0).astype(BF16)
    qr = jnp.concatenate(qrope, axis=0).astype(BF16)

    chb = ch_ref[...].astype(BF16)
    krhb = krht_ref[...].astype(BF16)
    cnb = cn_ref[...].astype(BF16)
    krnb = krn_ref[...].astype(BF16)
    s_h = (lax.dot_general(ql, chb, NT_DIMS, preferred_element_type=F32)
           + jnp.dot(qr, krhb, preferred_element_type=F32))
    s_n = (lax.dot_general(ql, cnb, NT_DIMS, preferred_element_type=F32)
           + lax.dot_general(qr, krnb, NT_DIMS, preferred_element_type=F32))

    rows = N_HEADS * t
    qchunk = (n_past + (lax.broadcasted_iota(jnp.int32, (rows, 1), 0) & (t - 1))) >> CHUNK_SHIFT
    s_h = jnp.where((lax.broadcasted_iota(jnp.int32, (rows, n_past), 1) >> CHUNK_SHIFT) <= qchunk, s_h, NEG)
    s_n = jnp.where(((n_past + lax.broadcasted_iota(jnp.int32, (rows, t), 1)) >> CHUNK_SHIFT) <= qchunk, s_n, NEG)

    m = jnp.maximum(jnp.max(s_h, axis=-1, keepdims=True), jnp.max(s_n, axis=-1, keepdims=True))
    p_h = jnp.exp2(s_h - m)
    p_n = jnp.exp2(s_n - m)
    l = jnp.sum(p_h, axis=-1, keepdims=True) + jnp.sum(p_n, axis=-1, keepdims=True)
    olat = (jnp.dot(p_h.astype(BF16), chb, preferred_element_type=F32)
            + jnp.dot(p_n.astype(BF16), cnb, preferred_element_type=F32)) / l
    wide = jnp.concatenate([olat[h * t:(h + 1) * t] for h in range(N_HEADS)], axis=1).astype(BF16)
    o_ref[...] = jnp.dot(wide, wuvbd_ref[...], preferred_element_type=F32)


def _post_kernel(x_ref, o_ref, u_ref, uprev_ref, pe_ref, chist_ref,
                 wpool_ref, spool_ref, wo_ref, ln1g_ref, ln1b_ref, wup_ref, wdw_ref, bdw_ref, wdown_ref,
                 wpg_ref, wpe_ref, ln2g_ref, ln2b_ref,
                 y_ref, clast_ref,
                 ubuf, bufa, bufb, hbuf, cbuf, *, nseq, seq_len, carry, o_transposed, pos0):
    i = pl.program_id(0)
    tm = nseq * seq_len
    hist_rows = slice(CONV_HALO - (CONV_W - 1), CONV_HALO)

    def init_conv_history():
        cbuf[...] = jnp.zeros_like(cbuf)
        cbuf[:, hist_rows, :] = chist_ref[...]

    if carry:
        pl.when(i == 0)(init_conv_history)
    else:
        init_conv_history()

    def gather(buf, halo, shift, cols):
        stride = halo + seq_len
        pieces = [buf[q * stride + halo - shift:q * stride + halo - shift + seq_len, cols] for q in range(nseq)]
        return pieces[0] if nseq == 1 else jnp.concatenate(pieces, axis=0)

    u = u_ref[...]
    for q in range(nseq):
        uprev = uprev_ref[q]
        if carry:
            uprev = jnp.where(i > 0, uprev, 0.0)
        ubuf[q * (HALO + seq_len):q * (HALO + seq_len) + HALO, :] = uprev
        ubuf[q * (HALO + seq_len) + HALO:(q + 1) * (HALO + seq_len), :] = u[q * seq_len:(q + 1) * seq_len]
    row = lax.broadcasted_iota(jnp.int32, (tm, POOL_GROUP_W), 0)
    frame = row + i * tm if carry else row & (seq_len - 1)
    pooled = []
    for g, w in enumerate(POOL_WINDOWS):
        cols = slice(g * POOL_GROUP_W, (g + 1) * POOL_GROUP_W)
        win = u[:, cols]
        for k in range(1, w):
            win = win + gather(ubuf, HALO, k, cols)
        cnt = jnp.minimum(w, pos0 + frame + 1).astype(F32)
        d = win / cnt - u[:, cols]
        yg = jnp.dot(d.astype(BF16), wpool_ref[g], preferred_element_type=F32) * spool_ref[:, cols]
        pooled.append(yg.astype(BF16))
    pooled = jnp.concatenate(pooled, axis=1)

    attn = o_ref[...].T if o_transposed else o_ref[...]
    mix = (jnp.dot(attn.astype(BF16), wo_ref[0:MLA_WIDTH, :], preferred_element_type=F32)
           + jnp.dot(pooled, wo_ref[MLA_WIDTH:D_MODEL, :], preferred_element_type=F32))
    x1 = _layer_norm(ALPHA * x_ref[...] + mix, ln1g_ref[...], ln1b_ref[...])
    x1b = x1.astype(BF16)

    def conv_chunk(cols, buf):
        up = jnp.dot(x1b, wup_ref[:, cols], preferred_element_type=F32)
        stride = CONV_HALO + seq_len
        for q in range(nseq):
            buf[q * stride:q * stride + CONV_HALO, :] = cbuf[q, :, cols]
            buf[q * stride + CONV_HALO:(q + 1) * stride, :] = up[q * seq_len:(q + 1) * seq_len]
            cbuf[q, :, cols] = buf[q * stride + seq_len:(q + 1) * stride, :]
        everything = slice(None)
        return (gather(buf, CONV_HALO, 2, everything) * wdw_ref[0:1, cols]
                + gather(buf, CONV_HALO, 1, everything) * wdw_ref[1:2, cols]
                + up * wdw_ref[2:3, cols] + bdw_ref[:, cols])

    for j in range(D_FF // FF_CHUNK):
        a = conv_chunk(slice(j * FF_CHUNK, (j + 1) * FF_CHUNK), bufa)
        b = conv_chunk(slice(D_FF + j * FF_CHUNK, D_FF + (j + 1) * FF_CHUNK), bufb)
        hbuf[:, j * FF_CHUNK:(j + 1) * FF_CHUNK] = (a * jax.nn.sigmoid(a) * b).astype(BF16)
    ffn = jnp.dot(hbuf[...], wdown_ref[...], preferred_element_type=F32)
    clast_ref[...] = cbuf[...]

    ple = (jax.nn.sigmoid(jnp.dot(x1b, wpg_ref[...], preferred_element_type=F32))
           * jnp.dot(pe_ref[...].astype(BF16), wpe_ref[...], preferred_element_type=F32))
    y_ref[...] = _layer_norm(ALPHA * x1 + ffn + ple, ln2g_ref[...], ln2b_ref[...])


def _const_spec(shape):
    nd = len(shape)
    return pl.BlockSpec(shape, lambda *_: (0,) * nd)


def _params(semantics):
    return pltpu.CompilerParams(dimension_semantics=semantics, vmem_limit_bytes=VMEM_LIMIT)


def _pad_heads(w, per_head, used):
    k = w.shape[0]
    w = w.reshape(k, N_HEADS, per_head)[:, :, :used]
    return jnp.pad(w, ((0, 0), (0, 0), (0, HEAD_PAD - used))).reshape(k, N_HEADS * HEAD_PAD)


def _prep_weights(w_in, w_q_b, w_kv_b, w_pool, w_o, w_up, w_down, w_pg, w_pe):
    w_kr = jnp.pad(w_in[:, 512:512 + QK_ROPE], ((0, 0), (ROPE_LO, LANES - ROPE_HI)))
    win = jnp.concatenate([w_in[:, :512], w_kr, w_in[:, 512 + QK_ROPE:]], axis=1).astype(BF16)
    wq_pad = _pad_heads(w_q_b, QK_NOPE + QK_ROPE, QK_NOPE + QK_ROPE).astype(BF16)
    w_kv = w_kv_b.reshape(KV_LORA, N_HEADS, QK_NOPE + V_DIM)
    w_uk, w_uv = w_kv[..., :QK_NOPE], w_kv[..., QK_NOPE:]
    wuk_pad = jnp.pad(w_uk, ((0, 0), (0, 0), (0, HEAD_PAD - QK_NOPE))).reshape(KV_LORA, -1).astype(BF16)
    wuvt = w_uv.reshape(KV_LORA, N_HEADS * V_DIM).T.astype(BF16)
    wukt = jnp.transpose(w_uk, (1, 2, 0)).astype(BF16)
    eye = jnp.eye(N_HEADS, dtype=w_uv.dtype)
    wuv_bd = jnp.einsum('lhv,hg->hlgv', w_uv, eye).reshape(N_HEADS * KV_LORA, N_HEADS * V_DIM).astype(BF16)
    return dict(win=win, wq_pad=wq_pad, wqt=wq_pad.T, wuk_pad=wuk_pad, wuvt=wuvt, wukt=wukt, wuv_bd=wuv_bd,
                wpool=w_pool.astype(BF16), wo=w_o.astype(BF16), wup=w_up.astype(BF16),
                wdown=w_down.astype(BF16), wpg=w_pg.astype(BF16), wpe=w_pe.astype(BF16))


def _rope_inv():
    inv = 1.0 / (ROPE_THETA ** (jnp.arange(0, QK_ROPE, 2, dtype=F32) / QK_ROPE))
    inv_lane = jnp.zeros((1, LANES), F32).at[0, ROPE_LO:ROPE_HI].set(jnp.concatenate([inv, inv]))
    return inv_lane, inv[:, None]


def _project_prompt(x, wb, g_q, g_kv, tm, tk):
    s = x.shape[0]
    _, inv_sub = _rope_inv()
    n = s // tm
    outs = pl.pallas_call(
        functools.partial(_proj_prompt_kernel, tm=tm, tk=tk, pos0=0),
        grid=(n,),
        in_specs=[pl.BlockSpec((tm, D_MODEL), lambda i: (i, 0)),
                  _const_spec(wb['win'].shape), _const_spec((1, Q_LORA)), _const_spec((1, KV_LORA)),
                  _const_spec(wb['wqt'].shape), _const_spec(wb['wuk_pad'].shape), _const_spec(wb['wuvt'].shape),
                  _const_spec((QK_ROPE // 2, 1))],
        out_specs=[pl.BlockSpec((N_HEADS * HEAD_PAD, tm), lambda i: (0, i)),
                   pl.BlockSpec((tm, N_HEADS * HEAD_PAD), lambda i: (i, 0)),
                   pl.BlockSpec((tm // tk, N_HEADS * V_ROWS, tk), lambda i: (i, 0, 0)),
                   pl.BlockSpec((tm, KV_LORA), lambda i: (i, 0)),
                   pl.BlockSpec((QK_ROPE, tm), lambda i: (0, i)),
                   pl.BlockSpec((tm, POOL_WIDTH), lambda i: (i, 0))],
        out_shape=[jax.ShapeDtypeStruct((N_HEADS * HEAD_PAD, s), BF16),
                   jax.ShapeDtypeStruct((s, N_HEADS * HEAD_PAD), BF16),
                   jax.ShapeDtypeStruct((s // tk, N_HEADS * V_ROWS, tk), BF16),
                   jax.ShapeDtypeStruct((s, KV_LORA), F32),
                   jax.ShapeDtypeStruct((QK_ROPE, s), F32),
                   jax.ShapeDtypeStruct((s, POOL_WIDTH), F32)],
        compiler_params=_params(("arbitrary",)),
        name="proj_prompt",
    )(x, wb['win'], g_q.reshape(1, -1), g_kv.reshape(1, -1), wb['wqt'], wb['wuk_pad'], wb['wuvt'], inv_sub)
    return outs


def _project_sample(x, wb, g_q, g_kv, seq_len, pos0):
    rows = x.shape[0]
    assert seq_len & (seq_len - 1) == 0
    inv_lane, _ = _rope_inv()
    return pl.pallas_call(
        functools.partial(_proj_sample_kernel, tm=rows, seq_len=seq_len, pos0=pos0),
        grid=(1,),
        in_specs=[_const_spec((rows, D_MODEL)), _const_spec(wb['win'].shape), _const_spec((1, Q_LORA)),
                  _const_spec((1, KV_LORA)), _const_spec(wb['wq_pad'].shape), _const_spec((1, LANES))],
        out_specs=[_const_spec((rows, N_HEADS * HEAD_PAD)), _const_spec((rows, KV_LORA)),
                   _const_spec((rows, QK_ROPE)), _const_spec((rows, POOL_WIDTH))],
        out_shape=[jax.ShapeDtypeStruct((rows, N_HEADS * HEAD_PAD), F32),
                   jax.ShapeDtypeStruct((rows, KV_LORA), F32),
                   jax.ShapeDtypeStruct((rows, QK_ROPE), F32),
                   jax.ShapeDtypeStruct((rows, POOL_WIDTH), F32)],
        compiler_params=_params(("arbitrary",)),
        name="proj_sample",
    )(x, wb['win'], g_q.reshape(1, -1), g_kv.reshape(1, -1), wb['wq_pad'], inv_lane)


def _attend_prompt(qt, k, vt, tq, tk, hg):
    s = k.shape[0]
    nkv = s // tk
    k3 = k.reshape(nkv, tk, N_HEADS * HEAD_PAD)
    resident = pl.Buffered(1)
    assert tq == tk
    return pl.pallas_call(
        functools.partial(_attn_prompt_kernel, tq=tq, hg=hg),
        grid=(N_HEADS // hg, s // tq),
        in_specs=[pl.BlockSpec((hg * HEAD_PAD, tq), lambda g, i: (g, i)),
                  pl.BlockSpec((nkv, tk, hg * HEAD_PAD), lambda g, i: (0, 0, g), pipeline_mode=resident),
                  pl.BlockSpec((nkv, hg * V_ROWS, tk), lambda g, i: (0, g, 0), pipeline_mode=resident)],
        out_specs=pl.BlockSpec((hg * V_DIM, tq), lambda g, i: (g, i)),
        out_shape=jax.ShapeDtypeStruct((MLA_WIDTH, s), F32),
        scratch_shapes=[pltpu.VMEM((hg, tk, tq), F32), pltpu.VMEM((hg, tk, tq), F32),
                        pltpu.VMEM((hg * HEAD_PAD, tq), BF16),
                        pltpu.VMEM((hg, 1, tq), F32), pltpu.VMEM((hg, V_ROWS, tq), F32)],
        compiler_params=_params(("arbitrary", "arbitrary")),
        name="attn_prompt",
    )(qt, k3, vt)


def _attend_sample(q, c_new, kr_new, c_hist, kr_hist, wb, t):
    nb, n_past, _ = c_hist.shape
    assert t & (t - 1) == 0
    return pl.pallas_call(
        functools.partial(_attn_sample_kernel, t=t, n_past=n_past),
        grid=(nb,),
        in_specs=[pl.BlockSpec((t, N_HEADS * HEAD_PAD), lambda b: (b, 0)),
                  pl.BlockSpec((t, KV_LORA), lambda b: (b, 0)),
                  pl.BlockSpec((t, QK_ROPE), lambda b: (b, 0)),
                  pl.BlockSpec((None, n_past, KV_LORA), lambda b: (b, 0, 0)),
                  pl.BlockSpec((None, QK_ROPE, n_past), lambda b: (b, 0, 0)),
                  _const_spec(wb['wukt'].shape), _const_spec(wb['wuv_bd'].shape)],
        out_specs=pl.BlockSpec((t, MLA_WIDTH), lambda b: (b, 0)),
        out_shape=jax.ShapeDtypeStruct((nb * t, MLA_WIDTH), F32),
        compiler_params=_params(("arbitrary",)),
        name="attn_sample",
    )(q, c_new, kr_new, c_hist, jnp.swapaxes(kr_hist, 1, 2), wb['wukt'], wb['wuv_bd'])


def _post(x, o, u, uprev, uprev_map, pe, chist, wb, small, *, nseq, seq_len, carry, o_transposed, pos0):
    rows = x.shape[0]
    tm = nseq * seq_len
    n = rows // tm
    assert nseq == 1 if carry else seq_len & (seq_len - 1) == 0
    s_pool, ln1_g, ln1_b, w_dw, b_dw, ln2_g, ln2_b = small
    o_spec = (pl.BlockSpec((MLA_WIDTH, tm), lambda i: (0, i)) if o_transposed
              else pl.BlockSpec((tm, MLA_WIDTH), lambda i: (i, 0)))
    per_tile = (lambda i: (0, 0, 0)) if carry else (lambda i: (i, 0, 0))
    row = lambda v: v.reshape(1, -1)
    return pl.pallas_call(
        functools.partial(_post_kernel, nseq=nseq, seq_len=seq_len, carry=carry, o_transposed=o_transposed,
                          pos0=pos0),
        grid=(n,),
        in_specs=[pl.BlockSpec((tm, D_MODEL), lambda i: (i, 0)),
                  o_spec,
                  pl.BlockSpec((tm, POOL_WIDTH), lambda i: (i, 0)),
                  pl.BlockSpec((nseq, HALO, POOL_WIDTH), uprev_map),
                  pl.BlockSpec((tm, PLE_DIM), lambda i: (i, 0)),
                  pl.BlockSpec((nseq, CONV_W - 1, 2 * D_FF), per_tile),
                  _const_spec(wb['wpool'].shape), _const_spec((1, POOL_WIDTH)), _const_spec(wb['wo'].shape),
                  _const_spec((1, D_MODEL)), _const_spec((1, D_MODEL)),
                  _const_spec(wb['wup'].shape), _const_spec((CONV_W, 2 * D_FF)), _const_spec((1, 2 * D_FF)),
                  _const_spec(wb['wdown'].shape), _const_spec(wb['wpg'].shape), _const_spec(wb['wpe'].shape),
                  _const_spec((1, D_MODEL)), _const_spec((1, D_MODEL))],
        out_specs=[pl.BlockSpec((tm, D_MODEL), lambda i: (i, 0)),
                   pl.BlockSpec((nseq, CONV_HALO, 2 * D_FF), per_tile)],
        out_shape=[jax.ShapeDtypeStruct((rows, D_MODEL), F32),
                   jax.ShapeDtypeStruct((nseq if carry else n * nseq, CONV_HALO, 2 * D_FF), F32)],
        scratch_shapes=[pltpu.VMEM((nseq * (seq_len + HALO), POOL_WIDTH), F32),
                        pltpu.VMEM((nseq * (seq_len + CONV_HALO), FF_CHUNK), F32),
                        pltpu.VMEM((nseq * (seq_len + CONV_HALO), FF_CHUNK), F32),
                        pltpu.VMEM((tm, D_FF), BF16),
                        pltpu.VMEM((nseq, CONV_HALO, 2 * D_FF), F32)],
        compiler_params=_params(("arbitrary",)),
        name="post_prompt" if carry else "post_sample",
    )(x, o, u, uprev, pe, chist,
      wb['wpool'], row(s_pool), wb['wo'], row(ln1_g), row(ln1_b), wb['wup'], w_dw, row(b_dw), wb['wdown'],
      wb['wpg'], wb['wpe'], row(ln2_g), row(ln2_b))


def _layer_prompt(x, pe, wb, g_q, g_kv, small, *, tm_proj, tq, tk, hg, tm_post):
    s = x.shape[0]
    qt, k, vt, c, kr, u = _project_prompt(x, wb, g_q, g_kv, tm_proj, tk)
    ot = _attend_prompt(qt, k, vt, tq, tk, hg)
    per = tm_post // HALO
    y, clast = _post(x, ot, u, u.reshape(s // HALO, HALO, POOL_WIDTH),
                     lambda i: (jnp.maximum(i * per - 1, 0), 0, 0),
                     pe, jnp.zeros((1, CONV_W - 1, 2 * D_FF), F32),
                     wb, small, nseq=1, seq_len=tm_post, carry=True, o_transposed=True, pos0=0)
    return y, c, kr.T, u[s - POOL_HIST:], clast[0, CONV_HALO - (CONV_W - 1):]


def _layer_sample(x, pe, c_hist, kr_hist, pool_hist, conv_hist, wb, g_q, g_kv, small):
    nb, t, _ = x.shape
    n_past = c_hist.shape[1]
    xf = x.reshape(nb * t, D_MODEL)
    q, c, kr, u = _project_sample(xf, wb, g_q, g_kv, t, n_past)
    o = _attend_sample(q, c, kr, c_hist, kr_hist, wb, t)
    uprev = jnp.pad(pool_hist, ((0, 0), (HALO - POOL_HIST, 0), (0, 0)))
    y, clast = _post(xf, o, u, uprev, lambda i: (i, 0, 0), pe.reshape(nb * t, PLE_DIM), conv_hist,
                     wb, small, nseq=nb, seq_len=t, carry=False, o_transposed=False, pos0=n_past)
    u3 = u.reshape(nb, t, POOL_WIDTH)
    new_pool = jnp.concatenate([pool_hist, u3], axis=1)[:, -POOL_HIST:]
    return (y.reshape(nb, t, D_MODEL), c.reshape(nb, t, KV_LORA), kr.reshape(nb, t, QK_ROPE), new_pool,
            clast[:, CONV_HALO - (CONV_W - 1):])


def kernel(x_prompt, x_sample, cache_ckv, cache_krope, state_pool, state_ffn_conv, p_prompt, p_sample,
           w_in, g_q, w_q_b, g_kv, w_kv_b, w_pool, s_pool, w_o, ln1_g, ln1_b,
           w_up, w_dw, b_dw, w_down, w_pg, w_pe, ln2_g, ln2_b):
    assert x_prompt.shape[0] == 1 and w_in.shape[0] == DEPTH
    wb = _prep_weights(w_in[0], w_q_b[0], w_kv_b[0], w_pool[0], w_o[0], w_up[0], w_down[0], w_pg[0], w_pe[0])
    small = (s_pool[0], ln1_g[0], ln1_b[0], w_dw[0], b_dw[0], ln2_g[0], ln2_b[0])
    s = x_prompt.shape[1]
    tm_proj = min(512, s)
    tq = tk = min(512, s)
    yp, cp, krp, poolp, convp = _layer_prompt(x_prompt[0], p_prompt[0, 0], wb, g_q[0], g_kv[0], small,
                                              tm_proj=tm_proj, tq=tq, tk=tk, hg=ATTN_HEAD_GROUP, tm_post=min(256, s))
    ys, cs, krs, pools, convs = _layer_sample(x_sample, p_sample[0], cache_ckv[0], cache_krope[0],
                                              state_pool[0], state_ffn_conv[0], wb, g_q[0], g_kv[0], small)
    return (yp[None], ys, cp[None, None], krp[None, None], poolp[None, None], convp[None, None],
            cs[None], krs[None], pools[None], convs[None])
```

```python
import functools
import math

import jax
import jax.numpy as jnp
from jax import lax
from jax.experimental import pallas as pl
from jax.experimental.pallas import tpu as pltpu

F32 = jnp.float32
BF16 = jnp.bfloat16

D_MODEL = 1024
CHUNK = 64
CHUNK_SHIFT = 6
ATTN_HEAD_GROUP = 4
N_HEADS = 8
QK_NOPE = 64
QK_ROPE = 32
V_DIM = 64
V_ROWS = V_DIM + 16
Q_LORA = 256
KV_LORA = 256
ROPE_THETA = 10000.0
MLA_WIDTH = N_HEADS * V_DIM
POOL_WINDOWS = (2, 4, 8, 16)
POOL_GROUP_W = 128
POOL_WIDTH = D_MODEL - MLA_WIDTH
POOL_HIST = max(POOL_WINDOWS) - 1
D_FF = 2816
CONV_W = 3
PLE_DIM = 256
DEPTH = 1
ALPHA = (2 * DEPTH) ** 0.25
LN_EPS = 1e-5
RMS_EPS = 1e-6
NEG = -1e30
ATTN_SCALE = 1.0 / math.sqrt(QK_NOPE + QK_ROPE)

LANES = 128
HEAD_PAD = 128
ROPE_LO = QK_NOPE
ROPE_MID = QK_NOPE + QK_ROPE // 2
ROPE_HI = QK_NOPE + QK_ROPE
SHIFT_LANE = ROPE_HI
HALO = 16
CONV_HALO = 8
FF_CHUNK = 256
Q_SCALE = ATTN_SCALE * math.log2(math.e)
VMEM_LIMIT = 56 * 1024 * 1024

NT_DIMS = (((1,), (1,)), ((), ()))


def _rms(x, g):
    return x * lax.rsqrt(jnp.mean(x * x, axis=-1, keepdims=True) + RMS_EPS) * g


def _layer_norm(x, g, b):
    mu = jnp.mean(x, axis=-1, keepdims=True)
    xc = x - mu
    var = jnp.mean(xc * xc, axis=-1, keepdims=True)
    return xc * lax.rsqrt(var + LN_EPS) * g + b


def _token_rope_tables(pos_col, inv_lane):
    ang = pos_col.astype(F32) * inv_lane
    lane = lax.broadcasted_iota(jnp.int32, ang.shape, 1)
    first = lane < ROPE_MID
    sin = jnp.sin(ang)
    return jnp.cos(ang), jnp.where(first, -sin, sin), first


def _token_rope(x, cos, sin_signed, first):
    partner = jnp.where(first, pltpu.roll(x, LANES - QK_ROPE // 2, 1), pltpu.roll(x, QK_ROPE // 2, 1))
    return x * cos + partner * sin_signed


def _proj_prompt_kernel(x_ref, win_ref, gq_ref, gkv_ref, wqt_ref, wuk_ref, wuvt_ref, invs_ref,
                        qt_ref, k_ref, vt_ref, c_ref, kr_ref, u_ref, *, tm, tk, pos0):
    i = pl.program_id(0)
    z = jnp.dot(x_ref[...].astype(BF16), win_ref[...], preferred_element_type=F32)
    q_a = z[:, 0:Q_LORA]
    c_raw = z[:, Q_LORA:Q_LORA + KV_LORA]
    kr_pad = z[:, 512:640]
    u_ref[...] = z[:, 640:640 + POOL_WIDTH]

    qn = _rms(q_a, gq_ref[...]).astype(BF16)
    c = _rms(c_raw, gkv_ref[...])
    c_ref[...] = c
    cb = c.astype(BF16)

    pos_row = pos0 + i * tm + lax.broadcasted_iota(jnp.int32, (QK_ROPE // 2, tm), 1)
    ang = pos_row.astype(F32) * invs_ref[...]
    cos_t, sin_t = jnp.cos(ang), jnp.sin(ang)
    lo, hi = (ROPE_LO, tm), (HEAD_PAD - ROPE_HI, tm)
    cos = jnp.concatenate([jnp.ones(lo, F32), cos_t, cos_t, jnp.ones(hi, F32)], axis=0).T
    sin_signed = jnp.concatenate([jnp.zeros(lo, F32), -sin_t, sin_t, jnp.zeros(hi, F32)], axis=0).T
    first = lax.broadcasted_iota(jnp.int32, (tm, LANES), 1) < ROPE_MID
    kr_rot = _token_rope(kr_pad, cos, sin_signed, first)
    kr_ref[...] = kr_rot.T[ROPE_LO:ROPE_HI, :]

    k_nope = jnp.dot(cb, wuk_ref[...], preferred_element_type=F32)
    k_tail = kr_rot + jnp.where(lax.broadcasted_iota(jnp.int32, (tm, LANES), 1) == SHIFT_LANE, 1.0, 0.0)
    for h in range(N_HEADS):
        sl = slice(h * HEAD_PAD, (h + 1) * HEAD_PAD)
        k_ref[:, sl] = (k_nope[:, sl] + k_tail).astype(BF16)

    vt = lax.dot_general(wuvt_ref[...], cb, NT_DIMS, preferred_element_type=F32)
    ones = jnp.ones((V_ROWS - V_DIM, tk), BF16)
    for s in range(tm // tk):
        for h in range(N_HEADS):
            vt_ref[s, h * V_ROWS:h * V_ROWS + V_DIM, :] = vt[h * V_DIM:(h + 1) * V_DIM, s * tk:(s + 1) * tk].astype(BF16)
            vt_ref[s, h * V_ROWS + V_DIM:(h + 1) * V_ROWS, :] = ones

    qt = lax.dot_general(wqt_ref[...], qn, NT_DIMS, preferred_element_type=F32)
    shift_row = lax.broadcasted_iota(jnp.int32, (HEAD_PAD - SHIFT_LANE, tm), 0) == 0
    for h in range(N_HEADS):
        b0 = h * HEAD_PAD
        qt_ref[b0:b0 + ROPE_LO, :] = (qt[b0:b0 + ROPE_LO] * Q_SCALE).astype(BF16)
        x1 = qt[b0 + ROPE_LO:b0 + ROPE_MID]
        x2 = qt[b0 + ROPE_MID:b0 + ROPE_HI]
        qt_ref[b0 + ROPE_LO:b0 + ROPE_MID, :] = ((x1 * cos_t - x2 * sin_t) * Q_SCALE).astype(BF16)
        qt_ref[b0 + ROPE_MID:b0 + ROPE_HI, :] = ((x2 * cos_t + x1 * sin_t) * Q_SCALE).astype(BF16)
        qt_ref[b0 + ROPE_HI:b0 + HEAD_PAD, :] = jnp.zeros((HEAD_PAD - ROPE_HI, tm), BF16)
        lead = jnp.dot(k_ref[0:CHUNK, b0:b0 + HEAD_PAD], qt_ref[b0:b0 + HEAD_PAD, :],
                       preferred_element_type=F32)
        shift = jnp.max(lead, axis=0, keepdims=True)
        qt_ref[b0 + SHIFT_LANE:b0 + HEAD_PAD, :] = jnp.where(shift_row, -shift, 0.0).astype(BF16)


def _proj_sample_kernel(x_ref, win_ref, gq_ref, gkv_ref, wq_ref, invl_ref,
                        q_ref, c_ref, kr_ref, u_ref, *, tm, seq_len, pos0):
    z = jnp.dot(x_ref[...].astype(BF16), win_ref[...], preferred_element_type=F32)
    q_a = z[:, 0:Q_LORA]
    c_raw = z[:, Q_LORA:Q_LORA + KV_LORA]
    kr_pad = z[:, 512:640]
    u_ref[...] = z[:, 640:640 + POOL_WIDTH]

    qn = _rms(q_a, gq_ref[...]).astype(BF16)
    c_ref[...] = _rms(c_raw, gkv_ref[...])

    row = lax.broadcasted_iota(jnp.int32, (tm, LANES), 0)
    pos_col = pos0 + (row & (seq_len - 1))
    cos, sin_signed, first = _token_rope_tables(pos_col, invl_ref[...])
    kr_ref[...] = _token_rope(kr_pad, cos, sin_signed, first)[:, ROPE_LO:ROPE_HI]

    q = jnp.dot(qn, wq_ref[...], preferred_element_type=F32)
    for h in range(N_HEADS):
        sl = slice(h * HEAD_PAD, (h + 1) * HEAD_PAD)
        q_ref[:, sl] = _token_rope(q[:, sl], cos, sin_signed, first) * Q_SCALE


def _attn_prompt_kernel(qt_ref, qn_ref, k_ref, vt_ref, o_ref, sa_ref, sb_ref, m_ref, acc_ref, *, tq, hg, nq):
    i = pl.program_id(1)

    def scores(j, h):
        return jnp.dot(k_ref[j, :, h * HEAD_PAD:(h + 1) * HEAD_PAD], qt_ref[h * HEAD_PAD:(h + 1) * HEAD_PAD, :],
                       preferred_element_type=F32)

    kchunk = lax.broadcasted_iota(jnp.int32, (tq, tq), 0) >> CHUNK_SHIFT
    qchunk = lax.broadcasted_iota(jnp.int32, (tq, tq), 1) >> CHUNK_SHIFT
    visible = kchunk <= qchunk

    def values(jv, h, p):
        return jnp.dot(vt_ref[jv, h * V_ROWS:(h + 1) * V_ROWS, :], p, preferred_element_type=F32)

    def fixed_shift_step(cur_ref, jv, issue_next):
        for h in range(hg):
            p = jnp.exp2(cur_ref[h]).astype(BF16)
            issue_next(h)
            acc_ref[h] += values(jv, h, p)

    def running_max_step(cur_ref, jv, issue_next):
        for h in range(hg):
            s = cur_ref[h]
            m_old = m_ref[h]
            m_new = jnp.maximum(m_old, jnp.max(s, axis=0, keepdims=True))
            alpha = jnp.exp2(m_old - m_new)
            p = jnp.exp2(s - m_new).astype(BF16)
            m_ref[h] = m_new
            issue_next(h)
            acc_ref[h] = alpha * acc_ref[h] + values(jv, h, p)

    def block_into(bank_ref, j):
        def issue(h):
            bank_ref[h] = scores(j, h)
        return issue

    def next_diagonal_into_a(h):
        j = jnp.minimum(i + 1, nq - 1)
        s = jnp.dot(k_ref[j, :, h * HEAD_PAD:(h + 1) * HEAD_PAD], qn_ref[h * HEAD_PAD:(h + 1) * HEAD_PAD, :],
                    preferred_element_type=F32)
        sa_ref[h] = jnp.where(visible, s, NEG)

    def sweep(step):
        def pair(t, c):
            step(sa_ref, jnp.where(t == 0, i, 2 * t - 1), block_into(sb_ref, 2 * t))
            step(sb_ref, 2 * t, block_into(sa_ref, 2 * t + 1))
            return c

        lax.fori_loop(0, i // 2, pair, 0)

        @pl.when(i % 2 == 0)
        def _():
            step(sa_ref, jnp.maximum(i - 1, 0), next_diagonal_into_a)

        @pl.when(i % 2 == 1)
        def _():
            step(sa_ref, jnp.where(i == 1, i, i - 2), block_into(sb_ref, i - 1))
            step(sb_ref, i - 1, next_diagonal_into_a)

    def load_diagonal():
        for h in range(hg):
            sa_ref[h] = jnp.where(visible, scores(i, h), NEG)

    acc_ref[...] = jnp.zeros(acc_ref.shape, F32)
    pl.when(i == 0)(load_diagonal)
    sweep(fixed_shift_step)

    bad = jnp.where(jnp.isfinite(acc_ref[...]), 0.0, 1.0)
    bad = jnp.max(jnp.max(jnp.max(bad, axis=0), axis=0, keepdims=True), axis=1, keepdims=True)

    @pl.when(bad[0, 0] > 0.0)
    def _():
        m_ref[...] = jnp.full(m_ref.shape, NEG, F32)
        acc_ref[...] = jnp.zeros(acc_ref.shape, F32)
        load_diagonal()
        sweep(running_max_step)

    for h in range(hg):
        o_ref[h * V_DIM:(h + 1) * V_DIM, :] = acc_ref[h, 0:V_DIM, :] / acc_ref[h, V_DIM:V_DIM + 1, :]


def _attn_sample_kernel(q_ref, cn_ref, krn_ref, ch_ref, krht_ref, wukt_ref, wuvbd_ref, o_ref, *, t, n_past, nbs):
    for e in range(nbs):
        _attn_sample_one(q_ref[e * t:(e + 1) * t, :], cn_ref[e * t:(e + 1) * t, :], krn_ref[e * t:(e + 1) * t, :],
                         ch_ref.at[e], krht_ref.at[e], wukt_ref, wuvbd_ref, o_ref.at[e * t:(e + 1) * t, :],
                         t=t, n_past=n_past)


def _attn_sample_one(q, c_new, kr_new, ch_ref, krht_ref, wukt_ref, wuvbd_ref, o_ref, *, t, n_past):
    qlat, qrope = [], []
    for h in range(N_HEADS):
        b0 = h * HEAD_PAD
        qn = q[:, b0:b0 + QK_NOPE].astype(BF16)
        qlat.append(jnp.dot(qn, wukt_ref[h], preferred_element_type=F32))
        qrope.append(q[:, b0 + ROPE_LO:b0 + ROPE_HI])
    ql = jnp.concatenate(qlat, axis=0).astype(BF16)
    qr = jnp.concatenate(qrope, axis=0).astype(BF16)

    chb = ch_ref[...].astype(BF16)
    krhb = krht_ref[...].astype(BF16)
    cnb = c_new.astype(BF16)
    krnb = kr_new.astype(BF16)
    s_h = (lax.dot_general(ql, chb, NT_DIMS, preferred_element_type=F32)
           + jnp.dot(qr, krhb, preferred_element_type=F32))
    s_n = (lax.dot_general(ql, cnb, NT_DIMS, preferred_element_type=F32)
           + lax.dot_general(qr, krnb, NT_DIMS, preferred_element_type=F32))

    rows = N_HEADS * t
    qchunk = (n_past + (lax.broadcasted_iota(jnp.int32, (rows, 1), 0) & (t - 1))) >> CHUNK_SHIFT
    s_h = jnp.where((lax.broadcasted_iota(jnp.int32, (rows, n_past), 1) >> CHUNK_SHIFT) <= qchunk, s_h, NEG)
    s_n = jnp.where(((n_past + lax.broadcasted_iota(jnp.int32, (rows, t), 1)) >> CHUNK_SHIFT) <= qchunk, s_n, NEG)

    m = jnp.maximum(jnp.max(s_h, axis=-1, keepdims=True), jnp.max(s_n, axis=-1, keepdims=True))
    p_h = jnp.exp2(s_h - m)
    p_n = jnp.exp2(s_n - m)
    l = jnp.sum(p_h, axis=-1, keepdims=True) + jnp.sum(p_n, axis=-1, keepdims=True)
    olat = (jnp.dot(p_h.astype(BF16), chb, preferred_element_type=F32)
            + jnp.dot(p_n.astype(BF16), cnb, preferred_element_type=F32)) / l
    wide = jnp.concatenate([olat[h * t:(h + 1) * t] for h in range(N_HEADS)], axis=1).astype(BF16)
    o_ref[...] = jnp.dot(wide, wuvbd_ref[...], preferred_element_type=F32)


def _post_kernel(x_ref, o_ref, u_ref, uprev_ref, pe_ref, chist_ref,
                 wpool_ref, spool_ref, wo_ref, ln1g_ref, ln1b_ref, wup_ref, wdw_ref, bdw_ref, wdown_ref,
                 wpg_ref, wpe_ref, ln2g_ref, ln2b_ref,
                 y_ref, clast_ref,
                 ubuf, bufa, bufb, hbuf, cbuf, *, nseq, seq_len, carry, o_transposed, pos0):
    i = pl.program_id(0)
    tm = nseq * seq_len
    hist_rows = slice(CONV_HALO - (CONV_W - 1), CONV_HALO)

    def init_conv_history():
        cbuf[...] = jnp.zeros_like(cbuf)
        cbuf[:, hist_rows, :] = chist_ref[...]

    if carry:
        pl.when(i == 0)(init_conv_history)
    else:
        init_conv_history()

    def gather(buf, halo, shift, cols):
        stride = halo + seq_len
        pieces = [buf[q * stride + halo - shift:q * stride + halo - shift + seq_len, cols] for q in range(nseq)]
        return pieces[0] if nseq == 1 else jnp.concatenate(pieces, axis=0)

    u = u_ref[...]
    for q in range(nseq):
        uprev = uprev_ref[q]
        if carry:
            uprev = jnp.where(i > 0, uprev, 0.0)
        ubuf[q * (HALO + seq_len):q * (HALO + seq_len) + HALO, :] = uprev
        ubuf[q * (HALO + seq_len) + HALO:(q + 1) * (HALO + seq_len), :] = u[q * seq_len:(q + 1) * seq_len]
    row = lax.broadcasted_iota(jnp.int32, (tm, POOL_GROUP_W), 0)
    frame = row + i * tm if carry else row & (seq_len - 1)
    pooled = []
    for g, w in enumerate(POOL_WINDOWS):
        cols = slice(g * POOL_GROUP_W, (g + 1) * POOL_GROUP_W)
        win = u[:, cols]
        for k in range(1, w):
            win = win + gather(ubuf, HALO, k, cols)
        cnt = jnp.minimum(w, pos0 + frame + 1).astype(F32)
        d = win / cnt - u[:, cols]
        yg = jnp.dot(d.astype(BF16), wpool_ref[g], preferred_element_type=F32) * spool_ref[:, cols]
        pooled.append(yg.astype(BF16))
    pooled = jnp.concatenate(pooled, axis=1)

    attn = o_ref[...].T if o_transposed else o_ref[...]
    mix = (jnp.dot(attn.astype(BF16), wo_ref[0:MLA_WIDTH, :], preferred_element_type=F32)
           + jnp.dot(pooled, wo_ref[MLA_WIDTH:D_MODEL, :], preferred_element_type=F32))
    x1 = _layer_norm(ALPHA * x_ref[...] + mix, ln1g_ref[...], ln1b_ref[...])
    x1b = x1.astype(BF16)

    def conv_chunk(cols, buf):
        up = jnp.dot(x1b, wup_ref[:, cols], preferred_element_type=F32)
        stride = CONV_HALO + seq_len
        for q in range(nseq):
            buf[q * stride:q * stride + CONV_HALO, :] = cbuf[q, :, cols]
            buf[q * stride + CONV_HALO:(q + 1) * stride, :] = up[q * seq_len:(q + 1) * seq_len]
            cbuf[q, :, cols] = buf[q * stride + seq_len:(q + 1) * stride, :]
        everything = slice(None)
        return (gather(buf, CONV_HALO, 2, everything) * wdw_ref[0:1, cols]
                + gather(buf, CONV_HALO, 1, everything) * wdw_ref[1:2, cols]
                + up * wdw_ref[2:3, cols] + bdw_ref[:, cols])

    for j in range(D_FF // FF_CHUNK):
        a = conv_chunk(slice(j * FF_CHUNK, (j + 1) * FF_CHUNK), bufa)
        b = conv_chunk(slice(D_FF + j * FF_CHUNK, D_FF + (j + 1) * FF_CHUNK), bufb)
        hbuf[:, j * FF_CHUNK:(j + 1) * FF_CHUNK] = (a * jax.nn.sigmoid(a) * b).astype(BF16)
    ffn = jnp.dot(hbuf[...], wdown_ref[...], preferred_element_type=F32)
    clast_ref[...] = cbuf[...]

    ple = (jax.nn.sigmoid(jnp.dot(x1b, wpg_ref[...], preferred_element_type=F32))
           * jnp.dot(pe_ref[...].astype(BF16), wpe_ref[...], preferred_element_type=F32))
    y_ref[...] = _layer_norm(ALPHA * x1 + ffn + ple, ln2g_ref[...], ln2b_ref[...])


def _const_spec(shape):
    nd = len(shape)
    return pl.BlockSpec(shape, lambda *_: (0,) * nd)


def _params(semantics):
    return pltpu.CompilerParams(dimension_semantics=semantics, vmem_limit_bytes=VMEM_LIMIT)


def _pad_heads(w, per_head, used):
    k = w.shape[0]
    w = w.reshape(k, N_HEADS, per_head)[:, :, :used]
    return jnp.pad(w, ((0, 0), (0, 0), (0, HEAD_PAD - used))).reshape(k, N_HEADS * HEAD_PAD)


def _prep_weights(w_in, w_q_b, w_kv_b, w_pool, w_o, w_up, w_down, w_pg, w_pe):
    w_kr = jnp.pad(w_in[:, 512:512 + QK_ROPE], ((0, 0), (ROPE_LO, LANES - ROPE_HI)))
    win = jnp.concatenate([w_in[:, :512], w_kr, w_in[:, 512 + QK_ROPE:]], axis=1).astype(BF16)
    wq_pad = _pad_heads(w_q_b, QK_NOPE + QK_ROPE, QK_NOPE + QK_ROPE).astype(BF16)
    w_kv = w_kv_b.reshape(KV_LORA, N_HEADS, QK_NOPE + V_DIM)
    w_uk, w_uv = w_kv[..., :QK_NOPE], w_kv[..., QK_NOPE:]
    wuk_pad = jnp.pad(w_uk, ((0, 0), (0, 0), (0, HEAD_PAD - QK_NOPE))).reshape(KV_LORA, -1).astype(BF16)
    wuvt = w_uv.reshape(KV_LORA, N_HEADS * V_DIM).T.astype(BF16)
    wukt = jnp.transpose(w_uk, (1, 2, 0)).astype(BF16)
    eye = jnp.eye(N_HEADS, dtype=w_uv.dtype)
    wuv_bd = jnp.einsum('lhv,hg->hlgv', w_uv, eye).reshape(N_HEADS * KV_LORA, N_HEADS * V_DIM).astype(BF16)
    return dict(win=win, wq_pad=wq_pad, wqt=wq_pad.T, wuk_pad=wuk_pad, wuvt=wuvt, wukt=wukt, wuv_bd=wuv_bd,
                wpool=w_pool.astype(BF16), wo=w_o.astype(BF16), wup=w_up.astype(BF16),
                wdown=w_down.astype(BF16), wpg=w_pg.astype(BF16), wpe=w_pe.astype(BF16))


def _rope_inv():
    inv = 1.0 / (ROPE_THETA ** (jnp.arange(0, QK_ROPE, 2, dtype=F32) / QK_ROPE))
    inv_lane = jnp.zeros((1, LANES), F32).at[0, ROPE_LO:ROPE_HI].set(jnp.concatenate([inv, inv]))
    return inv_lane, inv[:, None]


def _project_prompt(x, wb, g_q, g_kv, tm, tk):
    s = x.shape[0]
    _, inv_sub = _rope_inv()
    n = s // tm
    outs = pl.pallas_call(
        functools.partial(_proj_prompt_kernel, tm=tm, tk=tk, pos0=0),
        grid=(n,),
        in_specs=[pl.BlockSpec((tm, D_MODEL), lambda i: (i, 0)),
                  _const_spec(wb['win'].shape), _const_spec((1, Q_LORA)), _const_spec((1, KV_LORA)),
                  _const_spec(wb['wqt'].shape), _const_spec(wb['wuk_pad'].shape), _const_spec(wb['wuvt'].shape),
                  _const_spec((QK_ROPE // 2, 1))],
        out_specs=[pl.BlockSpec((N_HEADS * HEAD_PAD, tm), lambda i: (0, i)),
                   pl.BlockSpec((tm, N_HEADS * HEAD_PAD), lambda i: (i, 0)),
                   pl.BlockSpec((tm // tk, N_HEADS * V_ROWS, tk), lambda i: (i, 0, 0)),
                   pl.BlockSpec((tm, KV_LORA), lambda i: (i, 0)),
                   pl.BlockSpec((QK_ROPE, tm), lambda i: (0, i)),
                   pl.BlockSpec((tm, POOL_WIDTH), lambda i: (i, 0))],
        out_shape=[jax.ShapeDtypeStruct((N_HEADS * HEAD_PAD, s), BF16),
                   jax.ShapeDtypeStruct((s, N_HEADS * HEAD_PAD), BF16),
                   jax.ShapeDtypeStruct((s // tk, N_HEADS * V_ROWS, tk), BF16),
                   jax.ShapeDtypeStruct((s, KV_LORA), F32),
                   jax.ShapeDtypeStruct((QK_ROPE, s), F32),
                   jax.ShapeDtypeStruct((s, POOL_WIDTH), F32)],
        compiler_params=_params(("arbitrary",)),
        name="proj_prompt",
    )(x, wb['win'], g_q.reshape(1, -1), g_kv.reshape(1, -1), wb['wqt'], wb['wuk_pad'], wb['wuvt'], inv_sub)
    return outs


def _project_sample(x, wb, g_q, g_kv, seq_len, pos0):
    rows = x.shape[0]
    assert seq_len & (seq_len - 1) == 0
    inv_lane, _ = _rope_inv()
    return pl.pallas_call(
        functools.partial(_proj_sample_kernel, tm=rows, seq_len=seq_len, pos0=pos0),
        grid=(1,),
        in_specs=[_const_spec((rows, D_MODEL)), _const_spec(wb['win'].shape), _const_spec((1, Q_LORA)),
                  _const_spec((1, KV_LORA)), _const_spec(wb['wq_pad'].shape), _const_spec((1, LANES))],
        out_specs=[_const_spec((rows, N_HEADS * HEAD_PAD)), _const_spec((rows, KV_LORA)),
                   _const_spec((rows, QK_ROPE)), _const_spec((rows, POOL_WIDTH))],
        out_shape=[jax.ShapeDtypeStruct((rows, N_HEADS * HEAD_PAD), F32),
                   jax.ShapeDtypeStruct((rows, KV_LORA), F32),
                   jax.ShapeDtypeStruct((rows, QK_ROPE), F32),
                   jax.ShapeDtypeStruct((rows, POOL_WIDTH), F32)],
        compiler_params=_params(("arbitrary",)),
        name="proj_sample",
    )(x, wb['win'], g_q.reshape(1, -1), g_kv.reshape(1, -1), wb['wq_pad'], inv_lane)


def _attend_prompt(qt, k, vt, tq, tk, hg):
    s = k.shape[0]
    nkv = s // tk
    k3 = k.reshape(nkv, tk, N_HEADS * HEAD_PAD)
    resident = pl.Buffered(1)
    assert tq == tk
    nq = s // tq
    return pl.pallas_call(
        functools.partial(_attn_prompt_kernel, tq=tq, hg=hg, nq=nq),
        grid=(N_HEADS // hg, nq),
        in_specs=[pl.BlockSpec((hg * HEAD_PAD, tq), lambda g, i: (g, i)),
                  pl.BlockSpec((hg * HEAD_PAD, tq), lambda g, i: (g, jnp.minimum(i + 1, nq - 1))),
                  pl.BlockSpec((nkv, tk, hg * HEAD_PAD), lambda g, i: (0, 0, g), pipeline_mode=resident),
                  pl.BlockSpec((nkv, hg * V_ROWS, tk), lambda g, i: (0, g, 0), pipeline_mode=resident)],
        out_specs=pl.BlockSpec((hg * V_DIM, tq), lambda g, i: (g, i)),
        out_shape=jax.ShapeDtypeStruct((MLA_WIDTH, s), F32),
        scratch_shapes=[pltpu.VMEM((hg, tk, tq), F32), pltpu.VMEM((hg, tk, tq), F32),
                        pltpu.VMEM((hg, 1, tq), F32), pltpu.VMEM((hg, V_ROWS, tq), F32)],
        compiler_params=_params(("arbitrary", "arbitrary")),
        name="attn_prompt",
    )(qt, qt, k3, vt)


def _attend_sample(q, c_new, kr_new, c_hist, kr_hist, wb, t):
    nb, n_past, _ = c_hist.shape
    assert t & (t - 1) == 0
    nbs = 2 if nb % 2 == 0 else 1
    return pl.pallas_call(
        functools.partial(_attn_sample_kernel, t=t, n_past=n_past, nbs=nbs),
        grid=(nb // nbs,),
        in_specs=[pl.BlockSpec((nbs * t, N_HEADS * HEAD_PAD), lambda b: (b, 0)),
                  pl.BlockSpec((nbs * t, KV_LORA), lambda b: (b, 0)),
                  pl.BlockSpec((nbs * t, QK_ROPE), lambda b: (b, 0)),
                  pl.BlockSpec((nbs, n_past, KV_LORA), lambda b: (b, 0, 0)),
                  pl.BlockSpec((nbs, QK_ROPE, n_past), lambda b: (b, 0, 0)),
                  _const_spec(wb['wukt'].shape), _const_spec(wb['wuv_bd'].shape)],
        out_specs=pl.BlockSpec((nbs * t, MLA_WIDTH), lambda b: (b, 0)),
        out_shape=jax.ShapeDtypeStruct((nb * t, MLA_WIDTH), F32),
        compiler_params=_params(("arbitrary",)),
        name="attn_sample",
    )(q, c_new, kr_new, c_hist, jnp.swapaxes(kr_hist, 1, 2), wb['wukt'], wb['wuv_bd'])


def _post(x, o, u, uprev, uprev_map, pe, chist, wb, small, *, nseq, seq_len, carry, o_transposed, pos0):
    rows = x.shape[0]
    tm = nseq * seq_len
    n = rows // tm
    assert nseq == 1 if carry else seq_len & (seq_len - 1) == 0
    s_pool, ln1_g, ln1_b, w_dw, b_dw, ln2_g, ln2_b = small
    o_spec = (pl.BlockSpec((MLA_WIDTH, tm), lambda i: (0, i)) if o_transposed
              else pl.BlockSpec((tm, MLA_WIDTH), lambda i: (i, 0)))
    per_tile = (lambda i: (0, 0, 0)) if carry else (lambda i: (i, 0, 0))
    row = lambda v: v.reshape(1, -1)
    return pl.pallas_call(
        functools.partial(_post_kernel, nseq=nseq, seq_len=seq_len, carry=carry, o_transposed=o_transposed,
                          pos0=pos0),
        grid=(n,),
        in_specs=[pl.BlockSpec((tm, D_MODEL), lambda i: (i, 0)),
                  o_spec,
                  pl.BlockSpec((tm, POOL_WIDTH), lambda i: (i, 0)),
                  pl.BlockSpec((nseq, HALO, POOL_WIDTH), uprev_map),
                  pl.BlockSpec((tm, PLE_DIM), lambda i: (i, 0)),
                  pl.BlockSpec((nseq, CONV_W - 1, 2 * D_FF), per_tile),
                  _const_spec(wb['wpool'].shape), _const_spec((1, POOL_WIDTH)), _const_spec(wb['wo'].shape),
                  _const_spec((1, D_MODEL)), _const_spec((1, D_MODEL)),
                  _const_spec(wb['wup'].shape), _const_spec((CONV_W, 2 * D_FF)), _const_spec((1, 2 * D_FF)),
                  _const_spec(wb['wdown'].shape), _const_spec(wb['wpg'].shape), _const_spec(wb['wpe'].shape),
                  _const_spec((1, D_MODEL)), _const_spec((1, D_MODEL))],
        out_specs=[pl.BlockSpec((tm, D_MODEL), lambda i: (i, 0)),
                   pl.BlockSpec((nseq, CONV_HALO, 2 * D_FF), per_tile)],
        out_shape=[jax.ShapeDtypeStruct((rows, D_MODEL), F32),
                   jax.ShapeDtypeStruct((nseq if carry else n * nseq, CONV_HALO, 2 * D_FF), F32)],
        scratch_shapes=[pltpu.VMEM((nseq * (seq_len + HALO), POOL_WIDTH), F32),
                        pltpu.VMEM((nseq * (seq_len + CONV_HALO), FF_CHUNK), F32),
                        pltpu.VMEM((nseq * (seq_len + CONV_HALO), FF_CHUNK), F32),
                        pltpu.VMEM((tm, D_FF), BF16),
                        pltpu.VMEM((nseq, CONV_HALO, 2 * D_FF), F32)],
        compiler_params=_params(("arbitrary",)),
        name="post_prompt" if carry else "post_sample",
    )(x, o, u, uprev, pe, chist,
      wb['wpool'], row(s_pool), wb['wo'], row(ln1_g), row(ln1_b), wb['wup'], w_dw, row(b_dw), wb['wdown'],
      wb['wpg'], wb['wpe'], row(ln2_g), row(ln2_b))


def _layer_prompt(x, pe, wb, g_q, g_kv, small, *, tm_proj, tq, tk, hg, tm_post):
    s = x.shape[0]
    assert tm_proj == tq
    qt, k, vt, c, kr, u = _project_prompt(x, wb, g_q, g_kv, tm_proj, tk)
    ot = _attend_prompt(qt, k, vt, tq, tk, hg)
    per = tm_post // HALO
    y, clast = _post(x, ot, u, u.reshape(s // HALO, HALO, POOL_WIDTH),
                     lambda i: (jnp.maximum(i * per - 1, 0), 0, 0),
                     pe, jnp.zeros((1, CONV_W - 1, 2 * D_FF), F32),
                     wb, small, nseq=1, seq_len=tm_post, carry=True, o_transposed=True, pos0=0)
    return y, c, kr.T, u[s - POOL_HIST:], clast[0, CONV_HALO - (CONV_W - 1):]


def _layer_sample(x, pe, c_hist, kr_hist, pool_hist, conv_hist, wb, g_q, g_kv, small):
    nb, t, _ = x.shape
    n_past = c_hist.shape[1]
    xf = x.reshape(nb * t, D_MODEL)
    q, c, kr, u = _project_sample(xf, wb, g_q, g_kv, t, n_past)
    o = _attend_sample(q, c, kr, c_hist, kr_hist, wb, t)
    uprev = jnp.pad(pool_hist, ((0, 0), (HALO - POOL_HIST, 0), (0, 0)))
    y, clast = _post(xf, o, u, uprev, lambda i: (i, 0, 0), pe.reshape(nb * t, PLE_DIM), conv_hist,
                     wb, small, nseq=nb, seq_len=t, carry=False, o_transposed=False, pos0=n_past)
    u3 = u.reshape(nb, t, POOL_WIDTH)
    new_pool = jnp.concatenate([pool_hist, u3], axis=1)[:, -POOL_HIST:]
    return (y.reshape(nb, t, D_MODEL), c.reshape(nb, t, KV_LORA), kr.reshape(nb, t, QK_ROPE), new_pool,
            clast[:, CONV_HALO - (CONV_W - 1):])


def kernel(x_prompt, x_sample, cache_ckv, cache_krope, state_pool, state_ffn_conv, p_prompt, p_sample,
           w_in, g_q, w_q_b, g_kv, w_kv_b, w_pool, s_pool, w_o, ln1_g, ln1_b,
           w_up, w_dw, b_dw, w_down, w_pg, w_pe, ln2_g, ln2_b):
    assert x_prompt.shape[0] == 1 and w_in.shape[0] == DEPTH
    wb = _prep_weights(w_in[0], w_q_b[0], w_kv_b[0], w_pool[0], w_o[0], w_up[0], w_down[0], w_pg[0], w_pe[0])
    small = (s_pool[0], ln1_g[0], ln1_b[0], w_dw[0], b_dw[0], ln2_g[0], ln2_b[0])
    s = x_prompt.shape[1]
    tm_proj = min(512, s)
    tq = tk = min(512, s)
    yp, cp, krp, poolp, convp = _layer_prompt(x_prompt[0], p_prompt[0, 0], wb, g_q[0], g_kv[0], small,
                                              tm_proj=tm_proj, tq=tq, tk=tk, hg=ATTN_HEAD_GROUP, tm_post=min(256, s))
    ys, cs, krs, pools, convs = _layer_sample(x_sample, p_sample[0], cache_ckv[0], cache_krope[0],
                                              state_pool[0], state_ffn_conv[0], wb, g_q[0], g_kv[0], small)
    return (yp[None], ys, cp[None, None], krp[None, None], poolp[None, None], convp[None, None],
            cs[None], krs[None], pools[None], convs[None])
```

```python
import functools
import math

import jax
import jax.numpy as jnp
from jax import lax
from jax.experimental import pallas as pl
from jax.experimental.pallas import tpu as pltpu

F32 = jnp.float32
BF16 = jnp.bfloat16

D_MODEL = 1024
CHUNK = 64
CHUNK_SHIFT = 6
ATTN_HEAD_GROUP = 4
N_HEADS = 8
QK_NOPE = 64
QK_ROPE = 32
V_DIM = 64
V_ROWS = V_DIM + 16
Q_LORA = 256
KV_LORA = 256
ROPE_THETA = 10000.0
MLA_WIDTH = N_HEADS * V_DIM
POOL_WINDOWS = (2, 4, 8, 16)
POOL_GROUP_W = 128
POOL_WIDTH = D_MODEL - MLA_WIDTH
POOL_HIST = max(POOL_WINDOWS) - 1
D_FF = 2816
CONV_W = 3
PLE_DIM = 256
DEPTH = 1
ALPHA = (2 * DEPTH) ** 0.25
LN_EPS = 1e-5
RMS_EPS = 1e-6
NEG = -1e30
ATTN_SCALE = 1.0 / math.sqrt(QK_NOPE + QK_ROPE)

LANES = 128
HEAD_PAD = 128
ROPE_LO = QK_NOPE
ROPE_MID = QK_NOPE + QK_ROPE // 2
ROPE_HI = QK_NOPE + QK_ROPE
SHIFT_LANE = ROPE_HI
HALO = 16
CONV_HALO = 8
FF_CHUNK = 256
Q_SCALE = ATTN_SCALE * math.log2(math.e)
VMEM_LIMIT = 56 * 1024 * 1024

NT_DIMS = (((1,), (1,)), ((), ()))


def _rms(x, g):
    return x * lax.rsqrt(jnp.mean(x * x, axis=-1, keepdims=True) + RMS_EPS) * g


def _layer_norm(x, g, b):
    mu = jnp.mean(x, axis=-1, keepdims=True)
    xc = x - mu
    var = jnp.mean(xc * xc, axis=-1, keepdims=True)
    return xc * lax.rsqrt(var + LN_EPS) * g + b


def _token_rope_tables(pos_col, inv_lane):
    ang = pos_col.astype(F32) * inv_lane
    lane = lax.broadcasted_iota(jnp.int32, ang.shape, 1)
    first = lane < ROPE_MID
    sin = jnp.sin(ang)
    return jnp.cos(ang), jnp.where(first, -sin, sin), first


def _token_rope(x, cos, sin_signed, first):
    partner = jnp.where(first, pltpu.roll(x, LANES - QK_ROPE // 2, 1), pltpu.roll(x, QK_ROPE // 2, 1))
    return x * cos + partner * sin_signed


def _proj_prompt_kernel(x_ref, win_ref, gq_ref, gkv_ref, wqt_ref, wuk_ref, wuvt_ref, invs_ref,
                        qt_ref, k_ref, vt_ref, c_ref, kr_ref, u_ref, *, tm, tk, pos0):
    i = pl.program_id(0)
    z = jnp.dot(x_ref[...].astype(BF16), win_ref[...], preferred_element_type=F32)
    q_a = z[:, 0:Q_LORA]
    c_raw = z[:, Q_LORA:Q_LORA + KV_LORA]
    kr_pad = z[:, 512:640]
    u_ref[...] = z[:, 640:640 + POOL_WIDTH]

    qn = _rms(q_a, gq_ref[...]).astype(BF16)
    c = _rms(c_raw, gkv_ref[...])
    c_ref[...] = c
    cb = c.astype(BF16)

    pos_row = pos0 + i * tm + lax.broadcasted_iota(jnp.int32, (QK_ROPE // 2, tm), 1)
    ang = pos_row.astype(F32) * invs_ref[...]
    cos_t, sin_t = jnp.cos(ang), jnp.sin(ang)
    lo, hi = (ROPE_LO, tm), (HEAD_PAD - ROPE_HI, tm)
    cos = jnp.concatenate([jnp.ones(lo, F32), cos_t, cos_t, jnp.ones(hi, F32)], axis=0).T
    sin_signed = jnp.concatenate([jnp.zeros(lo, F32), -sin_t, sin_t, jnp.zeros(hi, F32)], axis=0).T
    first = lax.broadcasted_iota(jnp.int32, (tm, LANES), 1) < ROPE_MID
    kr_rot = _token_rope(kr_pad, cos, sin_signed, first)
    kr_ref[...] = kr_rot.T[ROPE_LO:ROPE_HI, :]

    k_nope = jnp.dot(cb, wuk_ref[...], preferred_element_type=F32)
    k_tail = kr_rot + jnp.where(lax.broadcasted_iota(jnp.int32, (tm, LANES), 1) == SHIFT_LANE, 1.0, 0.0)
    for h in range(N_HEADS):
        sl = slice(h * HEAD_PAD, (h + 1) * HEAD_PAD)
        k_ref[:, sl] = (k_nope[:, sl] + k_tail).astype(BF16)

    vt = lax.dot_general(wuvt_ref[...], cb, NT_DIMS, preferred_element_type=F32)
    ones = jnp.ones((V_ROWS - V_DIM, tk), BF16)
    for s in range(tm // tk):
        for h in range(N_HEADS):
            vt_ref[s, h * V_ROWS:h * V_ROWS + V_DIM, :] = vt[h * V_DIM:(h + 1) * V_DIM, s * tk:(s + 1) * tk].astype(BF16)
            vt_ref[s, h * V_ROWS + V_DIM:(h + 1) * V_ROWS, :] = ones

    qt = lax.dot_general(wqt_ref[...], qn, NT_DIMS, preferred_element_type=F32)
    shift_row = lax.broadcasted_iota(jnp.int32, (HEAD_PAD - SHIFT_LANE, tm), 0) == 0
    for h in range(N_HEADS):
        b0 = h * HEAD_PAD
        qt_ref[b0:b0 + ROPE_LO, :] = (qt[b0:b0 + ROPE_LO] * Q_SCALE).astype(BF16)
        x1 = qt[b0 + ROPE_LO:b0 + ROPE_MID]
        x2 = qt[b0 + ROPE_MID:b0 + ROPE_HI]
        qt_ref[b0 + ROPE_LO:b0 + ROPE_MID, :] = ((x1 * cos_t - x2 * sin_t) * Q_SCALE).astype(BF16)
        qt_ref[b0 + ROPE_MID:b0 + ROPE_HI, :] = ((x2 * cos_t + x1 * sin_t) * Q_SCALE).astype(BF16)
        qt_ref[b0 + ROPE_HI:b0 + HEAD_PAD, :] = jnp.zeros((HEAD_PAD - ROPE_HI, tm), BF16)
        lead = jnp.dot(k_ref[0:CHUNK, b0:b0 + HEAD_PAD], qt_ref[b0:b0 + HEAD_PAD, :],
                       preferred_element_type=F32)
        shift = jnp.max(lead, axis=0, keepdims=True)
        qt_ref[b0 + SHIFT_LANE:b0 + HEAD_PAD, :] = jnp.where(shift_row, -shift, 0.0).astype(BF16)


def _proj_sample_kernel(x_ref, win_ref, gq_ref, gkv_ref, wq_ref, invl_ref,
                        q_ref, c_ref, kr_ref, u_ref, *, tm, seq_len, pos0):
    z = jnp.dot(x_ref[...].astype(BF16), win_ref[...], preferred_element_type=F32)
    q_a = z[:, 0:Q_LORA]
    c_raw = z[:, Q_LORA:Q_LORA + KV_LORA]
    kr_pad = z[:, 512:640]
    u_ref[...] = z[:, 640:640 + POOL_WIDTH]

    qn = _rms(q_a, gq_ref[...]).astype(BF16)
    c_ref[...] = _rms(c_raw, gkv_ref[...])

    row = lax.broadcasted_iota(jnp.int32, (tm, LANES), 0)
    pos_col = pos0 + (row & (seq_len - 1))
    cos, sin_signed, first = _token_rope_tables(pos_col, invl_ref[...])
    kr_ref[...] = _token_rope(kr_pad, cos, sin_signed, first)[:, ROPE_LO:ROPE_HI]

    q = jnp.dot(qn, wq_ref[...], preferred_element_type=F32)
    for h in range(N_HEADS):
        sl = slice(h * HEAD_PAD, (h + 1) * HEAD_PAD)
        q_ref[:, sl] = _token_rope(q[:, sl], cos, sin_signed, first) * Q_SCALE


def _attn_prompt_kernel(qt_ref, qn_ref, k_ref, vt_ref, o_ref, sa_ref, sb_ref, m_ref, acc_ref, *, tq, hg, nq):
    i = pl.program_id(1)

    def scores(j, h):
        return jnp.dot(k_ref[j, :, h * HEAD_PAD:(h + 1) * HEAD_PAD], qt_ref[h * HEAD_PAD:(h + 1) * HEAD_PAD, :],
                       preferred_element_type=F32)

    kchunk = lax.broadcasted_iota(jnp.int32, (tq, tq), 0) >> CHUNK_SHIFT
    qchunk = lax.broadcasted_iota(jnp.int32, (tq, tq), 1) >> CHUNK_SHIFT
    visible = kchunk <= qchunk

    def values(jv, h, p):
        return jnp.dot(vt_ref[jv, h * V_ROWS:(h + 1) * V_ROWS, :], p, preferred_element_type=F32)

    def fixed_shift_step(cur_ref, jv, issue_next):
        for h in range(hg):
            p = jnp.exp2(cur_ref[h]).astype(BF16)
            issue_next(h)
            acc_ref[h] += values(jv, h, p)

    def running_max_step(cur_ref, jv, issue_next):
        for h in range(hg):
            s = cur_ref[h]
            m_old = m_ref[h]
            m_new = jnp.maximum(m_old, jnp.max(s, axis=0, keepdims=True))
            alpha = jnp.exp2(m_old - m_new)
            p = jnp.exp2(s - m_new).astype(BF16)
            m_ref[h] = m_new
            issue_next(h)
            acc_ref[h] = alpha * acc_ref[h] + values(jv, h, p)

    def block_into(bank_ref, j):
        def issue(h):
            bank_ref[h] = scores(j, h)
        return issue

    def next_diagonal_into_a(h):
        j = jnp.minimum(i + 1, nq - 1)
        s = jnp.dot(k_ref[j, :, h * HEAD_PAD:(h + 1) * HEAD_PAD], qn_ref[h * HEAD_PAD:(h + 1) * HEAD_PAD, :],
                    preferred_element_type=F32)
        sa_ref[h] = jnp.where(visible, s, NEG)

    def sweep(step, unroll_pairs):
        def pair(t):
            step(sa_ref, jnp.where(t == 0, i, 2 * t - 1), block_into(sb_ref, 2 * t))
            step(sb_ref, 2 * t, block_into(sa_ref, 2 * t + 1))

        n_pairs = i // 2
        if unroll_pairs:
            def two_pairs(u, c):
                pair(2 * u)
                pair(2 * u + 1)
                return c

            lax.fori_loop(0, n_pairs // 2, two_pairs, 0)
            pl.when(n_pairs % 2 == 1)(lambda: pair(n_pairs - 1))
        else:
            def one_pair(t, c):
                pair(t)
                return c

            lax.fori_loop(0, n_pairs, one_pair, 0)

        @pl.when(i % 2 == 0)
        def _():
            step(sa_ref, jnp.maximum(i - 1, 0), next_diagonal_into_a)

        @pl.when(i % 2 == 1)
        def _():
            step(sa_ref, jnp.where(i == 1, i, i - 2), block_into(sb_ref, i - 1))
            step(sb_ref, i - 1, next_diagonal_into_a)

    def load_diagonal():
        for h in range(hg):
            sa_ref[h] = jnp.where(visible, scores(i, h), NEG)

    acc_ref[...] = jnp.zeros(acc_ref.shape, F32)
    pl.when(i == 0)(load_diagonal)
    sweep(fixed_shift_step, unroll_pairs=True)

    bad = jnp.where(jnp.isfinite(acc_ref[...]), 0.0, 1.0)
    bad = jnp.max(jnp.max(jnp.max(bad, axis=0), axis=0, keepdims=True), axis=1, keepdims=True)

    @pl.when(bad[0, 0] > 0.0)
    def _():
        m_ref[...] = jnp.full(m_ref.shape, NEG, F32)
        acc_ref[...] = jnp.zeros(acc_ref.shape, F32)
        load_diagonal()
        sweep(running_max_step, unroll_pairs=False)

    for h in range(hg):
        o_ref[h * V_DIM:(h + 1) * V_DIM, :] = acc_ref[h, 0:V_DIM, :] / acc_ref[h, V_DIM:V_DIM + 1, :]


def _attn_sample_kernel(q_ref, cn_ref, krn_ref, ch_ref, krht_ref, wukt_ref, wuvbd_ref, o_ref, *, t, n_past):
    q = q_ref[...]
    qlat, qrope = [], []
    for h in range(N_HEADS):
        b0 = h * HEAD_PAD
        qn = q[:, b0:b0 + QK_NOPE].astype(BF16)
        qlat.append(jnp.dot(qn, wukt_ref[h], preferred_element_type=F32))
        qrope.append(q[:, b0 + ROPE_LO:b0 + ROPE_HI])
    ql = jnp.concatenate(qlat, axis=0).astype(BF16)
    qr = jnp.concatenate(qrope, axis=0).astype(BF16)

    chb = ch_ref[...].astype(BF16)
    krhb = krht_ref[...].astype(BF16)
    cnb = cn_ref[...].astype(BF16)
    krnb = krn_ref[...].astype(BF16)
    s_h = (lax.dot_general(ql, chb, NT_DIMS, preferred_element_type=F32)
           + jnp.dot(qr, krhb, preferred_element_type=F32))
    s_n = (lax.dot_general(ql, cnb, NT_DIMS, preferred_element_type=F32)
           + lax.dot_general(qr, krnb, NT_DIMS, preferred_element_type=F32))

    rows = N_HEADS * t
    qchunk = (n_past + (lax.broadcasted_iota(jnp.int32, (rows, 1), 0) & (t - 1))) >> CHUNK_SHIFT
    s_h = jnp.where((lax.broadcasted_iota(jnp.int32, (rows, n_past), 1) >> CHUNK_SHIFT) <= qchunk, s_h, NEG)
    s_n = jnp.where(((n_past + lax.broadcasted_iota(jnp.int32, (rows, t), 1)) >> CHUNK_SHIFT) <= qchunk, s_n, NEG)

    m = jnp.maximum(jnp.max(s_h, axis=-1, keepdims=True), jnp.max(s_n, axis=-1, keepdims=True))
    p_h = jnp.exp2(s_h - m)
    p_n = jnp.exp2(s_n - m)
    l = jnp.sum(p_h, axis=-1, keepdims=True) + jnp.sum(p_n, axis=-1, keepdims=True)
    olat = (jnp.dot(p_h.astype(BF16), chb, preferred_element_type=F32)
            + jnp.dot(p_n.astype(BF16), cnb, preferred_element_type=F32)) / l
    wide = jnp.concatenate([olat[h * t:(h + 1) * t] for h in range(N_HEADS)], axis=1).astype(BF16)
    o_ref[...] = jnp.dot(wide, wuvbd_ref[...], preferred_element_type=F32)


def _post_kernel(x_ref, o_ref, u_ref, uprev_ref, pe_ref, chist_ref,
                 wpool_ref, spool_ref, wo_ref, ln1g_ref, ln1b_ref, wup_ref, wdw_ref, bdw_ref, wdown_ref,
                 wpg_ref, wpe_ref, ln2g_ref, ln2b_ref,
                 y_ref, clast_ref,
                 ubuf, convbuf, hbuf, cbuf, *, nseq, seq_len, carry, o_transposed, pos0):
    i = pl.program_id(0)
    tm = nseq * seq_len
    hist_rows = slice(CONV_HALO - (CONV_W - 1), CONV_HALO)

    def init_conv_history():
        cbuf[...] = jnp.zeros_like(cbuf)
        cbuf[:, hist_rows, :] = chist_ref[...]

    if carry:
        pl.when(i == 0)(init_conv_history)
    else:
        init_conv_history()

    def gather(buf, halo, shift, cols):
        stride = halo + seq_len
        pieces = [buf[q * stride + halo - shift:q * stride + halo - shift + seq_len, cols] for q in range(nseq)]
        return pieces[0] if nseq == 1 else jnp.concatenate(pieces, axis=0)

    attn = o_ref[...].T if o_transposed else o_ref[...]
    mix_attn = jnp.dot(attn.astype(BF16), wo_ref[0:MLA_WIDTH, :], preferred_element_type=F32)
    pe_proj = jnp.dot(pe_ref[...].astype(BF16), wpe_ref[...], preferred_element_type=F32)

    u = u_ref[...]
    for q in range(nseq):
        uprev = uprev_ref[q]
        if carry:
            uprev = jnp.where(i > 0, uprev, 0.0)
        ubuf[q * (HALO + seq_len):q * (HALO + seq_len) + HALO, :] = uprev
        ubuf[q * (HALO + seq_len) + HALO:(q + 1) * (HALO + seq_len), :] = u[q * seq_len:(q + 1) * seq_len]
    row = lax.broadcasted_iota(jnp.int32, (tm, POOL_GROUP_W), 0)
    frame = row + i * tm if carry else row & (seq_len - 1)
    pooled = []
    for g, w in enumerate(POOL_WINDOWS):
        cols = slice(g * POOL_GROUP_W, (g + 1) * POOL_GROUP_W)
        win = u[:, cols]
        for k in range(1, w):
            win = win + gather(ubuf, HALO, k, cols)
        cnt = jnp.minimum(w, pos0 + frame + 1).astype(F32)
        d = win / cnt - u[:, cols]
        yg = jnp.dot(d.astype(BF16), wpool_ref[g], preferred_element_type=F32) * spool_ref[:, cols]
        pooled.append(yg.astype(BF16))
    pooled = jnp.concatenate(pooled, axis=1)

    mix = mix_attn + jnp.dot(pooled, wo_ref[MLA_WIDTH:D_MODEL, :], preferred_element_type=F32)
    x1 = _layer_norm(ALPHA * x_ref[...] + mix, ln1g_ref[...], ln1b_ref[...])
    x1b = x1.astype(BF16)

    def conv_chunk(cols, buf):
        up = jnp.dot(x1b, wup_ref[:, cols], preferred_element_type=F32)
        stride = CONV_HALO + seq_len
        for q in range(nseq):
            buf[q * stride:q * stride + CONV_HALO, :] = cbuf[q, :, cols]
            buf[q * stride + CONV_HALO:(q + 1) * stride, :] = up[q * seq_len:(q + 1) * seq_len]
            cbuf[q, :, cols] = buf[q * stride + seq_len:(q + 1) * stride, :]
        everything = slice(None)
        return (gather(buf, CONV_HALO, 2, everything) * wdw_ref[0:1, cols]
                + gather(buf, CONV_HALO, 1, everything) * wdw_ref[1:2, cols]
                + up * wdw_ref[2:3, cols] + bdw_ref[:, cols])

    for j in range(D_FF // FF_CHUNK):
        a = conv_chunk(slice(j * FF_CHUNK, (j + 1) * FF_CHUNK), convbuf.at[2 * j])
        b = conv_chunk(slice(D_FF + j * FF_CHUNK, D_FF + (j + 1) * FF_CHUNK), convbuf.at[2 * j + 1])
        hbuf[:, j * FF_CHUNK:(j + 1) * FF_CHUNK] = (a * jax.nn.sigmoid(a) * b).astype(BF16)
    ffn = jnp.dot(hbuf[...], wdown_ref[...], preferred_element_type=F32)
    clast_ref[...] = cbuf[...]

    ple = jax.nn.sigmoid(jnp.dot(x1b, wpg_ref[...], preferred_element_type=F32)) * pe_proj
    y_ref[...] = _layer_norm(ALPHA * x1 + ffn + ple, ln2g_ref[...], ln2b_ref[...])


def _const_spec(shape):
    nd = len(shape)
    return pl.BlockSpec(shape, lambda *_: (0,) * nd)


def _params(semantics):
    return pltpu.CompilerParams(dimension_semantics=semantics, vmem_limit_bytes=VMEM_LIMIT)


def _pad_heads(w, per_head, used):
    k = w.shape[0]
    w = w.reshape(k, N_HEADS, per_head)[:, :, :used]
    return jnp.pad(w, ((0, 0), (0, 0), (0, HEAD_PAD - used))).reshape(k, N_HEADS * HEAD_PAD)


def _prep_weights(w_in, w_q_b, w_kv_b, w_pool, w_o, w_up, w_down, w_pg, w_pe):
    w_kr = jnp.pad(w_in[:, 512:512 + QK_ROPE], ((0, 0), (ROPE_LO, LANES - ROPE_HI)))
    win = jnp.concatenate([w_in[:, :512], w_kr, w_in[:, 512 + QK_ROPE:]], axis=1).astype(BF16)
    wq_pad = _pad_heads(w_q_b, QK_NOPE + QK_ROPE, QK_NOPE + QK_ROPE).astype(BF16)
    w_kv = w_kv_b.reshape(KV_LORA, N_HEADS, QK_NOPE + V_DIM)
    w_uk, w_uv = w_kv[..., :QK_NOPE], w_kv[..., QK_NOPE:]
    wuk_pad = jnp.pad(w_uk, ((0, 0), (0, 0), (0, HEAD_PAD - QK_NOPE))).reshape(KV_LORA, -1).astype(BF16)
    wuvt = w_uv.reshape(KV_LORA, N_HEADS * V_DIM).T.astype(BF16)
    wukt = jnp.transpose(w_uk, (1, 2, 0)).astype(BF16)
    eye = jnp.eye(N_HEADS, dtype=w_uv.dtype)
    wuv_bd = jnp.einsum('lhv,hg->hlgv', w_uv, eye).reshape(N_HEADS * KV_LORA, N_HEADS * V_DIM).astype(BF16)
    return dict(win=win, wq_pad=wq_pad, wqt=wq_pad.T, wuk_pad=wuk_pad, wuvt=wuvt, wukt=wukt, wuv_bd=wuv_bd,
                wpool=w_pool.astype(BF16), wo=w_o.astype(BF16), wup=w_up.astype(BF16),
                wdown=w_down.astype(BF16), wpg=w_pg.astype(BF16), wpe=w_pe.astype(BF16))


def _rope_inv():
    inv = 1.0 / (ROPE_THETA ** (jnp.arange(0, QK_ROPE, 2, dtype=F32) / QK_ROPE))
    inv_lane = jnp.zeros((1, LANES), F32).at[0, ROPE_LO:ROPE_HI].set(jnp.concatenate([inv, inv]))
    return inv_lane, inv[:, None]


def _project_prompt(x, wb, g_q, g_kv, tm, tk):
    s = x.shape[0]
    _, inv_sub = _rope_inv()
    n = s // tm
    outs = pl.pallas_call(
        functools.partial(_proj_prompt_kernel, tm=tm, tk=tk, pos0=0),
        grid=(n,),
        in_specs=[pl.BlockSpec((tm, D_MODEL), lambda i: (i, 0)),
                  _const_spec(wb['win'].shape), _const_spec((1, Q_LORA)), _const_spec((1, KV_LORA)),
                  _const_spec(wb['wqt'].shape), _const_spec(wb['wuk_pad'].shape), _const_spec(wb['wuvt'].shape),
                  _const_spec((QK_ROPE // 2, 1))],
        out_specs=[pl.BlockSpec((N_HEADS * HEAD_PAD, tm), lambda i: (0, i)),
                   pl.BlockSpec((tm, N_HEADS * HEAD_PAD), lambda i: (i, 0)),
                   pl.BlockSpec((tm // tk, N_HEADS * V_ROWS, tk), lambda i: (i, 0, 0)),
                   pl.BlockSpec((tm, KV_LORA), lambda i: (i, 0)),
                   pl.BlockSpec((QK_ROPE, tm), lambda i: (0, i)),
                   pl.BlockSpec((tm, POOL_WIDTH), lambda i: (i, 0))],
        out_shape=[jax.ShapeDtypeStruct((N_HEADS * HEAD_PAD, s), BF16),
                   jax.ShapeDtypeStruct((s, N_HEADS * HEAD_PAD), BF16),
                   jax.ShapeDtypeStruct((s // tk, N_HEADS * V_ROWS, tk), BF16),
                   jax.ShapeDtypeStruct((s, KV_LORA), F32),
                   jax.ShapeDtypeStruct((QK_ROPE, s), F32),
                   jax.ShapeDtypeStruct((s, POOL_WIDTH), F32)],
        compiler_params=_params(("arbitrary",)),
        name="proj_prompt",
    )(x, wb['win'], g_q.reshape(1, -1), g_kv.reshape(1, -1), wb['wqt'], wb['wuk_pad'], wb['wuvt'], inv_sub)
    return outs


def _project_sample(x, wb, g_q, g_kv, seq_len, pos0):
    rows = x.shape[0]
    assert seq_len & (seq_len - 1) == 0
    inv_lane, _ = _rope_inv()
    return pl.pallas_call(
        functools.partial(_proj_sample_kernel, tm=rows, seq_len=seq_len, pos0=pos0),
        grid=(1,),
        in_specs=[_const_spec((rows, D_MODEL)), _const_spec(wb['win'].shape), _const_spec((1, Q_LORA)),
                  _const_spec((1, KV_LORA)), _const_spec(wb['wq_pad'].shape), _const_spec((1, LANES))],
        out_specs=[_const_spec((rows, N_HEADS * HEAD_PAD)), _const_spec((rows, KV_LORA)),
                   _const_spec((rows, QK_ROPE)), _const_spec((rows, POOL_WIDTH))],
        out_shape=[jax.ShapeDtypeStruct((rows, N_HEADS * HEAD_PAD), F32),
                   jax.ShapeDtypeStruct((rows, KV_LORA), F32),
                   jax.ShapeDtypeStruct((rows, QK_ROPE), F32),
                   jax.ShapeDtypeStruct((rows, POOL_WIDTH), F32)],
        compiler_params=_params(("arbitrary",)),
        name="proj_sample",
    )(x, wb['win'], g_q.reshape(1, -1), g_kv.reshape(1, -1), wb['wq_pad'], inv_lane)


def _attend_prompt(qt, k, vt, tq, tk, hg):
    s = k.shape[0]
    nkv = s // tk
    k3 = k.reshape(nkv, tk, N_HEADS * HEAD_PAD)
    resident = pl.Buffered(1)
    assert tq == tk
    nq = s // tq
    return pl.pallas_call(
        functools.partial(_attn_prompt_kernel, tq=tq, hg=hg, nq=nq),
        grid=(N_HEADS // hg, nq),
        in_specs=[pl.BlockSpec((hg * HEAD_PAD, tq), lambda g, i: (g, i)),
                  pl.BlockSpec((hg * HEAD_PAD, tq), lambda g, i: (g, jnp.minimum(i + 1, nq - 1))),
                  pl.BlockSpec((nkv, tk, hg * HEAD_PAD), lambda g, i: (0, 0, g), pipeline_mode=resident),
                  pl.BlockSpec((nkv, hg * V_ROWS, tk), lambda g, i: (0, g, 0), pipeline_mode=resident)],
        out_specs=pl.BlockSpec((hg * V_DIM, tq), lambda g, i: (g, i)),
        out_shape=jax.ShapeDtypeStruct((MLA_WIDTH, s), F32),
        scratch_shapes=[pltpu.VMEM((hg, tk, tq), F32), pltpu.VMEM((hg, tk, tq), F32),
                        pltpu.VMEM((hg, 1, tq), F32), pltpu.VMEM((hg, V_ROWS, tq), F32)],
        compiler_params=_params(("arbitrary", "arbitrary")),
        name="attn_prompt",
    )(qt, qt, k3, vt)


def _attend_sample(q, c_new, kr_new, c_hist, kr_hist, wb, t):
    nb, n_past, _ = c_hist.shape
    assert t & (t - 1) == 0
    return pl.pallas_call(
        functools.partial(_attn_sample_kernel, t=t, n_past=n_past),
        grid=(nb,),
        in_specs=[pl.BlockSpec((t, N_HEADS * HEAD_PAD), lambda b: (b, 0)),
                  pl.BlockSpec((t, KV_LORA), lambda b: (b, 0)),
                  pl.BlockSpec((t, QK_ROPE), lambda b: (b, 0)),
                  pl.BlockSpec((None, n_past, KV_LORA), lambda b: (b, 0, 0)),
                  pl.BlockSpec((None, QK_ROPE, n_past), lambda b: (b, 0, 0)),
                  _const_spec(wb['wukt'].shape), _const_spec(wb['wuv_bd'].shape)],
        out_specs=pl.BlockSpec((t, MLA_WIDTH), lambda b: (b, 0)),
        out_shape=jax.ShapeDtypeStruct((nb * t, MLA_WIDTH), F32),
        compiler_params=_params(("arbitrary",)),
        name="attn_sample",
    )(q, c_new, kr_new, c_hist, jnp.swapaxes(kr_hist, 1, 2), wb['wukt'], wb['wuv_bd'])


def _post(x, o, u, uprev, uprev_map, pe, chist, wb, small, *, nseq, seq_len, carry, o_transposed, pos0):
    rows = x.shape[0]
    tm = nseq * seq_len
    n = rows // tm
    assert nseq == 1 if carry else seq_len & (seq_len - 1) == 0
    s_pool, ln1_g, ln1_b, w_dw, b_dw, ln2_g, ln2_b = small
    o_spec = (pl.BlockSpec((MLA_WIDTH, tm), lambda i: (0, i)) if o_transposed
              else pl.BlockSpec((tm, MLA_WIDTH), lambda i: (i, 0)))
    per_tile = (lambda i: (0, 0, 0)) if carry else (lambda i: (i, 0, 0))
    row = lambda v: v.reshape(1, -1)
    return pl.pallas_call(
        functools.partial(_post_kernel, nseq=nseq, seq_len=seq_len, carry=carry, o_transposed=o_transposed,
                          pos0=pos0),
        grid=(n,),
        in_specs=[pl.BlockSpec((tm, D_MODEL), lambda i: (i, 0)),
                  o_spec,
                  pl.BlockSpec((tm, POOL_WIDTH), lambda i: (i, 0)),
                  pl.BlockSpec((nseq, HALO, POOL_WIDTH), uprev_map),
                  pl.BlockSpec((tm, PLE_DIM), lambda i: (i, 0)),
                  pl.BlockSpec((nseq, CONV_W - 1, 2 * D_FF), per_tile),
                  _const_spec(wb['wpool'].shape), _const_spec((1, POOL_WIDTH)), _const_spec(wb['wo'].shape),
                  _const_spec((1, D_MODEL)), _const_spec((1, D_MODEL)),
                  _const_spec(wb['wup'].shape), _const_spec((CONV_W, 2 * D_FF)), _const_spec((1, 2 * D_FF)),
                  _const_spec(wb['wdown'].shape), _const_spec(wb['wpg'].shape), _const_spec(wb['wpe'].shape),
                  _const_spec((1, D_MODEL)), _const_spec((1, D_MODEL))],
        out_specs=[pl.BlockSpec((tm, D_MODEL), lambda i: (i, 0)),
                   pl.BlockSpec((nseq, CONV_HALO, 2 * D_FF), per_tile)],
        out_shape=[jax.ShapeDtypeStruct((rows, D_MODEL), F32),
                   jax.ShapeDtypeStruct((nseq if carry else n * nseq, CONV_HALO, 2 * D_FF), F32)],
        scratch_shapes=[pltpu.VMEM((nseq * (seq_len + HALO), POOL_WIDTH), F32),
                        pltpu.VMEM((2 * (D_FF // FF_CHUNK), nseq * (seq_len + CONV_HALO), FF_CHUNK), F32),
                        pltpu.VMEM((tm, D_FF), BF16),
                        pltpu.VMEM((nseq, CONV_HALO, 2 * D_FF), F32)],
        compiler_params=_params(("arbitrary",)),
        name="post_prompt" if carry else "post_sample",
    )(x, o, u, uprev, pe, chist,
      wb['wpool'], row(s_pool), wb['wo'], row(ln1_g), row(ln1_b), wb['wup'], w_dw, row(b_dw), wb['wdown'],
      wb['wpg'], wb['wpe'], row(ln2_g), row(ln2_b))


def _layer_prompt(x, pe, wb, g_q, g_kv, small, *, tm_proj, tq, tk, hg, tm_post):
    s = x.shape[0]
    assert tm_proj == tq
    qt, k, vt, c, kr, u = _project_prompt(x, wb, g_q, g_kv, tm_proj, tk)
    ot = _attend_prompt(qt, k, vt, tq, tk, hg)
    per = tm_post // HALO
    y, clast = _post(x, ot, u, u.reshape(s // HALO, HALO, POOL_WIDTH),
                     lambda i: (jnp.maximum(i * per - 1, 0), 0, 0),
                     pe, jnp.zeros((1, CONV_W - 1, 2 * D_FF), F32),
                     wb, small, nseq=1, seq_len=tm_post, carry=True, o_transposed=True, pos0=0)
    return y, c, kr.T, u[s - POOL_HIST:], clast[0, CONV_HALO - (CONV_W - 1):]


def _layer_sample(x, pe, c_hist, kr_hist, pool_hist, conv_hist, wb, g_q, g_kv, small):
    nb, t, _ = x.shape
    n_past = c_hist.shape[1]
    xf = x.reshape(nb * t, D_MODEL)
    q, c, kr, u = _project_sample(xf, wb, g_q, g_kv, t, n_past)
    o = _attend_sample(q, c, kr, c_hist, kr_hist, wb, t)
    uprev = jnp.pad(pool_hist, ((0, 0), (HALO - POOL_HIST, 0), (0, 0)))
    y, clast = _post(xf, o, u, uprev, lambda i: (i, 0, 0), pe.reshape(nb * t, PLE_DIM), conv_hist,
                     wb, small, nseq=nb, seq_len=t, carry=False, o_transposed=False, pos0=n_past)
    u3 = u.reshape(nb, t, POOL_WIDTH)
    new_pool = jnp.concatenate([pool_hist, u3], axis=1)[:, -POOL_HIST:]
    return (y.reshape(nb, t, D_MODEL), c.reshape(nb, t, KV_LORA), kr.reshape(nb, t, QK_ROPE), new_pool,
            clast[:, CONV_HALO - (CONV_W - 1):])


def kernel(x_prompt, x_sample, cache_ckv, cache_krope, state_pool, state_ffn_conv, p_prompt, p_sample,
           w_in, g_q, w_q_b, g_kv, w_kv_b, w_pool, s_pool, w_o, ln1_g, ln1_b,
           w_up, w_dw, b_dw, w_down, w_pg, w_pe, ln2_g, ln2_b):
    assert x_prompt.shape[0] == 1 and w_in.shape[0] == DEPTH
    wb = _prep_weights(w_in[0], w_q_b[0], w_kv_b[0], w_pool[0], w_o[0], w_up[0], w_down[0], w_pg[0], w_pe[0])
    small = (s_pool[0], ln1_g[0], ln1_b[0], w_dw[0], b_dw[0], ln2_g[0], ln2_b[0])
    s = x_prompt.shape[1]
    tm_proj = min(512, s)
    tq = tk = min(512, s)
    yp, cp, krp, poolp, convp = _layer_prompt(x_prompt[0], p_prompt[0, 0], wb, g_q[0], g_kv[0], small,
                                              tm_proj=tm_proj, tq=tq, tk=tk, hg=ATTN_HEAD_GROUP, tm_post=min(256, s))
    ys, cs, krs, pools, convs = _layer_sample(x_sample, p_sample[0], cache_ckv[0], cache_krope[0],
                                              state_pool[0], state_ffn_conv[0], wb, g_q[0], g_kv[0], small)
    return (yp[None], ys, cp[None, None], krp[None, None], poolp[None, None], convp[None, None],
            cs[None], krs[None], pools[None], convs[None])
```

```python
import functools
import math

import jax
import jax.numpy as jnp
from jax import lax
from jax.experimental import pallas as pl
from jax.experimental.pallas import tpu as pltpu

F32 = jnp.float32
BF16 = jnp.bfloat16

D_MODEL = 1024
CHUNK = 64
CHUNK_SHIFT = 6
ATTN_HEAD_GROUP = 4
N_HEADS = 8
QK_NOPE = 64
QK_ROPE = 32
V_DIM = 64
V_ROWS = V_DIM + 16
Q_LORA = 256
KV_LORA = 256
ROPE_THETA = 10000.0
MLA_WIDTH = N_HEADS * V_DIM
POOL_WINDOWS = (2, 4, 8, 16)
POOL_GROUP_W = 128
POOL_WIDTH = D_MODEL - MLA_WIDTH
POOL_HIST = max(POOL_WINDOWS) - 1
D_FF = 2816
CONV_W = 3
PLE_DIM = 256
DEPTH = 1
ALPHA = (2 * DEPTH) ** 0.25
LN_EPS = 1e-5
RMS_EPS = 1e-6
NEG = -1e30
ATTN_SCALE = 1.0 / math.sqrt(QK_NOPE + QK_ROPE)

LANES = 128
HEAD_PAD = 128
ROPE_LO = QK_NOPE
ROPE_MID = QK_NOPE + QK_ROPE // 2
ROPE_HI = QK_NOPE + QK_ROPE
SHIFT_LANE = ROPE_HI
HALO = 16
CONV_HALO = 8
FF_CHUNK = 256
Q_SCALE = ATTN_SCALE * math.log2(math.e)
VMEM_LIMIT = 56 * 1024 * 1024

NT_DIMS = (((1,), (1,)), ((), ()))


def _rms(x, g):
    return x * lax.rsqrt(jnp.mean(x * x, axis=-1, keepdims=True) + RMS_EPS) * g


def _layer_norm(x, g, b):
    mu = jnp.mean(x, axis=-1, keepdims=True)
    xc = x - mu
    var = jnp.mean(xc * xc, axis=-1, keepdims=True)
    return xc * lax.rsqrt(var + LN_EPS) * g + b


def _token_rope_tables(pos_col, inv_lane):
    ang = pos_col.astype(F32) * inv_lane
    lane = lax.broadcasted_iota(jnp.int32, ang.shape, 1)
    first = lane < ROPE_MID
    sin = jnp.sin(ang)
    return jnp.cos(ang), jnp.where(first, -sin, sin), first


def _token_rope(x, cos, sin_signed, first):
    partner = jnp.where(first, pltpu.roll(x, LANES - QK_ROPE // 2, 1), pltpu.roll(x, QK_ROPE // 2, 1))
    return x * cos + partner * sin_signed


def _proj_prompt_kernel(x_ref, win_ref, gq_ref, gkv_ref, wqt_ref, wuk_ref, wuvt_ref, invs_ref,
                        qt_ref, k_ref, vt_ref, c_ref, kr_ref, u_ref, *, tm, tk, pos0):
    i = pl.program_id(0)
    z = jnp.dot(x_ref[...].astype(BF16), win_ref[...], preferred_element_type=F32)
    q_a = z[:, 0:Q_LORA]
    c_raw = z[:, Q_LORA:Q_LORA + KV_LORA]
    kr_pad = z[:, 512:640]
    u_ref[...] = z[:, 640:640 + POOL_WIDTH]

    qn = _rms(q_a, gq_ref[...]).astype(BF16)
    c = _rms(c_raw, gkv_ref[...])
    c_ref[...] = c
    cb = c.astype(BF16)

    pos_row = pos0 + i * tm + lax.broadcasted_iota(jnp.int32, (QK_ROPE // 2, tm), 1)
    ang = pos_row.astype(F32) * invs_ref[...]
    cos_t, sin_t = jnp.cos(ang), jnp.sin(ang)
    lo, hi = (ROPE_LO, tm), (HEAD_PAD - ROPE_HI, tm)
    cos = jnp.concatenate([jnp.ones(lo, F32), cos_t, cos_t, jnp.ones(hi, F32)], axis=0).T
    sin_signed = jnp.concatenate([jnp.zeros(lo, F32), -sin_t, sin_t, jnp.zeros(hi, F32)], axis=0).T
    first = lax.broadcasted_iota(jnp.int32, (tm, LANES), 1) < ROPE_MID
    kr_rot = _token_rope(kr_pad, cos, sin_signed, first)
    kr_ref[...] = kr_rot.T[ROPE_LO:ROPE_HI, :]

    k_nope = jnp.dot(cb, wuk_ref[...], preferred_element_type=F32)
    k_tail = kr_rot + jnp.where(lax.broadcasted_iota(jnp.int32, (tm, LANES), 1) == SHIFT_LANE, 1.0, 0.0)
    for h in range(N_HEADS):
        sl = slice(h * HEAD_PAD, (h + 1) * HEAD_PAD)
        k_ref[:, sl] = (k_nope[:, sl] + k_tail).astype(BF16)

    vt = lax.dot_general(wuvt_ref[...], cb, NT_DIMS, preferred_element_type=F32)
    ones = jnp.ones((V_ROWS - V_DIM, tk), BF16)
    for s in range(tm // tk):
        for h in range(N_HEADS):
            vt_ref[s, h * V_ROWS:h * V_ROWS + V_DIM, :] = vt[h * V_DIM:(h + 1) * V_DIM, s * tk:(s + 1) * tk].astype(BF16)
            vt_ref[s, h * V_ROWS + V_DIM:(h + 1) * V_ROWS, :] = ones

    qt = lax.dot_general(wqt_ref[...], qn, NT_DIMS, preferred_element_type=F32)
    shift_row = lax.broadcasted_iota(jnp.int32, (HEAD_PAD - SHIFT_LANE, tm), 0) == 0
    for h in range(N_HEADS):
        b0 = h * HEAD_PAD
        qt_ref[b0:b0 + ROPE_LO, :] = (qt[b0:b0 + ROPE_LO] * Q_SCALE).astype(BF16)
        x1 = qt[b0 + ROPE_LO:b0 + ROPE_MID]
        x2 = qt[b0 + ROPE_MID:b0 + ROPE_HI]
        qt_ref[b0 + ROPE_LO:b0 + ROPE_MID, :] = ((x1 * cos_t - x2 * sin_t) * Q_SCALE).astype(BF16)
        qt_ref[b0 + ROPE_MID:b0 + ROPE_HI, :] = ((x2 * cos_t + x1 * sin_t) * Q_SCALE).astype(BF16)
        qt_ref[b0 + ROPE_HI:b0 + HEAD_PAD, :] = jnp.zeros((HEAD_PAD - ROPE_HI, tm), BF16)
        lead = jnp.dot(k_ref[0:CHUNK, b0:b0 + HEAD_PAD], qt_ref[b0:b0 + HEAD_PAD, :],
                       preferred_element_type=F32)
        shift = jnp.max(lead, axis=0, keepdims=True)
        qt_ref[b0 + SHIFT_LANE:b0 + HEAD_PAD, :] = jnp.where(shift_row, -shift, 0.0).astype(BF16)


def _proj_sample_kernel(x_ref, win_ref, gq_ref, gkv_ref, wq_ref, invl_ref,
                        q_ref, c_ref, kr_ref, u_ref, *, tm, seq_len, pos0):
    z = jnp.dot(x_ref[...].astype(BF16), win_ref[...], preferred_element_type=F32)
    q_a = z[:, 0:Q_LORA]
    c_raw = z[:, Q_LORA:Q_LORA + KV_LORA]
    kr_pad = z[:, 512:640]
    u_ref[...] = z[:, 640:640 + POOL_WIDTH]

    qn = _rms(q_a, gq_ref[...]).astype(BF16)
    c_ref[...] = _rms(c_raw, gkv_ref[...])

    row = lax.broadcasted_iota(jnp.int32, (tm, LANES), 0)
    pos_col = pos0 + (row & (seq_len - 1))
    cos, sin_signed, first = _token_rope_tables(pos_col, invl_ref[...])
    kr_ref[...] = _token_rope(kr_pad, cos, sin_signed, first)[:, ROPE_LO:ROPE_HI]

    q = jnp.dot(qn, wq_ref[...], preferred_element_type=F32)
    for h in range(N_HEADS):
        sl = slice(h * HEAD_PAD, (h + 1) * HEAD_PAD)
        q_ref[:, sl] = _token_rope(q[:, sl], cos, sin_signed, first) * Q_SCALE


def _attn_prompt_kernel(qt_ref, qn_ref, k_ref, vt_ref, o_ref, sa_ref, sb_ref, m_ref, acc_ref, *, tq, hg, nq):
    i = pl.program_id(1)

    def scores(j, h):
        return jnp.dot(k_ref[j, :, h * HEAD_PAD:(h + 1) * HEAD_PAD], qt_ref[h * HEAD_PAD:(h + 1) * HEAD_PAD, :],
                       preferred_element_type=F32)

    kchunk = lax.broadcasted_iota(jnp.int32, (tq, tq), 0) >> CHUNK_SHIFT
    qchunk = lax.broadcasted_iota(jnp.int32, (tq, tq), 1) >> CHUNK_SHIFT
    visible = kchunk <= qchunk

    def values(jv, h, p):
        return jnp.dot(vt_ref[jv, h * V_ROWS:(h + 1) * V_ROWS, :], p, preferred_element_type=F32)

    def fixed_shift_step(cur_ref, jv, issue_next):
        for h in range(hg):
            p = jnp.exp2(cur_ref[h]).astype(BF16)
            issue_next(h)
            acc_ref[h] += values(jv, h, p)

    def running_max_step(cur_ref, jv, issue_next):
        for h in range(hg):
            s = cur_ref[h]
            m_old = m_ref[h]
            m_new = jnp.maximum(m_old, jnp.max(s, axis=0, keepdims=True))
            alpha = jnp.exp2(m_old - m_new)
            p = jnp.exp2(s - m_new).astype(BF16)
            m_ref[h] = m_new
            issue_next(h)
            acc_ref[h] = alpha * acc_ref[h] + values(jv, h, p)

    def block_into(bank_ref, j):
        def issue(h):
            bank_ref[h] = scores(j, h)
        return issue

    def next_diagonal_into_a(h):
        j = jnp.minimum(i + 1, nq - 1)
        s = jnp.dot(k_ref[j, :, h * HEAD_PAD:(h + 1) * HEAD_PAD], qn_ref[h * HEAD_PAD:(h + 1) * HEAD_PAD, :],
                    preferred_element_type=F32)
        sa_ref[h] = jnp.where(visible, s, NEG)

    def sweep(step, pairs_per_trip):
        def pair(t):
            step(sa_ref, jnp.where(t == 0, i, 2 * t - 1), block_into(sb_ref, 2 * t))
            step(sb_ref, 2 * t, block_into(sa_ref, 2 * t + 1))

        n_pairs = i // 2
        n_trips = n_pairs // pairs_per_trip

        def trip(u, c):
            for r in range(pairs_per_trip):
                pair(pairs_per_trip * u + r)
            return c

        lax.fori_loop(0, n_trips, trip, 0)
        for r in range(pairs_per_trip - 1):
            pl.when(n_pairs - n_trips * pairs_per_trip > r)(
                functools.partial(pair, n_trips * pairs_per_trip + r))

        @pl.when(i % 2 == 0)
        def _():
            step(sa_ref, jnp.maximum(i - 1, 0), next_diagonal_into_a)

        @pl.when(i % 2 == 1)
        def _():
            step(sa_ref, jnp.where(i == 1, i, i - 2), block_into(sb_ref, i - 1))
            step(sb_ref, i - 1, next_diagonal_into_a)

    def load_diagonal():
        for h in range(hg):
            sa_ref[h] = jnp.where(visible, scores(i, h), NEG)

    acc_ref[...] = jnp.zeros(acc_ref.shape, F32)
    pl.when(i == 0)(load_diagonal)
    sweep(fixed_shift_step, pairs_per_trip=2)

    bad = jnp.where(jnp.isfinite(acc_ref[...]), 0.0, 1.0)
    bad = jnp.max(jnp.max(jnp.max(bad, axis=0), axis=0, keepdims=True), axis=1, keepdims=True)

    @pl.when(bad[0, 0] > 0.0)
    def _():
        m_ref[...] = jnp.full(m_ref.shape, NEG, F32)
        acc_ref[...] = jnp.zeros(acc_ref.shape, F32)
        load_diagonal()
        sweep(running_max_step, pairs_per_trip=1)

    for h in range(hg):
        o_ref[h * V_DIM:(h + 1) * V_DIM, :] = acc_ref[h, 0:V_DIM, :] / acc_ref[h, V_DIM:V_DIM + 1, :]


def _attn_sample_kernel(q_ref, cn_ref, krn_ref, ch_ref, krht_ref, wukt_ref, wuvbd_ref, o_ref, *, t, n_past):
    q = q_ref[...]
    qlat, qrope = [], []
    for h in range(N_HEADS):
        b0 = h * HEAD_PAD
        qn = q[:, b0:b0 + QK_NOPE].astype(BF16)
        qlat.append(jnp.dot(qn, wukt_ref[h], preferred_element_type=F32))
        qrope.append(q[:, b0 + ROPE_LO:b0 + ROPE_HI])
    ql = jnp.concatenate(qlat, axis=0).astype(BF16)
    qr = jnp.concatenate(qrope, axis=0).astype(BF16)

    chb = ch_ref[...].astype(BF16)
    krhb = krht_ref[...].astype(BF16)
    cnb = cn_ref[...].astype(BF16)
    krnb = krn_ref[...].astype(BF16)
    s_h = (lax.dot_general(ql, chb, NT_DIMS, preferred_element_type=F32)
           + jnp.dot(qr, krhb, preferred_element_type=F32))
    s_n = (lax.dot_general(ql, cnb, NT_DIMS, preferred_element_type=F32)
           + lax.dot_general(qr, krnb, NT_DIMS, preferred_element_type=F32))

    rows = N_HEADS * t
    qchunk = (n_past + (lax.broadcasted_iota(jnp.int32, (rows, 1), 0) & (t - 1))) >> CHUNK_SHIFT
    s_h = jnp.where((lax.broadcasted_iota(jnp.int32, (rows, n_past), 1) >> CHUNK_SHIFT) <= qchunk, s_h, NEG)
    s_n = jnp.where(((n_past + lax.broadcasted_iota(jnp.int32, (rows, t), 1)) >> CHUNK_SHIFT) <= qchunk, s_n, NEG)

    m = jnp.maximum(jnp.max(s_h, axis=-1, keepdims=True), jnp.max(s_n, axis=-1, keepdims=True))
    p_h = jnp.exp2(s_h - m)
    p_n = jnp.exp2(s_n - m)
    l = jnp.sum(p_h, axis=-1, keepdims=True) + jnp.sum(p_n, axis=-1, keepdims=True)
    olat = (jnp.dot(p_h.astype(BF16), chb, preferred_element_type=F32)
            + jnp.dot(p_n.astype(BF16), cnb, preferred_element_type=F32)) / l
    wide = jnp.concatenate([olat[h * t:(h + 1) * t] for h in range(N_HEADS)], axis=1).astype(BF16)
    o_ref[...] = jnp.dot(wide, wuvbd_ref[...], preferred_element_type=F32)


def _post_kernel(x_ref, o_ref, u_ref, uprev_ref, pe_ref, chist_ref,
                 wpool_ref, spool_ref, wo_ref, ln1g_ref, ln1b_ref, wup_ref, wdw_ref, bdw_ref, wdown_ref,
                 wpg_ref, wpe_ref, ln2g_ref, ln2b_ref,
                 y_ref, clast_ref,
                 ubuf, convbuf, hbuf, cbuf, *, nseq, seq_len, carry, o_transposed, pos0):
    i = pl.program_id(0)
    tm = nseq * seq_len
    hist_rows = slice(CONV_HALO - (CONV_W - 1), CONV_HALO)

    def init_conv_history():
        cbuf[...] = jnp.zeros_like(cbuf)
        cbuf[:, hist_rows, :] = chist_ref[...]

    if carry:
        pl.when(i == 0)(init_conv_history)
    else:
        init_conv_history()

    def gather(buf, halo, shift, cols):
        stride = halo + seq_len
        pieces = [buf[q * stride + halo - shift:q * stride + halo - shift + seq_len, cols] for q in range(nseq)]
        return pieces[0] if nseq == 1 else jnp.concatenate(pieces, axis=0)

    attn = o_ref[...].T if o_transposed else o_ref[...]
    mix_attn = jnp.dot(attn.astype(BF16), wo_ref[0:MLA_WIDTH, :], preferred_element_type=F32)
    pe_proj = jnp.dot(pe_ref[...].astype(BF16), wpe_ref[...], preferred_element_type=F32)

    u = u_ref[...]
    for q in range(nseq):
        uprev = uprev_ref[q]
        if carry:
            uprev = jnp.where(i > 0, uprev, 0.0)
        ubuf[q * (HALO + seq_len):q * (HALO + seq_len) + HALO, :] = uprev
        ubuf[q * (HALO + seq_len) + HALO:(q + 1) * (HALO + seq_len), :] = u[q * seq_len:(q + 1) * seq_len]
    row = lax.broadcasted_iota(jnp.int32, (tm, POOL_GROUP_W), 0)
    frame = row + i * tm if carry else row & (seq_len - 1)
    pooled = []
    for g, w in enumerate(POOL_WINDOWS):
        cols = slice(g * POOL_GROUP_W, (g + 1) * POOL_GROUP_W)
        win = u[:, cols]
        for k in range(1, w):
            win = win + gather(ubuf, HALO, k, cols)
        cnt = jnp.minimum(w, pos0 + frame + 1).astype(F32)
        d = win / cnt - u[:, cols]
        yg = jnp.dot(d.astype(BF16), wpool_ref[g], preferred_element_type=F32) * spool_ref[:, cols]
        pooled.append(yg.astype(BF16))
    pooled = jnp.concatenate(pooled, axis=1)

    mix = mix_attn + jnp.dot(pooled, wo_ref[MLA_WIDTH:D_MODEL, :], preferred_element_type=F32)
    x1 = _layer_norm(ALPHA * x_ref[...] + mix, ln1g_ref[...], ln1b_ref[...])
    x1b = x1.astype(BF16)

    def conv_chunk(cols, buf):
        up = jnp.dot(x1b, wup_ref[:, cols], preferred_element_type=F32)
        if nseq == 1:
            hist = cbuf[0, :, cols]
            cbuf[0, :, cols] = up[tm - CONV_HALO:tm]
            top = lax.broadcasted_iota(jnp.int32, (CONV_HALO, FF_CHUNK), 0)

            def shifted(k):
                rolled = pltpu.roll(up, k, 0)
                head = jnp.where(top < k, pltpu.roll(hist, k, 0), rolled[0:CONV_HALO])
                return jnp.concatenate([head, rolled[CONV_HALO:]], axis=0)

            return (shifted(2) * wdw_ref[0:1, cols] + shifted(1) * wdw_ref[1:2, cols]
                    + up * wdw_ref[2:3, cols] + bdw_ref[:, cols])
        stride = CONV_HALO + seq_len
        for q in range(nseq):
            buf[q * stride:q * stride + CONV_HALO, :] = cbuf[q, :, cols]
            buf[q * stride + CONV_HALO:(q + 1) * stride, :] = up[q * seq_len:(q + 1) * seq_len]
            cbuf[q, :, cols] = buf[q * stride + seq_len:(q + 1) * stride, :]
        everything = slice(None)
        return (gather(buf, CONV_HALO, 2, everything) * wdw_ref[0:1, cols]
                + gather(buf, CONV_HALO, 1, everything) * wdw_ref[1:2, cols]
                + up * wdw_ref[2:3, cols] + bdw_ref[:, cols])

    for j in range(D_FF // FF_CHUNK):
        a = conv_chunk(slice(j * FF_CHUNK, (j + 1) * FF_CHUNK), convbuf.at[2 * j])
        b = conv_chunk(slice(D_FF + j * FF_CHUNK, D_FF + (j + 1) * FF_CHUNK), convbuf.at[2 * j + 1])
        hbuf[:, j * FF_CHUNK:(j + 1) * FF_CHUNK] = (a * jax.nn.sigmoid(a) * b).astype(BF16)
    ffn = jnp.dot(hbuf[...], wdown_ref[...], preferred_element_type=F32)
    clast_ref[...] = cbuf[...]

    ple = jax.nn.sigmoid(jnp.dot(x1b, wpg_ref[...], preferred_element_type=F32)) * pe_proj
    y_ref[...] = _layer_norm(ALPHA * x1 + ffn + ple, ln2g_ref[...], ln2b_ref[...])


def _const_spec(shape):
    nd = len(shape)
    return pl.BlockSpec(shape, lambda *_: (0,) * nd)


def _params(semantics):
    return pltpu.CompilerParams(dimension_semantics=semantics, vmem_limit_bytes=VMEM_LIMIT)


def _pad_heads(w, per_head, used):
    k = w.shape[0]
    w = w.reshape(k, N_HEADS, per_head)[:, :, :used]
    return jnp.pad(w, ((0, 0), (0, 0), (0, HEAD_PAD - used))).reshape(k, N_HEADS * HEAD_PAD)


def _prep_weights(w_in, w_q_b, w_kv_b, w_pool, w_o, w_up, w_down, w_pg, w_pe):
    w_kr = jnp.pad(w_in[:, 512:512 + QK_ROPE], ((0, 0), (ROPE_LO, LANES - ROPE_HI)))
    win = jnp.concatenate([w_in[:, :512], w_kr, w_in[:, 512 + QK_ROPE:]], axis=1).astype(BF16)
    wq_pad = _pad_heads(w_q_b, QK_NOPE + QK_ROPE, QK_NOPE + QK_ROPE).astype(BF16)
    w_kv = w_kv_b.reshape(KV_LORA, N_HEADS, QK_NOPE + V_DIM)
    w_uk, w_uv = w_kv[..., :QK_NOPE], w_kv[..., QK_NOPE:]
    wuk_pad = jnp.pad(w_uk, ((0, 0), (0, 0), (0, HEAD_PAD - QK_NOPE))).reshape(KV_LORA, -1).astype(BF16)
    wuvt = w_uv.reshape(KV_LORA, N_HEADS * V_DIM).T.astype(BF16)
    wukt = jnp.transpose(w_uk, (1, 2, 0)).astype(BF16)
    eye = jnp.eye(N_HEADS, dtype=w_uv.dtype)
    wuv_bd = jnp.einsum('lhv,hg->hlgv', w_uv, eye).reshape(N_HEADS * KV_LORA, N_HEADS * V_DIM).astype(BF16)
    return dict(win=win, wq_pad=wq_pad, wqt=wq_pad.T, wuk_pad=wuk_pad, wuvt=wuvt, wukt=wukt, wuv_bd=wuv_bd,
                wpool=w_pool.astype(BF16), wo=w_o.astype(BF16), wup=w_up.astype(BF16),
                wdown=w_down.astype(BF16), wpg=w_pg.astype(BF16), wpe=w_pe.astype(BF16))


def _rope_inv():
    inv = 1.0 / (ROPE_THETA ** (jnp.arange(0, QK_ROPE, 2, dtype=F32) / QK_ROPE))
    inv_lane = jnp.zeros((1, LANES), F32).at[0, ROPE_LO:ROPE_HI].set(jnp.concatenate([inv, inv]))
    return inv_lane, inv[:, None]


def _project_prompt(x, wb, g_q, g_kv, tm, tk):
    s = x.shape[0]
    _, inv_sub = _rope_inv()
    n = s // tm
    outs = pl.pallas_call(
        functools.partial(_proj_prompt_kernel, tm=tm, tk=tk, pos0=0),
        grid=(n,),
        in_specs=[pl.BlockSpec((tm, D_MODEL), lambda i: (i, 0)),
                  _const_spec(wb['win'].shape), _const_spec((1, Q_LORA)), _const_spec((1, KV_LORA)),
                  _const_spec(wb['wqt'].shape), _const_spec(wb['wuk_pad'].shape), _const_spec(wb['wuvt'].shape),
                  _const_spec((QK_ROPE // 2, 1))],
        out_specs=[pl.BlockSpec((N_HEADS * HEAD_PAD, tm), lambda i: (0, i)),
                   pl.BlockSpec((tm, N_HEADS * HEAD_PAD), lambda i: (i, 0)),
                   pl.BlockSpec((tm // tk, N_HEADS * V_ROWS, tk), lambda i: (i, 0, 0)),
                   pl.BlockSpec((tm, KV_LORA), lambda i: (i, 0)),
                   pl.BlockSpec((QK_ROPE, tm), lambda i: (0, i)),
                   pl.BlockSpec((tm, POOL_WIDTH), lambda i: (i, 0))],
        out_shape=[jax.ShapeDtypeStruct((N_HEADS * HEAD_PAD, s), BF16),
                   jax.ShapeDtypeStruct((s, N_HEADS * HEAD_PAD), BF16),
                   jax.ShapeDtypeStruct((s // tk, N_HEADS * V_ROWS, tk), BF16),
                   jax.ShapeDtypeStruct((s, KV_LORA), F32),
                   jax.ShapeDtypeStruct((QK_ROPE, s), F32),
                   jax.ShapeDtypeStruct((s, POOL_WIDTH), F32)],
        compiler_params=_params(("arbitrary",)),
        name="proj_prompt",
    )(x, wb['win'], g_q.reshape(1, -1), g_kv.reshape(1, -1), wb['wqt'], wb['wuk_pad'], wb['wuvt'], inv_sub)
    return outs


def _project_sample(x, wb, g_q, g_kv, seq_len, pos0):
    rows = x.shape[0]
    assert seq_len & (seq_len - 1) == 0
    inv_lane, _ = _rope_inv()
    return pl.pallas_call(
        functools.partial(_proj_sample_kernel, tm=rows, seq_len=seq_len, pos0=pos0),
        grid=(1,),
        in_specs=[_const_spec((rows, D_MODEL)), _const_spec(wb['win'].shape), _const_spec((1, Q_LORA)),
                  _const_spec((1, KV_LORA)), _const_spec(wb['wq_pad'].shape), _const_spec((1, LANES))],
        out_specs=[_const_spec((rows, N_HEADS * HEAD_PAD)), _const_spec((rows, KV_LORA)),
                   _const_spec((rows, QK_ROPE)), _const_spec((rows, POOL_WIDTH))],
        out_shape=[jax.ShapeDtypeStruct((rows, N_HEADS * HEAD_PAD), F32),
                   jax.ShapeDtypeStruct((rows, KV_LORA), F32),
                   jax.ShapeDtypeStruct((rows, QK_ROPE), F32),
                   jax.ShapeDtypeStruct((rows, POOL_WIDTH), F32)],
        compiler_params=_params(("arbitrary",)),
        name="proj_sample",
    )(x, wb['win'], g_q.reshape(1, -1), g_kv.reshape(1, -1), wb['wq_pad'], inv_lane)


def _attend_prompt(qt, k, vt, tq, tk, hg):
    s = k.shape[0]
    nkv = s // tk
    k3 = k.reshape(nkv, tk, N_HEADS * HEAD_PAD)
    resident = pl.Buffered(1)
    assert tq == tk
    nq = s // tq
    return pl.pallas_call(
        functools.partial(_attn_prompt_kernel, tq=tq, hg=hg, nq=nq),
        grid=(N_HEADS // hg, nq),
        in_specs=[pl.BlockSpec((hg * HEAD_PAD, tq), lambda g, i: (g, i)),
                  pl.BlockSpec((hg * HEAD_PAD, tq), lambda g, i: (g, jnp.minimum(i + 1, nq - 1))),
                  pl.BlockSpec((nkv, tk, hg * HEAD_PAD), lambda g, i: (0, 0, g), pipeline_mode=resident),
                  pl.BlockSpec((nkv, hg * V_ROWS, tk), lambda g, i: (0, g, 0), pipeline_mode=resident)],
        out_specs=pl.BlockSpec((hg * V_DIM, tq), lambda g, i: (g, i)),
        out_shape=jax.ShapeDtypeStruct((MLA_WIDTH, s), F32),
        scratch_shapes=[pltpu.VMEM((hg, tk, tq), F32), pltpu.VMEM((hg, tk, tq), F32),
                        pltpu.VMEM((hg, 1, tq), F32), pltpu.VMEM((hg, V_ROWS, tq), F32)],
        compiler_params=_params(("arbitrary", "arbitrary")),
        name="attn_prompt",
    )(qt, qt, k3, vt)


def _attend_sample(q, c_new, kr_new, c_hist, kr_hist, wb, t):
    nb, n_past, _ = c_hist.shape
    assert t & (t - 1) == 0
    return pl.pallas_call(
        functools.partial(_attn_sample_kernel, t=t, n_past=n_past),
        grid=(nb,),
        in_specs=[pl.BlockSpec((t, N_HEADS * HEAD_PAD), lambda b: (b, 0)),
                  pl.BlockSpec((t, KV_LORA), lambda b: (b, 0)),
                  pl.BlockSpec((t, QK_ROPE), lambda b: (b, 0)),
                  pl.BlockSpec((None, n_past, KV_LORA), lambda b: (b, 0, 0)),
                  pl.BlockSpec((None, QK_ROPE, n_past), lambda b: (b, 0, 0)),
                  _const_spec(wb['wukt'].shape), _const_spec(wb['wuv_bd'].shape)],
        out_specs=pl.BlockSpec((t, MLA_WIDTH), lambda b: (b, 0)),
        out_shape=jax.ShapeDtypeStruct((nb * t, MLA_WIDTH), F32),
        compiler_params=_params(("arbitrary",)),
        name="attn_sample",
    )(q, c_new, kr_new, c_hist, jnp.swapaxes(kr_hist, 1, 2), wb['wukt'], wb['wuv_bd'])


def _post(x, o, u, uprev, uprev_map, pe, chist, wb, small, *, nseq, seq_len, carry, o_transposed, pos0):
    rows = x.shape[0]
    tm = nseq * seq_len
    n = rows // tm
    assert nseq == 1 if carry else seq_len & (seq_len - 1) == 0
    s_pool, ln1_g, ln1_b, w_dw, b_dw, ln2_g, ln2_b = small
    o_spec = (pl.BlockSpec((MLA_WIDTH, tm), lambda i: (0, i)) if o_transposed
              else pl.BlockSpec((tm, MLA_WIDTH), lambda i: (i, 0)))
    per_tile = (lambda i: (0, 0, 0)) if carry else (lambda i: (i, 0, 0))
    row = lambda v: v.reshape(1, -1)
    return pl.pallas_call(
        functools.partial(_post_kernel, nseq=nseq, seq_len=seq_len, carry=carry, o_transposed=o_transposed,
                          pos0=pos0),
        grid=(n,),
        in_specs=[pl.BlockSpec((tm, D_MODEL), lambda i: (i, 0)),
                  o_spec,
                  pl.BlockSpec((tm, POOL_WIDTH), lambda i: (i, 0)),
                  pl.BlockSpec((nseq, HALO, POOL_WIDTH), uprev_map),
                  pl.BlockSpec((tm, PLE_DIM), lambda i: (i, 0)),
                  pl.BlockSpec((nseq, CONV_W - 1, 2 * D_FF), per_tile),
                  _const_spec(wb['wpool'].shape), _const_spec((1, POOL_WIDTH)), _const_spec(wb['wo'].shape),
                  _const_spec((1, D_MODEL)), _const_spec((1, D_MODEL)),
                  _const_spec(wb['wup'].shape), _const_spec((CONV_W, 2 * D_FF)), _const_spec((1, 2 * D_FF)),
                  _const_spec(wb['wdown'].shape), _const_spec(wb['wpg'].shape), _const_spec(wb['wpe'].shape),
                  _const_spec((1, D_MODEL)), _const_spec((1, D_MODEL))],
        out_specs=[pl.BlockSpec((tm, D_MODEL), lambda i: (i, 0)),
                   pl.BlockSpec((nseq, CONV_HALO, 2 * D_FF), per_tile)],
        out_shape=[jax.ShapeDtypeStruct((rows, D_MODEL), F32),
                   jax.ShapeDtypeStruct((nseq if carry else n * nseq, CONV_HALO, 2 * D_FF), F32)],
        scratch_shapes=[pltpu.VMEM((nseq * (seq_len + HALO), POOL_WIDTH), F32),
                        pltpu.VMEM((2 * (D_FF // FF_CHUNK), nseq * (seq_len + CONV_HALO), FF_CHUNK), F32),
                        pltpu.VMEM((tm, D_FF), BF16),
                        pltpu.VMEM((nseq, CONV_HALO, 2 * D_FF), F32)],
        compiler_params=_params(("arbitrary",)),
        name="post_prompt" if carry else "post_sample",
    )(x, o, u, uprev, pe, chist,
      wb['wpool'], row(s_pool), wb['wo'], row(ln1_g), row(ln1_b), wb['wup'], w_dw, row(b_dw), wb['wdown'],
      wb['wpg'], wb['wpe'], row(ln2_g), row(ln2_b))


def _layer_prompt(x, pe, wb, g_q, g_kv, small, *, tm_proj, tq, tk, hg, tm_post):
    s = x.shape[0]
    assert tm_proj == tq
    qt, k, vt, c, kr, u = _project_prompt(x, wb, g_q, g_kv, tm_proj, tk)
    ot = _attend_prompt(qt, k, vt, tq, tk, hg)
    per = tm_post // HALO
    y, clast = _post(x, ot, u, u.reshape(s // HALO, HALO, POOL_WIDTH),
                     lambda i: (jnp.maximum(i * per - 1, 0), 0, 0),
                     pe, jnp.zeros((1, CONV_W - 1, 2 * D_FF), F32),
                     wb, small, nseq=1, seq_len=tm_post, carry=True, o_transposed=True, pos0=0)
    return y, c, kr.T, u[s - POOL_HIST:], clast[0, CONV_HALO - (CONV_W - 1):]


def _layer_sample(x, pe, c_hist, kr_hist, pool_hist, conv_hist, wb, g_q, g_kv, small):
    nb, t, _ = x.shape
    n_past = c_hist.shape[1]
    xf = x.reshape(nb * t, D_MODEL)
    q, c, kr, u = _project_sample(xf, wb, g_q, g_kv, t, n_past)
    o = _attend_sample(q, c, kr, c_hist, kr_hist, wb, t)
    uprev = jnp.pad(pool_hist, ((0, 0), (HALO - POOL_HIST, 0), (0, 0)))
    y, clast = _post(xf, o, u, uprev, lambda i: (i, 0, 0), pe.reshape(nb * t, PLE_DIM), conv_hist,
                     wb, small, nseq=nb, seq_len=t, carry=False, o_transposed=False, pos0=n_past)
    u3 = u.reshape(nb, t, POOL_WIDTH)
    new_pool = jnp.concatenate([pool_hist, u3], axis=1)[:, -POOL_HIST:]
    return (y.reshape(nb, t, D_MODEL), c.reshape(nb, t, KV_LORA), kr.reshape(nb, t, QK_ROPE), new_pool,
            clast[:, CONV_HALO - (CONV_W - 1):])


def kernel(x_prompt, x_sample, cache_ckv, cache_krope, state_pool, state_ffn_conv, p_prompt, p_sample,
           w_in, g_q, w_q_b, g_kv, w_kv_b, w_pool, s_pool, w_o, ln1_g, ln1_b,
           w_up, w_dw, b_dw, w_down, w_pg, w_pe, ln2_g, ln2_b):
    assert x_prompt.shape[0] == 1 and w_in.shape[0] == DEPTH
    wb = _prep_weights(w_in[0], w_q_b[0], w_kv_b[0], w_pool[0], w_o[0], w_up[0], w_down[0], w_pg[0], w_pe[0])
    small = (s_pool[0], ln1_g[0], ln1_b[0], w_dw[0], b_dw[0], ln2_g[0], ln2_b[0])
    s = x_prompt.shape[1]
    tm_proj = min(512, s)
    tq = tk = min(512, s)
    yp, cp, krp, poolp, convp = _layer_prompt(x_prompt[0], p_prompt[0, 0], wb, g_q[0], g_kv[0], small,
                                              tm_proj=tm_proj, tq=tq, tk=tk, hg=ATTN_HEAD_GROUP, tm_post=min(256, s))
    ys, cs, krs, pools, convs = _layer_sample(x_sample, p_sample[0], cache_ckv[0], cache_krope[0],
                                              state_pool[0], state_ffn_conv[0], wb, g_q[0], g_kv[0], small)
    return (yp[None], ys, cp[None, None], krp[None, None], poolp[None, None], convp[None, None],
            cs[None], krs[None], pools[None], convs[None])
```

```python
import functools
import math

import jax
import jax.numpy as jnp
from jax import lax
from jax.experimental import pallas as pl
from jax.experimental.pallas import tpu as pltpu

F32 = jnp.float32
BF16 = jnp.bfloat16

D_MODEL = 1024
CHUNK = 64
CHUNK_SHIFT = 6
ATTN_HEAD_GROUP = 4
N_HEADS = 8
QK_NOPE = 64
QK_ROPE = 32
V_DIM = 64
V_ROWS = V_DIM + 16
Q_LORA = 256
KV_LORA = 256
ROPE_THETA = 10000.0
MLA_WIDTH = N_HEADS * V_DIM
POOL_WINDOWS = (2, 4, 8, 16)
POOL_GROUP_W = 128
POOL_WIDTH = D_MODEL - MLA_WIDTH
POOL_HIST = max(POOL_WINDOWS) - 1
D_FF = 2816
CONV_W = 3
PLE_DIM = 256
DEPTH = 1
ALPHA = (2 * DEPTH) ** 0.25
LN_EPS = 1e-5
RMS_EPS = 1e-6
NEG = -1e30
ATTN_SCALE = 1.0 / math.sqrt(QK_NOPE + QK_ROPE)

LANES = 128
HEAD_PAD = 128
ROPE_LO = QK_NOPE
ROPE_MID = QK_NOPE + QK_ROPE // 2
ROPE_HI = QK_NOPE + QK_ROPE
SHIFT_LANE = ROPE_HI
HALO = 16
CONV_HALO = 8
FF_CHUNK = 256
Q_SCALE = ATTN_SCALE * math.log2(math.e)
VMEM_LIMIT = 56 * 1024 * 1024

NT_DIMS = (((1,), (1,)), ((), ()))


def _rms(x, g):
    return x * lax.rsqrt(jnp.mean(x * x, axis=-1, keepdims=True) + RMS_EPS) * g


def _layer_norm(x, g, b):
    mu = jnp.mean(x, axis=-1, keepdims=True)
    xc = x - mu
    var = jnp.mean(xc * xc, axis=-1, keepdims=True)
    return xc * lax.rsqrt(var + LN_EPS) * g + b


def _token_rope_tables(pos_col, inv_lane):
    ang = pos_col.astype(F32) * inv_lane
    lane = lax.broadcasted_iota(jnp.int32, ang.shape, 1)
    first = lane < ROPE_MID
    sin = jnp.sin(ang)
    return jnp.cos(ang), jnp.where(first, -sin, sin), first


def _token_rope(x, cos, sin_signed, first):
    partner = jnp.where(first, pltpu.roll(x, LANES - QK_ROPE // 2, 1), pltpu.roll(x, QK_ROPE // 2, 1))
    return x * cos + partner * sin_signed


def _proj_prompt_kernel(x_ref, win_ref, gq_ref, gkv_ref, wqt_ref, wuk_ref, wuvt_ref, invs_ref,
                        qt_ref, k_ref, vt_ref, c_ref, kr_ref, u_ref, *, tm, tk, pos0):
    i = pl.program_id(0)
    z = jnp.dot(x_ref[...].astype(BF16), win_ref[...], preferred_element_type=F32)
    q_a = z[:, 0:Q_LORA]
    c_raw = z[:, Q_LORA:Q_LORA + KV_LORA]
    kr_pad = z[:, 512:640]
    u_ref[...] = z[:, 640:640 + POOL_WIDTH]

    qn = _rms(q_a, gq_ref[...]).astype(BF16)
    c = _rms(c_raw, gkv_ref[...])
    c_ref[...] = c
    cb = c.astype(BF16)

    pos_row = pos0 + i * tm + lax.broadcasted_iota(jnp.int32, (QK_ROPE // 2, tm), 1)
    ang = pos_row.astype(F32) * invs_ref[...]
    cos_t, sin_t = jnp.cos(ang), jnp.sin(ang)
    lo, hi = (ROPE_LO, tm), (HEAD_PAD - ROPE_HI, tm)
    cos = jnp.concatenate([jnp.ones(lo, F32), cos_t, cos_t, jnp.ones(hi, F32)], axis=0).T
    sin_signed = jnp.concatenate([jnp.zeros(lo, F32), -sin_t, sin_t, jnp.zeros(hi, F32)], axis=0).T
    first = lax.broadcasted_iota(jnp.int32, (tm, LANES), 1) < ROPE_MID
    kr_rot = _token_rope(kr_pad, cos, sin_signed, first)
    kr_ref[...] = kr_rot.T[ROPE_LO:ROPE_HI, :]

    k_nope = jnp.dot(cb, wuk_ref[...], preferred_element_type=F32)
    k_tail = kr_rot + jnp.where(lax.broadcasted_iota(jnp.int32, (tm, LANES), 1) == SHIFT_LANE, 1.0, 0.0)
    for h in range(N_HEADS):
        sl = slice(h * HEAD_PAD, (h + 1) * HEAD_PAD)
        k_ref[:, sl] = (k_nope[:, sl] + k_tail).astype(BF16)

    vt = lax.dot_general(wuvt_ref[...], cb, NT_DIMS, preferred_element_type=F32)
    ones = jnp.ones((V_ROWS - V_DIM, tk), BF16)
    for s in range(tm // tk):
        for h in range(N_HEADS):
            vt_ref[s, h * V_ROWS:h * V_ROWS + V_DIM, :] = vt[h * V_DIM:(h + 1) * V_DIM, s * tk:(s + 1) * tk].astype(BF16)
            vt_ref[s, h * V_ROWS + V_DIM:(h + 1) * V_ROWS, :] = ones

    qt = lax.dot_general(wqt_ref[...], qn, NT_DIMS, preferred_element_type=F32)
    shift_row = lax.broadcasted_iota(jnp.int32, (HEAD_PAD - SHIFT_LANE, tm), 0) == 0
    for h in range(N_HEADS):
        b0 = h * HEAD_PAD
        qt_ref[b0:b0 + ROPE_LO, :] = (qt[b0:b0 + ROPE_LO] * Q_SCALE).astype(BF16)
        x1 = qt[b0 + ROPE_LO:b0 + ROPE_MID]
        x2 = qt[b0 + ROPE_MID:b0 + ROPE_HI]
        qt_ref[b0 + ROPE_LO:b0 + ROPE_MID, :] = ((x1 * cos_t - x2 * sin_t) * Q_SCALE).astype(BF16)
        qt_ref[b0 + ROPE_MID:b0 + ROPE_HI, :] = ((x2 * cos_t + x1 * sin_t) * Q_SCALE).astype(BF16)
        qt_ref[b0 + ROPE_HI:b0 + HEAD_PAD, :] = jnp.zeros((HEAD_PAD - ROPE_HI, tm), BF16)
        lead = jnp.dot(k_ref[0:CHUNK, b0:b0 + HEAD_PAD], qt_ref[b0:b0 + HEAD_PAD, :],
                       preferred_element_type=F32)
        shift = jnp.max(lead, axis=0, keepdims=True)
        qt_ref[b0 + SHIFT_LANE:b0 + HEAD_PAD, :] = jnp.where(shift_row, -shift, 0.0).astype(BF16)


def _proj_sample_kernel(x_ref, win_ref, gq_ref, gkv_ref, wq_ref, invl_ref,
                        q_ref, c_ref, kr_ref, u_ref, *, tm, seq_len, pos0):
    z = jnp.dot(x_ref[...].astype(BF16), win_ref[...], preferred_element_type=F32)
    q_a = z[:, 0:Q_LORA]
    c_raw = z[:, Q_LORA:Q_LORA + KV_LORA]
    kr_pad = z[:, 512:640]
    u_ref[...] = z[:, 640:640 + POOL_WIDTH]

    qn = _rms(q_a, gq_ref[...]).astype(BF16)
    c_ref[...] = _rms(c_raw, gkv_ref[...])

    row = lax.broadcasted_iota(jnp.int32, (tm, LANES), 0)
    pos_col = pos0 + (row & (seq_len - 1))
    cos, sin_signed, first = _token_rope_tables(pos_col, invl_ref[...])
    kr_ref[...] = _token_rope(kr_pad, cos, sin_signed, first)[:, ROPE_LO:ROPE_HI]

    q = jnp.dot(qn, wq_ref[...], preferred_element_type=F32)
    for h in range(N_HEADS):
        sl = slice(h * HEAD_PAD, (h + 1) * HEAD_PAD)
        q_ref[:, sl] = _token_rope(q[:, sl], cos, sin_signed, first) * Q_SCALE


def _attn_prompt_kernel(qt_ref, qn_ref, k_ref, vt_ref, o_ref, pa_ref, pb_ref, sa_ref, sb_ref, m_ref, acc_ref, *, tq, hg, nq):
    i = pl.program_id(1)

    def scores(j, h):
        return jnp.dot(k_ref[j, :, h * HEAD_PAD:(h + 1) * HEAD_PAD], qt_ref[h * HEAD_PAD:(h + 1) * HEAD_PAD, :],
                       preferred_element_type=F32)

    kchunk = lax.broadcasted_iota(jnp.int32, (tq, tq), 0) >> CHUNK_SHIFT
    qchunk = lax.broadcasted_iota(jnp.int32, (tq, tq), 1) >> CHUNK_SHIFT
    visible = kchunk <= qchunk

    def values(jv, h, p):
        return jnp.dot(vt_ref[jv, h * V_ROWS:(h + 1) * V_ROWS, :], p, preferred_element_type=F32)

    def to_prob(s):
        return jnp.exp2(s).astype(BF16)

    def fixed_shift_step(cur_ref, jv, issue_next):
        for h in range(hg):
            p = cur_ref[h]
            issue_next(h)
            acc_ref[h] += values(jv, h, p)

    def running_max_step(cur_ref, jv, issue_next):
        for h in range(hg):
            s = cur_ref[h]
            m_old = m_ref[h]
            m_new = jnp.maximum(m_old, jnp.max(s, axis=0, keepdims=True))
            alpha = jnp.exp2(m_old - m_new)
            p = jnp.exp2(s - m_new).astype(BF16)
            m_ref[h] = m_new
            issue_next(h)
            acc_ref[h] = alpha * acc_ref[h] + values(jv, h, p)

    def load_diagonal(bank_a, encode):
        for h in range(hg):
            bank_a[h] = encode(jnp.where(visible, scores(i, h), NEG))

    def sweep(step, bank_a, bank_b, encode, pairs_per_trip):
        def block_into(bank_ref, j):
            def issue(h):
                bank_ref[h] = encode(scores(j, h))
            return issue

        def next_diagonal_into_a(h):
            j = jnp.minimum(i + 1, nq - 1)
            s = jnp.dot(k_ref[j, :, h * HEAD_PAD:(h + 1) * HEAD_PAD], qn_ref[h * HEAD_PAD:(h + 1) * HEAD_PAD, :],
                        preferred_element_type=F32)
            bank_a[h] = encode(jnp.where(visible, s, NEG))

        def pair(t):
            step(bank_a, jnp.where(t == 0, i, 2 * t - 1), block_into(bank_b, 2 * t))
            step(bank_b, 2 * t, block_into(bank_a, 2 * t + 1))

        n_pairs = i // 2
        n_trips = n_pairs // pairs_per_trip

        def trip(u, c):
            for r in range(pairs_per_trip):
                pair(pairs_per_trip * u + r)
            return c

        lax.fori_loop(0, n_trips, trip, 0)
        for r in range(pairs_per_trip - 1):
            pl.when(n_pairs - n_trips * pairs_per_trip > r)(
                functools.partial(pair, n_trips * pairs_per_trip + r))

        @pl.when(i % 2 == 0)
        def _():
            step(bank_a, jnp.maximum(i - 1, 0), next_diagonal_into_a)

        @pl.when(i % 2 == 1)
        def _():
            step(bank_a, jnp.where(i == 1, i, i - 2), block_into(bank_b, i - 1))
            step(bank_b, i - 1, next_diagonal_into_a)

    acc_ref[...] = jnp.zeros(acc_ref.shape, F32)
    pl.when(i == 0)(functools.partial(load_diagonal, pa_ref, to_prob))
    sweep(fixed_shift_step, pa_ref, pb_ref, to_prob, pairs_per_trip=2)

    bad = jnp.where(jnp.isfinite(acc_ref[...]), 0.0, 1.0)
    bad = jnp.max(jnp.max(jnp.max(bad, axis=0), axis=0, keepdims=True), axis=1, keepdims=True)

    @pl.when(bad[0, 0] > 0.0)
    def _():
        m_ref[...] = jnp.full(m_ref.shape, NEG, F32)
        acc_ref[...] = jnp.zeros(acc_ref.shape, F32)
        load_diagonal(sa_ref, lambda s: s)
        sweep(running_max_step, sa_ref, sb_ref, lambda s: s, pairs_per_trip=1)
        for h in range(hg):
            pa_ref[h] = to_prob(sa_ref[h])

    for h in range(hg):
        o_ref[h * V_DIM:(h + 1) * V_DIM, :] = acc_ref[h, 0:V_DIM, :] / acc_ref[h, V_DIM:V_DIM + 1, :]


def _attn_sample_kernel(q_ref, cn_ref, krn_ref, ch_ref, krht_ref, wukt_ref, wuvbd_ref, o_ref, *, t, n_past):
    q = q_ref[...]
    qlat, qrope = [], []
    for h in range(N_HEADS):
        b0 = h * HEAD_PAD
        qn = q[:, b0:b0 + QK_NOPE].astype(BF16)
        qlat.append(jnp.dot(qn, wukt_ref[h], preferred_element_type=F32))
        qrope.append(q[:, b0 + ROPE_LO:b0 + ROPE_HI])
    ql = jnp.concatenate(qlat, axis=0).astype(BF16)
    qr = jnp.concatenate(qrope, axis=0).astype(BF16)

    chb = ch_ref[...].astype(BF16)
    krhb = krht_ref[...].astype(BF16)
    cnb = cn_ref[...].astype(BF16)
    krnb = krn_ref[...].astype(BF16)
    s_h = (lax.dot_general(ql, chb, NT_DIMS, preferred_element_type=F32)
           + jnp.dot(qr, krhb, preferred_element_type=F32))
    s_n = (lax.dot_general(ql, cnb, NT_DIMS, preferred_element_type=F32)
           + lax.dot_general(qr, krnb, NT_DIMS, preferred_element_type=F32))

    rows = N_HEADS * t
    qchunk = (n_past + (lax.broadcasted_iota(jnp.int32, (rows, 1), 0) & (t - 1))) >> CHUNK_SHIFT
    s_h = jnp.where((lax.broadcasted_iota(jnp.int32, (rows, n_past), 1) >> CHUNK_SHIFT) <= qchunk, s_h, NEG)
    s_n = jnp.where(((n_past + lax.broadcasted_iota(jnp.int32, (rows, t), 1)) >> CHUNK_SHIFT) <= qchunk, s_n, NEG)

    m = jnp.maximum(jnp.max(s_h, axis=-1, keepdims=True), jnp.max(s_n, axis=-1, keepdims=True))
    p_h = jnp.exp2(s_h - m)
    p_n = jnp.exp2(s_n - m)
    l = jnp.sum(p_h, axis=-1, keepdims=True) + jnp.sum(p_n, axis=-1, keepdims=True)
    olat = (jnp.dot(p_h.astype(BF16), chb, preferred_element_type=F32)
            + jnp.dot(p_n.astype(BF16), cnb, preferred_element_type=F32)) / l
    wide = jnp.concatenate([olat[h * t:(h + 1) * t] for h in range(N_HEADS)], axis=1).astype(BF16)
    o_ref[...] = jnp.dot(wide, wuvbd_ref[...], preferred_element_type=F32)


def _post_kernel(x_ref, o_ref, u_ref, uprev_ref, pe_ref, chist_ref,
                 wpool_ref, spool_ref, wo_ref, ln1g_ref, ln1b_ref, wup_ref, wdw_ref, bdw_ref, wdown_ref,
                 wpg_ref, wpe_ref, ln2g_ref, ln2b_ref,
                 y_ref, clast_ref,
                 ubuf, convbuf, hbuf, cbuf, *, nseq, seq_len, carry, o_transposed, pos0):
    i = pl.program_id(0)
    tm = nseq * seq_len
    hist_rows = slice(CONV_HALO - (CONV_W - 1), CONV_HALO)

    def init_conv_history():
        cbuf[...] = jnp.zeros_like(cbuf)
        cbuf[:, hist_rows, :] = chist_ref[...]

    if carry:
        pl.when(i == 0)(init_conv_history)
    else:
        init_conv_history()

    def gather(buf, halo, shift, cols):
        stride = halo + seq_len
        pieces = [buf[q * stride + halo - shift:q * stride + halo - shift + seq_len, cols] for q in range(nseq)]
        return pieces[0] if nseq == 1 else jnp.concatenate(pieces, axis=0)

    attn = o_ref[...].T if o_transposed else o_ref[...]
    mix_attn = jnp.dot(attn.astype(BF16), wo_ref[0:MLA_WIDTH, :], preferred_element_type=F32)
    pe_proj = jnp.dot(pe_ref[...].astype(BF16), wpe_ref[...], preferred_element_type=F32)

    u = u_ref[...]
    for q in range(nseq):
        uprev = uprev_ref[q]
        if carry:
            uprev = jnp.where(i > 0, uprev, 0.0)
        ubuf[q * (HALO + seq_len):q * (HALO + seq_len) + HALO, :] = uprev
        ubuf[q * (HALO + seq_len) + HALO:(q + 1) * (HALO + seq_len), :] = u[q * seq_len:(q + 1) * seq_len]
    row = lax.broadcasted_iota(jnp.int32, (tm, POOL_GROUP_W), 0)
    frame = row + i * tm if carry else row & (seq_len - 1)
    pooled = []
    for g, w in enumerate(POOL_WINDOWS):
        cols = slice(g * POOL_GROUP_W, (g + 1) * POOL_GROUP_W)
        win = u[:, cols]
        for k in range(1, w):
            win = win + gather(ubuf, HALO, k, cols)
        cnt = jnp.minimum(w, pos0 + frame + 1).astype(F32)
        d = win / cnt - u[:, cols]
        yg = jnp.dot(d.astype(BF16), wpool_ref[g], preferred_element_type=F32) * spool_ref[:, cols]
        pooled.append(yg.astype(BF16))
    pooled = jnp.concatenate(pooled, axis=1)

    mix = mix_attn + jnp.dot(pooled, wo_ref[MLA_WIDTH:D_MODEL, :], preferred_element_type=F32)
    x1 = _layer_norm(ALPHA * x_ref[...] + mix, ln1g_ref[...], ln1b_ref[...])
    x1b = x1.astype(BF16)

    def conv_chunk(cols, buf):
        up = jnp.dot(x1b, wup_ref[:, cols], preferred_element_type=F32)
        if nseq == 1:
            hist = cbuf[0, :, cols]
            cbuf[0, :, cols] = up[tm - CONV_HALO:tm]
            top = lax.broadcasted_iota(jnp.int32, (CONV_HALO, FF_CHUNK), 0)

            def shifted(k):
                rolled = pltpu.roll(up, k, 0)
                head = jnp.where(top < k, pltpu.roll(hist, k, 0), rolled[0:CONV_HALO])
                return jnp.concatenate([head, rolled[CONV_HALO:]], axis=0)

            return (shifted(2) * wdw_ref[0:1, cols] + shifted(1) * wdw_ref[1:2, cols]
                    + up * wdw_ref[2:3, cols] + bdw_ref[:, cols])
        stride = CONV_HALO + seq_len
        for q in range(nseq):
            buf[q * stride:q * stride + CONV_HALO, :] = cbuf[q, :, cols]
            buf[q * stride + CONV_HALO:(q + 1) * stride, :] = up[q * seq_len:(q + 1) * seq_len]
            cbuf[q, :, cols] = buf[q * stride + seq_len:(q + 1) * stride, :]
        everything = slice(None)
        return (gather(buf, CONV_HALO, 2, everything) * wdw_ref[0:1, cols]
                + gather(buf, CONV_HALO, 1, everything) * wdw_ref[1:2, cols]
                + up * wdw_ref[2:3, cols] + bdw_ref[:, cols])

    for j in range(D_FF // FF_CHUNK):
        a = conv_chunk(slice(j * FF_CHUNK, (j + 1) * FF_CHUNK), convbuf.at[2 * j])
        b = conv_chunk(slice(D_FF + j * FF_CHUNK, D_FF + (j + 1) * FF_CHUNK), convbuf.at[2 * j + 1])
        hbuf[:, j * FF_CHUNK:(j + 1) * FF_CHUNK] = (a * jax.nn.sigmoid(a) * b).astype(BF16)
    ffn = jnp.dot(hbuf[...], wdown_ref[...], preferred_element_type=F32)
    clast_ref[...] = cbuf[...]

    ple = jax.nn.sigmoid(jnp.dot(x1b, wpg_ref[...], preferred_element_type=F32)) * pe_proj
    y_ref[...] = _layer_norm(ALPHA * x1 + ffn + ple, ln2g_ref[...], ln2b_ref[...])


def _const_spec(shape):
    nd = len(shape)
    return pl.BlockSpec(shape, lambda *_: (0,) * nd)


def _params(semantics):
    return pltpu.CompilerParams(dimension_semantics=semantics, vmem_limit_bytes=VMEM_LIMIT)


def _pad_heads(w, per_head, used):
    k = w.shape[0]
    w = w.reshape(k, N_HEADS, per_head)[:, :, :used]
    return jnp.pad(w, ((0, 0), (0, 0), (0, HEAD_PAD - used))).reshape(k, N_HEADS * HEAD_PAD)


def _prep_weights(w_in, w_q_b, w_kv_b, w_pool, w_o, w_up, w_down, w_pg, w_pe):
    w_kr = jnp.pad(w_in[:, 512:512 + QK_ROPE], ((0, 0), (ROPE_LO, LANES - ROPE_HI)))
    win = jnp.concatenate([w_in[:, :512], w_kr, w_in[:, 512 + QK_ROPE:]], axis=1).astype(BF16)
    wq_pad = _pad_heads(w_q_b, QK_NOPE + QK_ROPE, QK_NOPE + QK_ROPE).astype(BF16)
    w_kv = w_kv_b.reshape(KV_LORA, N_HEADS, QK_NOPE + V_DIM)
    w_uk, w_uv = w_kv[..., :QK_NOPE], w_kv[..., QK_NOPE:]
    wuk_pad = jnp.pad(w_uk, ((0, 0), (0, 0), (0, HEAD_PAD - QK_NOPE))).reshape(KV_LORA, -1).astype(BF16)
    wuvt = w_uv.reshape(KV_LORA, N_HEADS * V_DIM).T.astype(BF16)
    wukt = jnp.transpose(w_uk, (1, 2, 0)).astype(BF16)
    eye = jnp.eye(N_HEADS, dtype=w_uv.dtype)
    wuv_bd = jnp.einsum('lhv,hg->hlgv', w_uv, eye).reshape(N_HEADS * KV_LORA, N_HEADS * V_DIM).astype(BF16)
    return dict(win=win, wq_pad=wq_pad, wqt=wq_pad.T, wuk_pad=wuk_pad, wuvt=wuvt, wukt=wukt, wuv_bd=wuv_bd,
                wpool=w_pool.astype(BF16), wo=w_o.astype(BF16), wup=w_up.astype(BF16),
                wdown=w_down.astype(BF16), wpg=w_pg.astype(BF16), wpe=w_pe.astype(BF16))


def _rope_inv():
    inv = 1.0 / (ROPE_THETA ** (jnp.arange(0, QK_ROPE, 2, dtype=F32) / QK_ROPE))
    inv_lane = jnp.zeros((1, LANES), F32).at[0, ROPE_LO:ROPE_HI].set(jnp.concatenate([inv, inv]))
    return inv_lane, inv[:, None]


def _project_prompt(x, wb, g_q, g_kv, tm, tk):
    s = x.shape[0]
    _, inv_sub = _rope_inv()
    n = s // tm
    outs = pl.pallas_call(
        functools.partial(_proj_prompt_kernel, tm=tm, tk=tk, pos0=0),
        grid=(n,),
        in_specs=[pl.BlockSpec((tm, D_MODEL), lambda i: (i, 0)),
                  _const_spec(wb['win'].shape), _const_spec((1, Q_LORA)), _const_spec((1, KV_LORA)),
                  _const_spec(wb['wqt'].shape), _const_spec(wb['wuk_pad'].shape), _const_spec(wb['wuvt'].shape),
                  _const_spec((QK_ROPE // 2, 1))],
        out_specs=[pl.BlockSpec((N_HEADS * HEAD_PAD, tm), lambda i: (0, i)),
                   pl.BlockSpec((tm, N_HEADS * HEAD_PAD), lambda i: (i, 0)),
                   pl.BlockSpec((tm // tk, N_HEADS * V_ROWS, tk), lambda i: (i, 0, 0)),
                   pl.BlockSpec((tm, KV_LORA), lambda i: (i, 0)),
                   pl.BlockSpec((QK_ROPE, tm), lambda i: (0, i)),
                   pl.BlockSpec((tm, POOL_WIDTH), lambda i: (i, 0))],
        out_shape=[jax.ShapeDtypeStruct((N_HEADS * HEAD_PAD, s), BF16),
                   jax.ShapeDtypeStruct((s, N_HEADS * HEAD_PAD), BF16),
                   jax.ShapeDtypeStruct((s // tk, N_HEADS * V_ROWS, tk), BF16),
                   jax.ShapeDtypeStruct((s, KV_LORA), F32),
                   jax.ShapeDtypeStruct((QK_ROPE, s), F32),
                   jax.ShapeDtypeStruct((s, POOL_WIDTH), F32)],
        compiler_params=_params(("arbitrary",)),
        name="proj_prompt",
    )(x, wb['win'], g_q.reshape(1, -1), g_kv.reshape(1, -1), wb['wqt'], wb['wuk_pad'], wb['wuvt'], inv_sub)
    return outs


def _project_sample(x, wb, g_q, g_kv, seq_len, pos0):
    rows = x.shape[0]
    assert seq_len & (seq_len - 1) == 0
    inv_lane, _ = _rope_inv()
    return pl.pallas_call(
        functools.partial(_proj_sample_kernel, tm=rows, seq_len=seq_len, pos0=pos0),
        grid=(1,),
        in_specs=[_const_spec((rows, D_MODEL)), _const_spec(wb['win'].shape), _const_spec((1, Q_LORA)),
                  _const_spec((1, KV_LORA)), _const_spec(wb['wq_pad'].shape), _const_spec((1, LANES))],
        out_specs=[_const_spec((rows, N_HEADS * HEAD_PAD)), _const_spec((rows, KV_LORA)),
                   _const_spec((rows, QK_ROPE)), _const_spec((rows, POOL_WIDTH))],
        out_shape=[jax.ShapeDtypeStruct((rows, N_HEADS * HEAD_PAD), F32),
                   jax.ShapeDtypeStruct((rows, KV_LORA), F32),
                   jax.ShapeDtypeStruct((rows, QK_ROPE), F32),
                   jax.ShapeDtypeStruct((rows, POOL_WIDTH), F32)],
        compiler_params=_params(("arbitrary",)),
        name="proj_sample",
    )(x, wb['win'], g_q.reshape(1, -1), g_kv.reshape(1, -1), wb['wq_pad'], inv_lane)


def _attend_prompt(qt, k, vt, tq, tk, hg):
    s = k.shape[0]
    nkv = s // tk
    k3 = k.reshape(nkv, tk, N_HEADS * HEAD_PAD)
    resident = pl.Buffered(1)
    assert tq == tk
    nq = s // tq
    return pl.pallas_call(
        functools.partial(_attn_prompt_kernel, tq=tq, hg=hg, nq=nq),
        grid=(N_HEADS // hg, nq),
        in_specs=[pl.BlockSpec((hg * HEAD_PAD, tq), lambda g, i: (g, i)),
                  pl.BlockSpec((hg * HEAD_PAD, tq), lambda g, i: (g, jnp.minimum(i + 1, nq - 1))),
                  pl.BlockSpec((nkv, tk, hg * HEAD_PAD), lambda g, i: (0, 0, g), pipeline_mode=resident),
                  pl.BlockSpec((nkv, hg * V_ROWS, tk), lambda g, i: (0, g, 0), pipeline_mode=resident)],
        out_specs=pl.BlockSpec((hg * V_DIM, tq), lambda g, i: (g, i)),
        out_shape=jax.ShapeDtypeStruct((MLA_WIDTH, s), F32),
        scratch_shapes=[pltpu.VMEM((hg, tk, tq), BF16), pltpu.VMEM((hg, tk, tq), BF16),
                        pltpu.VMEM((hg, tk, tq), F32), pltpu.VMEM((hg, tk, tq), F32),
                        pltpu.VMEM((hg, 1, tq), F32), pltpu.VMEM((hg, V_ROWS, tq), F32)],
        compiler_params=_params(("arbitrary", "arbitrary")),
        name="attn_prompt",
    )(qt, qt, k3, vt)


def _attend_sample(q, c_new, kr_new, c_hist, kr_hist, wb, t):
    nb, n_past, _ = c_hist.shape
    assert t & (t - 1) == 0
    return pl.pallas_call(
        functools.partial(_attn_sample_kernel, t=t, n_past=n_past),
        grid=(nb,),
        in_specs=[pl.BlockSpec((t, N_HEADS * HEAD_PAD), lambda b: (b, 0)),
                  pl.BlockSpec((t, KV_LORA), lambda b: (b, 0)),
                  pl.BlockSpec((t, QK_ROPE), lambda b: (b, 0)),
                  pl.BlockSpec((None, n_past, KV_LORA), lambda b: (b, 0, 0)),
                  pl.BlockSpec((None, QK_ROPE, n_past), lambda b: (b, 0, 0)),
                  _const_spec(wb['wukt'].shape), _const_spec(wb['wuv_bd'].shape)],
        out_specs=pl.BlockSpec((t, MLA_WIDTH), lambda b: (b, 0)),
        out_shape=jax.ShapeDtypeStruct((nb * t, MLA_WIDTH), F32),
        compiler_params=_params(("arbitrary",)),
        name="attn_sample",
    )(q, c_new, kr_new, c_hist, jnp.swapaxes(kr_hist, 1, 2), wb['wukt'], wb['wuv_bd'])


def _post(x, o, u, uprev, uprev_map, pe, chist, wb, small, *, nseq, seq_len, carry, o_transposed, pos0):
    rows = x.shape[0]
    tm = nseq * seq_len
    n = rows // tm
    assert nseq == 1 if carry else seq_len & (seq_len - 1) == 0
    s_pool, ln1_g, ln1_b, w_dw, b_dw, ln2_g, ln2_b = small
    o_spec = (pl.BlockSpec((MLA_WIDTH, tm), lambda i: (0, i)) if o_transposed
              else pl.BlockSpec((tm, MLA_WIDTH), lambda i: (i, 0)))
    per_tile = (lambda i: (0, 0, 0)) if carry else (lambda i: (i, 0, 0))
    row = lambda v: v.reshape(1, -1)
    return pl.pallas_call(
        functools.partial(_post_kernel, nseq=nseq, seq_len=seq_len, carry=carry, o_transposed=o_transposed,
                          pos0=pos0),
        grid=(n,),
        in_specs=[pl.BlockSpec((tm, D_MODEL), lambda i: (i, 0)),
                  o_spec,
                  pl.BlockSpec((tm, POOL_WIDTH), lambda i: (i, 0)),
                  pl.BlockSpec((nseq, HALO, POOL_WIDTH), uprev_map),
                  pl.BlockSpec((tm, PLE_DIM), lambda i: (i, 0)),
                  pl.BlockSpec((nseq, CONV_W - 1, 2 * D_FF), per_tile),
                  _const_spec(wb['wpool'].shape), _const_spec((1, POOL_WIDTH)), _const_spec(wb['wo'].shape),
                  _const_spec((1, D_MODEL)), _const_spec((1, D_MODEL)),
                  _const_spec(wb['wup'].shape), _const_spec((CONV_W, 2 * D_FF)), _const_spec((1, 2 * D_FF)),
                  _const_spec(wb['wdown'].shape), _const_spec(wb['wpg'].shape), _const_spec(wb['wpe'].shape),
                  _const_spec((1, D_MODEL)), _const_spec((1, D_MODEL))],
        out_specs=[pl.BlockSpec((tm, D_MODEL), lambda i: (i, 0)),
                   pl.BlockSpec((nseq, CONV_HALO, 2 * D_FF), per_tile)],
        out_shape=[jax.ShapeDtypeStruct((rows, D_MODEL), F32),
                   jax.ShapeDtypeStruct((nseq if carry else n * nseq, CONV_HALO, 2 * D_FF), F32)],
        scratch_shapes=[pltpu.VMEM((nseq * (seq_len + HALO), POOL_WIDTH), F32),
                        pltpu.VMEM((2 * (D_FF // FF_CHUNK), nseq * (seq_len + CONV_HALO), FF_CHUNK), F32),
                        pltpu.VMEM((tm, D_FF), BF16),
                        pltpu.VMEM((nseq, CONV_HALO, 2 * D_FF), F32)],
        compiler_params=_params(("arbitrary",)),
        name="post_prompt" if carry else "post_sample",
    )(x, o, u, uprev, pe, chist,
      wb['wpool'], row(s_pool), wb['wo'], row(ln1_g), row(ln1_b), wb['wup'], w_dw, row(b_dw), wb['wdown'],
      wb['wpg'], wb['wpe'], row(ln2_g), row(ln2_b))


def _layer_prompt(x, pe, wb, g_q, g_kv, small, *, tm_proj, tq, tk, hg, tm_post):
    s = x.shape[0]
    assert tm_proj == tq
    qt, k, vt, c, kr, u = _project_prompt(x, wb, g_q, g_kv, tm_proj, tk)
    ot = _attend_prompt(qt, k, vt, tq, tk, hg)
    per = tm_post // HALO
    y, clast = _post(x, ot, u, u.reshape(s // HALO, HALO, POOL_WIDTH),
                     lambda i: (jnp.maximum(i * per - 1, 0), 0, 0),
                     pe, jnp.zeros((1, CONV_W - 1, 2 * D_FF), F32),
                     wb, small, nseq=1, seq_len=tm_post, carry=True, o_transposed=True, pos0=0)
    return y, c, kr.T, u[s - POOL_HIST:], clast[0, CONV_HALO - (CONV_W - 1):]


def _layer_sample(x, pe, c_hist, kr_hist, pool_hist, conv_hist, wb, g_q, g_kv, small):
    nb, t, _ = x.shape
    n_past = c_hist.shape[1]
    xf = x.reshape(nb * t, D_MODEL)
    q, c, kr, u = _project_sample(xf, wb, g_q, g_kv, t, n_past)
    o = _attend_sample(q, c, kr, c_hist, kr_hist, wb, t)
    uprev = jnp.pad(pool_hist, ((0, 0), (HALO - POOL_HIST, 0), (0, 0)))
    y, clast = _post(xf, o, u, uprev, lambda i: (i, 0, 0), pe.reshape(nb * t, PLE_DIM), conv_hist,
                     wb, small, nseq=nb, seq_len=t, carry=False, o_transposed=False, pos0=n_past)
    u3 = u.reshape(nb, t, POOL_WIDTH)
    new_pool = jnp.concatenate([pool_hist, u3], axis=1)[:, -POOL_HIST:]
    return (y.reshape(nb, t, D_MODEL), c.reshape(nb, t, KV_LORA), kr.reshape(nb, t, QK_ROPE), new_pool,
            clast[:, CONV_HALO - (CONV_W - 1):])


def kernel(x_prompt, x_sample, cache_ckv, cache_krope, state_pool, state_ffn_conv, p_prompt, p_sample,
           w_in, g_q, w_q_b, g_kv, w_kv_b, w_pool, s_pool, w_o, ln1_g, ln1_b,
           w_up, w_dw, b_dw, w_down, w_pg, w_pe, ln2_g, ln2_b):
    assert x_prompt.shape[0] == 1 and w_in.shape[0] == DEPTH
    wb = _prep_weights(w_in[0], w_q_b[0], w_kv_b[0], w_pool[0], w_o[0], w_up[0], w_down[0], w_pg[0], w_pe[0])
    small = (s_pool[0], ln1_g[0], ln1_b[0], w_dw[0], b_dw[0], ln2_g[0], ln2_b[0])
    s = x_prompt.shape[1]
    tm_proj = min(512, s)
    tq = tk = min(512, s)
    yp, cp, krp, poolp, convp = _layer_prompt(x_prompt[0], p_prompt[0, 0], wb, g_q[0], g_kv[0], small,
                                              tm_proj=tm_proj, tq=tq, tk=tk, hg=ATTN_HEAD_GROUP, tm_post=min(256, s))
    ys, cs, krs, pools, convs = _layer_sample(x_sample, p_sample[0], cache_ckv[0], cache_krope[0],
                                              state_pool[0], state_ffn_conv[0], wb, g_q[0], g_kv[0], small)
    return (yp[None], ys, cp[None, None], krp[None, None], poolp[None, None], convp[None, None],
            cs[None], krs[None], pools[None], convs[None])
```

```python
import functools
import math

import jax
import jax.numpy as jnp
from jax import lax
from jax.experimental import pallas as pl
from jax.experimental.pallas import tpu as pltpu

F32 = jnp.float32
BF16 = jnp.bfloat16

D_MODEL = 1024
CHUNK = 64
CHUNK_SHIFT = 6
ATTN_HEAD_GROUP = 4
SAMPLE_ATTN_GROUP = 2
N_HEADS = 8
QK_NOPE = 64
QK_ROPE = 32
V_DIM = 64
V_ROWS = V_DIM + 16
Q_LORA = 256
KV_LORA = 256
ROPE_THETA = 10000.0
MLA_WIDTH = N_HEADS * V_DIM
POOL_WINDOWS = (2, 4, 8, 16)
POOL_GROUP_W = 128
POOL_WIDTH = D_MODEL - MLA_WIDTH
POOL_HIST = max(POOL_WINDOWS) - 1
D_FF = 2816
CONV_W = 3
PLE_DIM = 256
DEPTH = 1
ALPHA = (2 * DEPTH) ** 0.25
LN_EPS = 1e-5
RMS_EPS = 1e-6
NEG = -1e30
ATTN_SCALE = 1.0 / math.sqrt(QK_NOPE + QK_ROPE)

LANES = 128
HEAD_PAD = 128
ROPE_LO = QK_NOPE
ROPE_MID = QK_NOPE + QK_ROPE // 2
ROPE_HI = QK_NOPE + QK_ROPE
SHIFT_LANE = ROPE_HI
HALO = 16
CONV_HALO = 8
FF_CHUNK = 256
Q_SCALE = ATTN_SCALE * math.log2(math.e)
VMEM_LIMIT = 56 * 1024 * 1024

NT_DIMS = (((1,), (1,)), ((), ()))


def _rms(x, g):
    return x * lax.rsqrt(jnp.mean(x * x, axis=-1, keepdims=True) + RMS_EPS) * g


def _layer_norm(x, g, b):
    mu = jnp.mean(x, axis=-1, keepdims=True)
    xc = x - mu
    var = jnp.mean(xc * xc, axis=-1, keepdims=True)
    return xc * lax.rsqrt(var + LN_EPS) * g + b


def _token_rope_tables(pos_col, inv_lane):
    ang = pos_col.astype(F32) * inv_lane
    lane = lax.broadcasted_iota(jnp.int32, ang.shape, 1)
    first = lane < ROPE_MID
    sin = jnp.sin(ang)
    return jnp.cos(ang), jnp.where(first, -sin, sin), first


def _token_rope(x, cos, sin_signed, first):
    partner = jnp.where(first, pltpu.roll(x, LANES - QK_ROPE // 2, 1), pltpu.roll(x, QK_ROPE // 2, 1))
    return x * cos + partner * sin_signed


def _proj_prompt_kernel(x_ref, win_ref, gq_ref, gkv_ref, wqt_ref, wuk_ref, wuvt_ref, invs_ref,
                        qt_ref, k_ref, vt_ref, c_ref, kr_ref, u_ref, *, tm, tk, pos0):
    i = pl.program_id(0)
    z = jnp.dot(x_ref[...].astype(BF16), win_ref[...], preferred_element_type=F32)
    q_a = z[:, 0:Q_LORA]
    c_raw = z[:, Q_LORA:Q_LORA + KV_LORA]
    kr_pad = z[:, 512:640]
    u_ref[...] = z[:, 640:640 + POOL_WIDTH]

    qn = _rms(q_a, gq_ref[...]).astype(BF16)
    c = _rms(c_raw, gkv_ref[...])
    c_ref[...] = c
    cb = c.astype(BF16)

    pos_row = pos0 + i * tm + lax.broadcasted_iota(jnp.int32, (QK_ROPE // 2, tm), 1)
    ang = pos_row.astype(F32) * invs_ref[...]
    cos_t, sin_t = jnp.cos(ang), jnp.sin(ang)
    lo, hi = (ROPE_LO, tm), (HEAD_PAD - ROPE_HI, tm)
    cos = jnp.concatenate([jnp.ones(lo, F32), cos_t, cos_t, jnp.ones(hi, F32)], axis=0).T
    sin_signed = jnp.concatenate([jnp.zeros(lo, F32), -sin_t, sin_t, jnp.zeros(hi, F32)], axis=0).T
    first = lax.broadcasted_iota(jnp.int32, (tm, LANES), 1) < ROPE_MID
    kr_rot = _token_rope(kr_pad, cos, sin_signed, first)
    kr_ref[...] = kr_rot.T[ROPE_LO:ROPE_HI, :]

    k_nope = jnp.dot(cb, wuk_ref[...], preferred_element_type=F32)
    k_tail = kr_rot + jnp.where(lax.broadcasted_iota(jnp.int32, (tm, LANES), 1) == SHIFT_LANE, 1.0, 0.0)
    for h in range(N_HEADS):
        sl = slice(h * HEAD_PAD, (h + 1) * HEAD_PAD)
        k_ref[:, sl] = (k_nope[:, sl] + k_tail).astype(BF16)

    vt = lax.dot_general(wuvt_ref[...], cb, NT_DIMS, preferred_element_type=F32)
    ones = jnp.ones((V_ROWS - V_DIM, tk), BF16)
    for s in range(tm // tk):
        for h in range(N_HEADS):
            vt_ref[s, h * V_ROWS:h * V_ROWS + V_DIM, :] = vt[h * V_DIM:(h + 1) * V_DIM, s * tk:(s + 1) * tk].astype(BF16)
            vt_ref[s, h * V_ROWS + V_DIM:(h + 1) * V_ROWS, :] = ones

    qt = lax.dot_general(wqt_ref[...], qn, NT_DIMS, preferred_element_type=F32)
    shift_row = lax.broadcasted_iota(jnp.int32, (HEAD_PAD - SHIFT_LANE, tm), 0) == 0
    for h in range(N_HEADS):
        b0 = h * HEAD_PAD
        qt_ref[b0:b0 + ROPE_LO, :] = (qt[b0:b0 + ROPE_LO] * Q_SCALE).astype(BF16)
        x1 = qt[b0 + ROPE_LO:b0 + ROPE_MID]
        x2 = qt[b0 + ROPE_MID:b0 + ROPE_HI]
        qt_ref[b0 + ROPE_LO:b0 + ROPE_MID, :] = ((x1 * cos_t - x2 * sin_t) * Q_SCALE).astype(BF16)
        qt_ref[b0 + ROPE_MID:b0 + ROPE_HI, :] = ((x2 * cos_t + x1 * sin_t) * Q_SCALE).astype(BF16)
        qt_ref[b0 + ROPE_HI:b0 + HEAD_PAD, :] = jnp.zeros((HEAD_PAD - ROPE_HI, tm), BF16)
        lead = jnp.dot(k_ref[0:CHUNK, b0:b0 + HEAD_PAD], qt_ref[b0:b0 + HEAD_PAD, :],
                       preferred_element_type=F32)
        shift = jnp.max(lead, axis=0, keepdims=True)
        qt_ref[b0 + SHIFT_LANE:b0 + HEAD_PAD, :] = jnp.where(shift_row, -shift, 0.0).astype(BF16)


def _proj_sample_kernel(x_ref, win_ref, gq_ref, gkv_ref, wq_ref, invl_ref,
                        q_ref, c_ref, kr_ref, u_ref, *, tm, seq_len, pos0):
    z = jnp.dot(x_ref[...].astype(BF16), win_ref[...], preferred_element_type=F32)
    q_a = z[:, 0:Q_LORA]
    c_raw = z[:, Q_LORA:Q_LORA + KV_LORA]
    kr_pad = z[:, 512:640]
    u_ref[...] = z[:, 640:640 + POOL_WIDTH]

    qn = _rms(q_a, gq_ref[...]).astype(BF16)
    c_ref[...] = _rms(c_raw, gkv_ref[...])

    row = lax.broadcasted_iota(jnp.int32, (tm, LANES), 0)
    pos_col = pos0 + (row & (seq_len - 1))
    cos, sin_signed, first = _token_rope_tables(pos_col, invl_ref[...])
    kr_ref[...] = _token_rope(kr_pad, cos, sin_signed, first)[:, ROPE_LO:ROPE_HI]

    q = jnp.dot(qn, wq_ref[...], preferred_element_type=F32)
    for h in range(N_HEADS):
        sl = slice(h * HEAD_PAD, (h + 1) * HEAD_PAD)
        q_ref[:, sl] = _token_rope(q[:, sl], cos, sin_signed, first) * Q_SCALE


def _attn_prompt_kernel(qt_ref, qn_ref, k_ref, vt_ref, o_ref, pa_ref, pb_ref, sa_ref, sb_ref, m_ref, acc_ref, *, tq, hg, nq):
    i = pl.program_id(1)

    def scores(j, h):
        return jnp.dot(k_ref[j, :, h * HEAD_PAD:(h + 1) * HEAD_PAD], qt_ref[h * HEAD_PAD:(h + 1) * HEAD_PAD, :],
                       preferred_element_type=F32)

    kchunk = lax.broadcasted_iota(jnp.int32, (tq, tq), 0) >> CHUNK_SHIFT
    qchunk = lax.broadcasted_iota(jnp.int32, (tq, tq), 1) >> CHUNK_SHIFT
    visible = kchunk <= qchunk

    def values(jv, h, p):
        return jnp.dot(vt_ref[jv, h * V_ROWS:(h + 1) * V_ROWS, :], p, preferred_element_type=F32)

    def to_prob(s):
        return jnp.exp2(s).astype(BF16)

    def fixed_shift_step(cur_ref, jv, issue_next):
        for h in range(hg):
            p = cur_ref[h]
            issue_next(h)
            acc_ref[h] += values(jv, h, p)

    def running_max_step(cur_ref, jv, issue_next):
        for h in range(hg):
            s = cur_ref[h]
            m_old = m_ref[h]
            m_new = jnp.maximum(m_old, jnp.max(s, axis=0, keepdims=True))
            alpha = jnp.exp2(m_old - m_new)
            p = jnp.exp2(s - m_new).astype(BF16)
            m_ref[h] = m_new
            issue_next(h)
            acc_ref[h] = alpha * acc_ref[h] + values(jv, h, p)

    def load_diagonal(bank_a, encode):
        for h in range(hg):
            bank_a[h] = encode(jnp.where(visible, scores(i, h), NEG))

    def sweep(step, bank_a, bank_b, encode, pairs_per_trip):
        def block_into(bank_ref, j):
            def issue(h):
                bank_ref[h] = encode(scores(j, h))
            return issue

        def next_diagonal_into_a(h):
            j = jnp.minimum(i + 1, nq - 1)
            s = jnp.dot(k_ref[j, :, h * HEAD_PAD:(h + 1) * HEAD_PAD], qn_ref[h * HEAD_PAD:(h + 1) * HEAD_PAD, :],
                        preferred_element_type=F32)
            bank_a[h] = encode(jnp.where(visible, s, NEG))

        def pair(t):
            step(bank_a, jnp.where(t == 0, i, 2 * t - 1), block_into(bank_b, 2 * t))
            step(bank_b, 2 * t, block_into(bank_a, 2 * t + 1))

        n_pairs = i // 2
        n_trips = n_pairs // pairs_per_trip

        def trip(u, c):
            for r in range(pairs_per_trip):
                pair(pairs_per_trip * u + r)
            return c

        lax.fori_loop(0, n_trips, trip, 0)
        for r in range(pairs_per_trip - 1):
            pl.when(n_pairs - n_trips * pairs_per_trip > r)(
                functools.partial(pair, n_trips * pairs_per_trip + r))

        @pl.when(i % 2 == 0)
        def _():
            step(bank_a, jnp.maximum(i - 1, 0), next_diagonal_into_a)

        @pl.when(i % 2 == 1)
        def _():
            step(bank_a, jnp.where(i == 1, i, i - 2), block_into(bank_b, i - 1))
            step(bank_b, i - 1, next_diagonal_into_a)

    acc_ref[...] = jnp.zeros(acc_ref.shape, F32)
    pl.when(i == 0)(functools.partial(load_diagonal, pa_ref, to_prob))
    sweep(fixed_shift_step, pa_ref, pb_ref, to_prob, pairs_per_trip=2)

    bad = jnp.where(jnp.isfinite(acc_ref[...]), 0.0, 1.0)
    bad = jnp.max(jnp.max(jnp.max(bad, axis=0), axis=0, keepdims=True), axis=1, keepdims=True)

    @pl.when(bad[0, 0] > 0.0)
    def _():
        m_ref[...] = jnp.full(m_ref.shape, NEG, F32)
        acc_ref[...] = jnp.zeros(acc_ref.shape, F32)
        load_diagonal(sa_ref, lambda s: s)
        sweep(running_max_step, sa_ref, sb_ref, lambda s: s, pairs_per_trip=1)
        for h in range(hg):
            pa_ref[h] = to_prob(sa_ref[h])

    for h in range(hg):
        o_ref[h * V_DIM:(h + 1) * V_DIM, :] = acc_ref[h, 0:V_DIM, :] / acc_ref[h, V_DIM:V_DIM + 1, :]


def _attn_sample_kernel(q_ref, cn_ref, krn_ref, ch_ref, krht_ref, wukt_ref, wuvbd_ref, o_ref, *, t, n_past, nbs):
    rows = N_HEADS * t
    qchunk = (n_past + (lax.broadcasted_iota(jnp.int32, (rows, 1), 0) & (t - 1))) >> CHUNK_SHIFT
    seen_hist = (lax.broadcasted_iota(jnp.int32, (rows, n_past), 1) >> CHUNK_SHIFT) <= qchunk
    seen_new = ((n_past + lax.broadcasted_iota(jnp.int32, (rows, t), 1)) >> CHUNK_SHIFT) <= qchunk
    elements = range(nbs)

    def queries(e):
        q = q_ref[e * t:(e + 1) * t, :]
        qlat, qrope = [], []
        for h in range(N_HEADS):
            b0 = h * HEAD_PAD
            qn = q[:, b0:b0 + QK_NOPE].astype(BF16)
            qlat.append(jnp.dot(qn, wukt_ref[h], preferred_element_type=F32))
            qrope.append(q[:, b0 + ROPE_LO:b0 + ROPE_HI])
        return (jnp.concatenate(qlat, axis=0).astype(BF16),
                jnp.concatenate(qrope, axis=0).astype(BF16))

    def keys(e):
        return (ch_ref[e].astype(BF16),
                krht_ref[e].astype(BF16),
                cn_ref[e * t:(e + 1) * t, :].astype(BF16),
                krn_ref[e * t:(e + 1) * t, :].astype(BF16))

    def scores(qs, ks):
        (ql, qr), (chb, krhb, cnb, krnb) = qs, ks
        s_h = (lax.dot_general(ql, chb, NT_DIMS, preferred_element_type=F32)
               + jnp.dot(qr, krhb, preferred_element_type=F32))
        s_n = (lax.dot_general(ql, cnb, NT_DIMS, preferred_element_type=F32)
               + lax.dot_general(qr, krnb, NT_DIMS, preferred_element_type=F32))
        return jnp.where(seen_hist, s_h, NEG), jnp.where(seen_new, s_n, NEG)

    def softmax(s):
        s_h, s_n = s
        m = jnp.maximum(jnp.max(s_h, axis=-1, keepdims=True), jnp.max(s_n, axis=-1, keepdims=True))
        p_h = jnp.exp2(s_h - m)
        p_n = jnp.exp2(s_n - m)
        l = jnp.sum(p_h, axis=-1, keepdims=True) + jnp.sum(p_n, axis=-1, keepdims=True)
        return p_h.astype(BF16), p_n.astype(BF16), l

    def latent_out(p, ks):
        (p_h, p_n, l), (chb, _, cnb, _) = p, ks
        olat = (jnp.dot(p_h, chb, preferred_element_type=F32) + jnp.dot(p_n, cnb, preferred_element_type=F32)) / l
        return jnp.concatenate([olat[h * t:(h + 1) * t] for h in range(N_HEADS)], axis=1).astype(BF16)

    qs = [queries(e) for e in elements]
    ks = [keys(e) for e in elements]
    ss = [scores(qs[e], ks[e]) for e in elements]
    ps = [softmax(ss[e]) for e in elements]
    wide = jnp.concatenate([latent_out(ps[e], ks[e]) for e in elements], axis=0)
    o_ref[...] = jnp.dot(wide, wuvbd_ref[...], preferred_element_type=F32)


def _post_kernel(x_ref, o_ref, u_ref, uprev_ref, pe_ref, chist_ref,
                 wpool_ref, spool_ref, wo_ref, ln1g_ref, ln1b_ref, wup_ref, wdw_ref, bdw_ref, wdown_ref,
                 wpg_ref, wpe_ref, ln2g_ref, ln2b_ref,
                 y_ref, clast_ref,
                 ubuf, convbuf, hbuf, cbuf, *, nseq, seq_len, carry, o_transposed, pos0):
    i = pl.program_id(0)
    tm = nseq * seq_len
    hist_rows = slice(CONV_HALO - (CONV_W - 1), CONV_HALO)

    def init_conv_history():
        cbuf[...] = jnp.zeros_like(cbuf)
        cbuf[:, hist_rows, :] = chist_ref[...]

    if carry:
        pl.when(i == 0)(init_conv_history)
    else:
        init_conv_history()

    def gather(buf, halo, shift, cols):
        stride = halo + seq_len
        pieces = [buf[q * stride + halo - shift:q * stride + halo - shift + seq_len, cols] for q in range(nseq)]
        return pieces[0] if nseq == 1 else jnp.concatenate(pieces, axis=0)

    attn = o_ref[...].T if o_transposed else o_ref[...]
    mix_attn = jnp.dot(attn.astype(BF16), wo_ref[0:MLA_WIDTH, :], preferred_element_type=F32)
    pe_proj = jnp.dot(pe_ref[...].astype(BF16), wpe_ref[...], preferred_element_type=F32)

    u = u_ref[...]
    for q in range(nseq):
        uprev = uprev_ref[q]
        if carry:
            uprev = jnp.where(i > 0, uprev, 0.0)
        ubuf[q * (HALO + seq_len):q * (HALO + seq_len) + HALO, :] = uprev
        ubuf[q * (HALO + seq_len) + HALO:(q + 1) * (HALO + seq_len), :] = u[q * seq_len:(q + 1) * seq_len]
    row = lax.broadcasted_iota(jnp.int32, (tm, POOL_GROUP_W), 0)
    frame = row + i * tm if carry else row & (seq_len - 1)
    pooled = []
    for g, w in enumerate(POOL_WINDOWS):
        cols = slice(g * POOL_GROUP_W, (g + 1) * POOL_GROUP_W)
        win = u[:, cols]
        for k in range(1, w):
            win = win + gather(ubuf, HALO, k, cols)
        cnt = jnp.minimum(w, pos0 + frame + 1).astype(F32)
        d = win / cnt - u[:, cols]
        yg = jnp.dot(d.astype(BF16), wpool_ref[g], preferred_element_type=F32) * spool_ref[:, cols]
        pooled.append(yg.astype(BF16))
    pooled = jnp.concatenate(pooled, axis=1)

    mix = mix_attn + jnp.dot(pooled, wo_ref[MLA_WIDTH:D_MODEL, :], preferred_element_type=F32)
    x1 = _layer_norm(ALPHA * x_ref[...] + mix, ln1g_ref[...], ln1b_ref[...])
    x1b = x1.astype(BF16)

    def conv_chunk(cols, buf):
        up = jnp.dot(x1b, wup_ref[:, cols], preferred_element_type=F32)
        if nseq == 1:
            hist = cbuf[0, :, cols]
            cbuf[0, :, cols] = up[tm - CONV_HALO:tm]
            top = lax.broadcasted_iota(jnp.int32, (CONV_HALO, FF_CHUNK), 0)

            def shifted(k):
                rolled = pltpu.roll(up, k, 0)
                head = jnp.where(top < k, pltpu.roll(hist, k, 0), rolled[0:CONV_HALO])
                return jnp.concatenate([head, rolled[CONV_HALO:]], axis=0)

            return (shifted(2) * wdw_ref[0:1, cols] + shifted(1) * wdw_ref[1:2, cols]
                    + up * wdw_ref[2:3, cols] + bdw_ref[:, cols])
        stride = CONV_HALO + seq_len
        for q in range(nseq):
            buf[q * stride:q * stride + CONV_HALO, :] = cbuf[q, :, cols]
            buf[q * stride + CONV_HALO:(q + 1) * stride, :] = up[q * seq_len:(q + 1) * seq_len]
            cbuf[q, :, cols] = buf[q * stride + seq_len:(q + 1) * stride, :]
        everything = slice(None)
        return (gather(buf, CONV_HALO, 2, everything) * wdw_ref[0:1, cols]
                + gather(buf, CONV_HALO, 1, everything) * wdw_ref[1:2, cols]
                + up * wdw_ref[2:3, cols] + bdw_ref[:, cols])

    for j in range(D_FF // FF_CHUNK):
        a = conv_chunk(slice(j * FF_CHUNK, (j + 1) * FF_CHUNK), convbuf.at[2 * j])
        b = conv_chunk(slice(D_FF + j * FF_CHUNK, D_FF + (j + 1) * FF_CHUNK), convbuf.at[2 * j + 1])
        hbuf[:, j * FF_CHUNK:(j + 1) * FF_CHUNK] = (a * jax.nn.sigmoid(a) * b).astype(BF16)
    ffn = jnp.dot(hbuf[...], wdown_ref[...], preferred_element_type=F32)
    clast_ref[...] = cbuf[...]

    ple = jax.nn.sigmoid(jnp.dot(x1b, wpg_ref[...], preferred_element_type=F32)) * pe_proj
    y_ref[...] = _layer_norm(ALPHA * x1 + ffn + ple, ln2g_ref[...], ln2b_ref[...])


def _const_spec(shape):
    nd = len(shape)
    return pl.BlockSpec(shape, lambda *_: (0,) * nd)


def _params(semantics):
    return pltpu.CompilerParams(dimension_semantics=semantics, vmem_limit_bytes=VMEM_LIMIT)


def _pad_heads(w, per_head, used):
    k = w.shape[0]
    w = w.reshape(k, N_HEADS, per_head)[:, :, :used]
    return jnp.pad(w, ((0, 0), (0, 0), (0, HEAD_PAD - used))).reshape(k, N_HEADS * HEAD_PAD)


def _prep_weights(w_in, w_q_b, w_kv_b, w_pool, w_o, w_up, w_down, w_pg, w_pe):
    w_kr = jnp.pad(w_in[:, 512:512 + QK_ROPE], ((0, 0), (ROPE_LO, LANES - ROPE_HI)))
    win = jnp.concatenate([w_in[:, :512], w_kr, w_in[:, 512 + QK_ROPE:]], axis=1).astype(BF16)
    wq_pad = _pad_heads(w_q_b, QK_NOPE + QK_ROPE, QK_NOPE + QK_ROPE).astype(BF16)
    w_kv = w_kv_b.reshape(KV_LORA, N_HEADS, QK_NOPE + V_DIM)
    w_uk, w_uv = w_kv[..., :QK_NOPE], w_kv[..., QK_NOPE:]
    wuk_pad = jnp.pad(w_uk, ((0, 0), (0, 0), (0, HEAD_PAD - QK_NOPE))).reshape(KV_LORA, -1).astype(BF16)
    wuvt = w_uv.reshape(KV_LORA, N_HEADS * V_DIM).T.astype(BF16)
    wukt = jnp.transpose(w_uk, (1, 2, 0)).astype(BF16)
    eye = jnp.eye(N_HEADS, dtype=w_uv.dtype)
    wuv_bd = jnp.einsum('lhv,hg->hlgv', w_uv, eye).reshape(N_HEADS * KV_LORA, N_HEADS * V_DIM).astype(BF16)
    return dict(win=win, wq_pad=wq_pad, wqt=wq_pad.T, wuk_pad=wuk_pad, wuvt=wuvt, wukt=wukt, wuv_bd=wuv_bd,
                wpool=w_pool.astype(BF16), wo=w_o.astype(BF16), wup=w_up.astype(BF16),
                wdown=w_down.astype(BF16), wpg=w_pg.astype(BF16), wpe=w_pe.astype(BF16))


def _rope_inv():
    inv = 1.0 / (ROPE_THETA ** (jnp.arange(0, QK_ROPE, 2, dtype=F32) / QK_ROPE))
    inv_lane = jnp.zeros((1, LANES), F32).at[0, ROPE_LO:ROPE_HI].set(jnp.concatenate([inv, inv]))
    return inv_lane, inv[:, None]


def _project_prompt(x, wb, g_q, g_kv, tm, tk):
    s = x.shape[0]
    _, inv_sub = _rope_inv()
    n = s // tm
    outs = pl.pallas_call(
        functools.partial(_proj_prompt_kernel, tm=tm, tk=tk, pos0=0),
        grid=(n,),
        in_specs=[pl.BlockSpec((tm, D_MODEL), lambda i: (i, 0)),
                  _const_spec(wb['win'].shape), _const_spec((1, Q_LORA)), _const_spec((1, KV_LORA)),
                  _const_spec(wb['wqt'].shape), _const_spec(wb['wuk_pad'].shape), _const_spec(wb['wuvt'].shape),
                  _const_spec((QK_ROPE // 2, 1))],
        out_specs=[pl.BlockSpec((N_HEADS * HEAD_PAD, tm), lambda i: (0, i)),
                   pl.BlockSpec((tm, N_HEADS * HEAD_PAD), lambda i: (i, 0)),
                   pl.BlockSpec((tm // tk, N_HEADS * V_ROWS, tk), lambda i: (i, 0, 0)),
                   pl.BlockSpec((tm, KV_LORA), lambda i: (i, 0)),
                   pl.BlockSpec((QK_ROPE, tm), lambda i: (0, i)),
                   pl.BlockSpec((tm, POOL_WIDTH), lambda i: (i, 0))],
        out_shape=[jax.ShapeDtypeStruct((N_HEADS * HEAD_PAD, s), BF16),
                   jax.ShapeDtypeStruct((s, N_HEADS * HEAD_PAD), BF16),
                   jax.ShapeDtypeStruct((s // tk, N_HEADS * V_ROWS, tk), BF16),
                   jax.ShapeDtypeStruct((s, KV_LORA), F32),
                   jax.ShapeDtypeStruct((QK_ROPE, s), F32),
                   jax.ShapeDtypeStruct((s, POOL_WIDTH), F32)],
        compiler_params=_params(("arbitrary",)),
        name="proj_prompt",
    )(x, wb['win'], g_q.reshape(1, -1), g_kv.reshape(1, -1), wb['wqt'], wb['wuk_pad'], wb['wuvt'], inv_sub)
    return outs


def _project_sample(x, wb, g_q, g_kv, seq_len, pos0):
    rows = x.shape[0]
    assert seq_len & (seq_len - 1) == 0
    inv_lane, _ = _rope_inv()
    return pl.pallas_call(
        functools.partial(_proj_sample_kernel, tm=rows, seq_len=seq_len, pos0=pos0),
        grid=(1,),
        in_specs=[_const_spec((rows, D_MODEL)), _const_spec(wb['win'].shape), _const_spec((1, Q_LORA)),
                  _const_spec((1, KV_LORA)), _const_spec(wb['wq_pad'].shape), _const_spec((1, LANES))],
        out_specs=[_const_spec((rows, N_HEADS * HEAD_PAD)), _const_spec((rows, KV_LORA)),
                   _const_spec((rows, QK_ROPE)), _const_spec((rows, POOL_WIDTH))],
        out_shape=[jax.ShapeDtypeStruct((rows, N_HEADS * HEAD_PAD), F32),
                   jax.ShapeDtypeStruct((rows, KV_LORA), F32),
                   jax.ShapeDtypeStruct((rows, QK_ROPE), F32),
                   jax.ShapeDtypeStruct((rows, POOL_WIDTH), F32)],
        compiler_params=_params(("arbitrary",)),
        name="proj_sample",
    )(x, wb['win'], g_q.reshape(1, -1), g_kv.reshape(1, -1), wb['wq_pad'], inv_lane)


def _attend_prompt(qt, k, vt, tq, tk, hg):
    s = k.shape[0]
    nkv = s // tk
    k3 = k.reshape(nkv, tk, N_HEADS * HEAD_PAD)
    resident = pl.Buffered(1)
    assert tq == tk
    nq = s // tq
    return pl.pallas_call(
        functools.partial(_attn_prompt_kernel, tq=tq, hg=hg, nq=nq),
        grid=(N_HEADS // hg, nq),
        in_specs=[pl.BlockSpec((hg * HEAD_PAD, tq), lambda g, i: (g, i)),
                  pl.BlockSpec((hg * HEAD_PAD, tq), lambda g, i: (g, jnp.minimum(i + 1, nq - 1))),
                  pl.BlockSpec((nkv, tk, hg * HEAD_PAD), lambda g, i: (0, 0, g), pipeline_mode=resident),
                  pl.BlockSpec((nkv, hg * V_ROWS, tk), lambda g, i: (0, g, 0), pipeline_mode=resident)],
        out_specs=pl.BlockSpec((hg * V_DIM, tq), lambda g, i: (g, i)),
        out_shape=jax.ShapeDtypeStruct((MLA_WIDTH, s), F32),
        scratch_shapes=[pltpu.VMEM((hg, tk, tq), BF16), pltpu.VMEM((hg, tk, tq), BF16),
                        pltpu.VMEM((hg, tk, tq), F32), pltpu.VMEM((hg, tk, tq), F32),
                        pltpu.VMEM((hg, 1, tq), F32), pltpu.VMEM((hg, V_ROWS, tq), F32)],
        compiler_params=_params(("arbitrary", "arbitrary")),
        name="attn_prompt",
    )(qt, qt, k3, vt)


def _attend_sample(q, c_new, kr_new, c_hist, kr_hist, wb, t):
    nb, n_past, _ = c_hist.shape
    assert t & (t - 1) == 0
    nbs = SAMPLE_ATTN_GROUP if nb % SAMPLE_ATTN_GROUP == 0 else 1
    return pl.pallas_call(
        functools.partial(_attn_sample_kernel, t=t, n_past=n_past, nbs=nbs),
        grid=(nb // nbs,),
        in_specs=[pl.BlockSpec((nbs * t, N_HEADS * HEAD_PAD), lambda b: (b, 0)),
                  pl.BlockSpec((nbs * t, KV_LORA), lambda b: (b, 0)),
                  pl.BlockSpec((nbs * t, QK_ROPE), lambda b: (b, 0)),
                  pl.BlockSpec((nbs, n_past, KV_LORA), lambda b: (b, 0, 0)),
                  pl.BlockSpec((nbs, QK_ROPE, n_past), lambda b: (b, 0, 0)),
                  _const_spec(wb['wukt'].shape), _const_spec(wb['wuv_bd'].shape)],
        out_specs=pl.BlockSpec((nbs * t, MLA_WIDTH), lambda b: (b, 0)),
        out_shape=jax.ShapeDtypeStruct((nb * t, MLA_WIDTH), F32),
        compiler_params=_params(("arbitrary",)),
        name="attn_sample",
    )(q, c_new, kr_new, c_hist, jnp.swapaxes(kr_hist, 1, 2), wb['wukt'], wb['wuv_bd'])


def _post(x, o, u, uprev, uprev_map, pe, chist, wb, small, *, nseq, seq_len, carry, o_transposed, pos0):
    rows = x.shape[0]
    tm = nseq * seq_len
    n = rows // tm
    assert nseq == 1 if carry else seq_len & (seq_len - 1) == 0
    s_pool, ln1_g, ln1_b, w_dw, b_dw, ln2_g, ln2_b = small
    o_spec = (pl.BlockSpec((MLA_WIDTH, tm), lambda i: (0, i)) if o_transposed
              else pl.BlockSpec((tm, MLA_WIDTH), lambda i: (i, 0)))
    per_tile = (lambda i: (0, 0, 0)) if carry else (lambda i: (i, 0, 0))
    row = lambda v: v.reshape(1, -1)
    return pl.pallas_call(
        functools.partial(_post_kernel, nseq=nseq, seq_len=seq_len, carry=carry, o_transposed=o_transposed,
                          pos0=pos0),
        grid=(n,),
        in_specs=[pl.BlockSpec((tm, D_MODEL), lambda i: (i, 0)),
                  o_spec,
                  pl.BlockSpec((tm, POOL_WIDTH), lambda i: (i, 0)),
                  pl.BlockSpec((nseq, HALO, POOL_WIDTH), uprev_map),
                  pl.BlockSpec((tm, PLE_DIM), lambda i: (i, 0)),
                  pl.BlockSpec((nseq, CONV_W - 1, 2 * D_FF), per_tile),
                  _const_spec(wb['wpool'].shape), _const_spec((1, POOL_WIDTH)), _const_spec(wb['wo'].shape),
                  _const_spec((1, D_MODEL)), _const_spec((1, D_MODEL)),
                  _const_spec(wb['wup'].shape), _const_spec((CONV_W, 2 * D_FF)), _const_spec((1, 2 * D_FF)),
                  _const_spec(wb['wdown'].shape), _const_spec(wb['wpg'].shape), _const_spec(wb['wpe'].shape),
                  _const_spec((1, D_MODEL)), _const_spec((1, D_MODEL))],
        out_specs=[pl.BlockSpec((tm, D_MODEL), lambda i: (i, 0)),
                   pl.BlockSpec((nseq, CONV_HALO, 2 * D_FF), per_tile)],
        out_shape=[jax.ShapeDtypeStruct((rows, D_MODEL), F32),
                   jax.ShapeDtypeStruct((nseq if carry else n * nseq, CONV_HALO, 2 * D_FF), F32)],
        scratch_shapes=[pltpu.VMEM((nseq * (seq_len + HALO), POOL_WIDTH), F32),
                        pltpu.VMEM((2 * (D_FF // FF_CHUNK), nseq * (seq_len + CONV_HALO), FF_CHUNK), F32),
                        pltpu.VMEM((tm, D_FF), BF16),
                        pltpu.VMEM((nseq, CONV_HALO, 2 * D_FF), F32)],
        compiler_params=_params(("arbitrary",)),
        name="post_prompt" if carry else "post_sample",
    )(x, o, u, uprev, pe, chist,
      wb['wpool'], row(s_pool), wb['wo'], row(ln1_g), row(ln1_b), wb['wup'], w_dw, row(b_dw), wb['wdown'],
      wb['wpg'], wb['wpe'], row(ln2_g), row(ln2_b))


def _layer_prompt(x, pe, wb, g_q, g_kv, small, *, tm_proj, tq, tk, hg, tm_post):
    s = x.shape[0]
    assert tm_proj == tq
    qt, k, vt, c, kr, u = _project_prompt(x, wb, g_q, g_kv, tm_proj, tk)
    ot = _attend_prompt(qt, k, vt, tq, tk, hg)
    per = tm_post // HALO
    y, clast = _post(x, ot, u, u.reshape(s // HALO, HALO, POOL_WIDTH),
                     lambda i: (jnp.maximum(i * per - 1, 0), 0, 0),
                     pe, jnp.zeros((1, CONV_W - 1, 2 * D_FF), F32),
                     wb, small, nseq=1, seq_len=tm_post, carry=True, o_transposed=True, pos0=0)
    return y, c, kr.T, u[s - POOL_HIST:], clast[0, CONV_HALO - (CONV_W - 1):]


def _layer_sample(x, pe, c_hist, kr_hist, pool_hist, conv_hist, wb, g_q, g_kv, small):
    nb, t, _ = x.shape
    n_past = c_hist.shape[1]
    xf = x.reshape(nb * t, D_MODEL)
    q, c, kr, u = _project_sample(xf, wb, g_q, g_kv, t, n_past)
    o = _attend_sample(q, c, kr, c_hist, kr_hist, wb, t)
    uprev = jnp.pad(pool_hist, ((0, 0), (HALO - POOL_HIST, 0), (0, 0)))
    y, clast = _post(xf, o, u, uprev, lambda i: (i, 0, 0), pe.reshape(nb * t, PLE_DIM), conv_hist,
                     wb, small, nseq=nb, seq_len=t, carry=False, o_transposed=False, pos0=n_past)
    u3 = u.reshape(nb, t, POOL_WIDTH)
    new_pool = jnp.concatenate([pool_hist, u3], axis=1)[:, -POOL_HIST:]
    return (y.reshape(nb, t, D_MODEL), c.reshape(nb, t, KV_LORA), kr.reshape(nb, t, QK_ROPE), new_pool,
            clast[:, CONV_HALO - (CONV_W - 1):])


def kernel(x_prompt, x_sample, cache_ckv, cache_krope, state_pool, state_ffn_conv, p_prompt, p_sample,
           w_in, g_q, w_q_b, g_kv, w_kv_b, w_pool, s_pool, w_o, ln1_g, ln1_b,
           w_up, w_dw, b_dw, w_down, w_pg, w_pe, ln2_g, ln2_b):
    assert x_prompt.shape[0] == 1 and w_in.shape[0] == DEPTH
    wb = _prep_weights(w_in[0], w_q_b[0], w_kv_b[0], w_pool[0], w_o[0], w_up[0], w_down[0], w_pg[0], w_pe[0])
    small = (s_pool[0], ln1_g[0], ln1_b[0], w_dw[0], b_dw[0], ln2_g[0], ln2_b[0])
    s = x_prompt.shape[1]
    tm_proj = min(512, s)
    tq = tk = min(512, s)
    yp, cp, krp, poolp, convp = _layer_prompt(x_prompt[0], p_prompt[0, 0], wb, g_q[0], g_kv[0], small,
                                              tm_proj=tm_proj, tq=tq, tk=tk, hg=ATTN_HEAD_GROUP, tm_post=min(256, s))
    ys, cs, krs, pools, convs = _layer_sample(x_sample, p_sample[0], cache_ckv[0], cache_krope[0],
                                              state_pool[0], state_ffn_conv[0], wb, g_q[0], g_kv[0], small)
    return (yp[None], ys, cp[None, None], krp[None, None], poolp[None, None], convp[None, None],
            cs[None], krs[None], pools[None], convs[None])
```

```python
import functools
import math

import jax
import jax.numpy as jnp
from jax import lax
from jax.experimental import pallas as pl
from jax.experimental.pallas import tpu as pltpu

F32 = jnp.float32
BF16 = jnp.bfloat16

D_MODEL = 1024
CHUNK = 64
CHUNK_SHIFT = 6
ATTN_HEAD_GROUP = 4
SAMPLE_ATTN_GROUP = 2
N_HEADS = 8
QK_NOPE = 64
QK_ROPE = 32
V_DIM = 64
V_ROWS = V_DIM + 16
Q_LORA = 256
KV_LORA = 256
ROPE_THETA = 10000.0
MLA_WIDTH = N_HEADS * V_DIM
POOL_WINDOWS = (2, 4, 8, 16)
POOL_GROUP_W = 128
POOL_WIDTH = D_MODEL - MLA_WIDTH
POOL_HIST = max(POOL_WINDOWS) - 1
D_FF = 2816
CONV_W = 3
PLE_DIM = 256
DEPTH = 1
ALPHA = (2 * DEPTH) ** 0.25
LN_EPS = 1e-5
RMS_EPS = 1e-6
NEG = -1e30
ATTN_SCALE = 1.0 / math.sqrt(QK_NOPE + QK_ROPE)

LANES = 128
HEAD_PAD = 128
ROPE_LO = QK_NOPE
ROPE_MID = QK_NOPE + QK_ROPE // 2
ROPE_HI = QK_NOPE + QK_ROPE
SHIFT_LANE = ROPE_HI
HALO = 16
CONV_HALO = 8
FF_CHUNK = 256
Q_SCALE = ATTN_SCALE * math.log2(math.e)
VMEM_LIMIT = 56 * 1024 * 1024

NT_DIMS = (((1,), (1,)), ((), ()))


def _rms(x, g):
    return x * lax.rsqrt(jnp.mean(x * x, axis=-1, keepdims=True) + RMS_EPS) * g


def _layer_norm(x, g, b):
    mu = jnp.mean(x, axis=-1, keepdims=True)
    xc = x - mu
    var = jnp.mean(xc * xc, axis=-1, keepdims=True)
    return xc * lax.rsqrt(var + LN_EPS) * g + b


def _token_rope_tables(pos_col, inv_lane):
    ang = pos_col.astype(F32) * inv_lane
    lane = lax.broadcasted_iota(jnp.int32, ang.shape, 1)
    first = lane < ROPE_MID
    sin = jnp.sin(ang)
    return jnp.cos(ang), jnp.where(first, -sin, sin), first


def _token_rope(x, cos, sin_signed, first):
    partner = jnp.where(first, pltpu.roll(x, LANES - QK_ROPE // 2, 1), pltpu.roll(x, QK_ROPE // 2, 1))
    return x * cos + partner * sin_signed


def _proj_prompt_kernel(x_ref, win_ref, gq_ref, gkv_ref, wqt_ref, wuk_ref, wuvt_ref, invs_ref,
                        qt_ref, k_ref, vt_ref, c_ref, kr_ref, u_ref, *, tm, tk, pos0):
    i = pl.program_id(0)
    z = jnp.dot(x_ref[...].astype(BF16), win_ref[...], preferred_element_type=F32)
    q_a = z[:, 0:Q_LORA]
    c_raw = z[:, Q_LORA:Q_LORA + KV_LORA]
    kr_pad = z[:, 512:640]
    u_ref[...] = z[:, 640:640 + POOL_WIDTH]

    qn = _rms(q_a, gq_ref[...]).astype(BF16)
    c = _rms(c_raw, gkv_ref[...])
    c_ref[...] = c
    cb = c.astype(BF16)

    pos_row = pos0 + i * tm + lax.broadcasted_iota(jnp.int32, (QK_ROPE // 2, tm), 1)
    ang = pos_row.astype(F32) * invs_ref[...]
    cos_t, sin_t = jnp.cos(ang), jnp.sin(ang)
    lo, hi = (ROPE_LO, tm), (HEAD_PAD - ROPE_HI, tm)
    cos = jnp.concatenate([jnp.ones(lo, F32), cos_t, cos_t, jnp.ones(hi, F32)], axis=0).T
    sin_signed = jnp.concatenate([jnp.zeros(lo, F32), -sin_t, sin_t, jnp.zeros(hi, F32)], axis=0).T
    first = lax.broadcasted_iota(jnp.int32, (tm, LANES), 1) < ROPE_MID
    kr_rot = _token_rope(kr_pad, cos, sin_signed, first)
    kr_ref[...] = kr_rot.T[ROPE_LO:ROPE_HI, :]

    k_nope = jnp.dot(cb, wuk_ref[...], preferred_element_type=F32)
    k_tail = kr_rot + jnp.where(lax.broadcasted_iota(jnp.int32, (tm, LANES), 1) == SHIFT_LANE, 1.0, 0.0)
    for h in range(N_HEADS):
        sl = slice(h * HEAD_PAD, (h + 1) * HEAD_PAD)
        k_ref[:, sl] = (k_nope[:, sl] + k_tail).astype(BF16)

    vt = lax.dot_general(wuvt_ref[...], cb, NT_DIMS, preferred_element_type=F32)
    ones = jnp.ones((V_ROWS - V_DIM, tk), BF16)
    for s in range(tm // tk):
        for h in range(N_HEADS):
            vt_ref[s, h * V_ROWS:h * V_ROWS + V_DIM, :] = vt[h * V_DIM:(h + 1) * V_DIM, s * tk:(s + 1) * tk].astype(BF16)
            vt_ref[s, h * V_ROWS + V_DIM:(h + 1) * V_ROWS, :] = ones

    qt = lax.dot_general(wqt_ref[...], qn, NT_DIMS, preferred_element_type=F32)
    shift_row = lax.broadcasted_iota(jnp.int32, (HEAD_PAD - SHIFT_LANE, tm), 0) == 0
    for h in range(N_HEADS):
        b0 = h * HEAD_PAD
        qt_ref[b0:b0 + ROPE_LO, :] = (qt[b0:b0 + ROPE_LO] * Q_SCALE).astype(BF16)
        x1 = qt[b0 + ROPE_LO:b0 + ROPE_MID]
        x2 = qt[b0 + ROPE_MID:b0 + ROPE_HI]
        qt_ref[b0 + ROPE_LO:b0 + ROPE_MID, :] = ((x1 * cos_t - x2 * sin_t) * Q_SCALE).astype(BF16)
        qt_ref[b0 + ROPE_MID:b0 + ROPE_HI, :] = ((x2 * cos_t + x1 * sin_t) * Q_SCALE).astype(BF16)
        qt_ref[b0 + ROPE_HI:b0 + HEAD_PAD, :] = jnp.zeros((HEAD_PAD - ROPE_HI, tm), BF16)
        lead = jnp.dot(k_ref[0:CHUNK, b0:b0 + HEAD_PAD], qt_ref[b0:b0 + HEAD_PAD, :],
                       preferred_element_type=F32)
        shift = jnp.max(lead, axis=0, keepdims=True)
        qt_ref[b0 + SHIFT_LANE:b0 + HEAD_PAD, :] = jnp.where(shift_row, -shift, 0.0).astype(BF16)


def _proj_sample_kernel(x_ref, win_ref, gq_ref, gkv_ref, wq_ref, invl_ref,
                        q_ref, c_ref, kr_ref, u_ref, *, tm, seq_len, pos0):
    z = jnp.dot(x_ref[...].astype(BF16), win_ref[...], preferred_element_type=F32)
    q_a = z[:, 0:Q_LORA]
    c_raw = z[:, Q_LORA:Q_LORA + KV_LORA]
    kr_pad = z[:, 512:640]
    u_ref[...] = z[:, 640:640 + POOL_WIDTH]

    qn = _rms(q_a, gq_ref[...]).astype(BF16)
    c_ref[...] = _rms(c_raw, gkv_ref[...])

    row = lax.broadcasted_iota(jnp.int32, (tm, LANES), 0)
    pos_col = pos0 + (row & (seq_len - 1))
    cos, sin_signed, first = _token_rope_tables(pos_col, invl_ref[...])
    kr_ref[...] = _token_rope(kr_pad, cos, sin_signed, first)[:, ROPE_LO:ROPE_HI]

    q = jnp.dot(qn, wq_ref[...], preferred_element_type=F32)
    for h in range(N_HEADS):
        sl = slice(h * HEAD_PAD, (h + 1) * HEAD_PAD)
        q_ref[:, sl] = _token_rope(q[:, sl], cos, sin_signed, first) * Q_SCALE


def _attn_prompt_kernel(qt_ref, qn_ref, k_ref, vt_ref, o_ref, pa_ref, pb_ref, sa_ref, sb_ref, m_ref, acc_ref, *, tq, hg, nq):
    i = pl.program_id(1)

    def scores(j, h):
        return jnp.dot(k_ref[j, :, h * HEAD_PAD:(h + 1) * HEAD_PAD], qt_ref[h * HEAD_PAD:(h + 1) * HEAD_PAD, :],
                       preferred_element_type=F32)

    kchunk = lax.broadcasted_iota(jnp.int32, (tq, tq), 0) >> CHUNK_SHIFT
    qchunk = lax.broadcasted_iota(jnp.int32, (tq, tq), 1) >> CHUNK_SHIFT
    visible = kchunk <= qchunk

    def values(jv, h, p):
        return jnp.dot(vt_ref[jv, h * V_ROWS:(h + 1) * V_ROWS, :], p, preferred_element_type=F32)

    def to_prob(s):
        return jnp.exp2(s).astype(BF16)

    def fixed_shift_step(cur_ref, jv, issue_next):
        for h in range(hg):
            p = cur_ref[h]
            issue_next(h)
            acc_ref[h] += values(jv, h, p)

    def running_max_step(cur_ref, jv, issue_next):
        for h in range(hg):
            s = cur_ref[h]
            m_old = m_ref[h]
            m_new = jnp.maximum(m_old, jnp.max(s, axis=0, keepdims=True))
            alpha = jnp.exp2(m_old - m_new)
            p = jnp.exp2(s - m_new).astype(BF16)
            m_ref[h] = m_new
            issue_next(h)
            acc_ref[h] = alpha * acc_ref[h] + values(jv, h, p)

    def load_diagonal(bank_a, encode):
        for h in range(hg):
            bank_a[h] = encode(jnp.where(visible, scores(i, h), NEG))

    def sweep(step, bank_a, bank_b, encode, pairs_per_trip):
        def block_into(bank_ref, j):
            def issue(h):
                bank_ref[h] = encode(scores(j, h))
            return issue

        def next_diagonal_into_a(h):
            j = jnp.minimum(i + 1, nq - 1)
            s = jnp.dot(k_ref[j, :, h * HEAD_PAD:(h + 1) * HEAD_PAD], qn_ref[h * HEAD_PAD:(h + 1) * HEAD_PAD, :],
                        preferred_element_type=F32)
            bank_a[h] = encode(jnp.where(visible, s, NEG))

        def pair(t):
            step(bank_a, jnp.where(t == 0, i, 2 * t - 1), block_into(bank_b, 2 * t))
            step(bank_b, 2 * t, block_into(bank_a, 2 * t + 1))

        n_pairs = i // 2
        n_trips = n_pairs // pairs_per_trip

        def trip(u, c):
            for r in range(pairs_per_trip):
                pair(pairs_per_trip * u + r)
            return c

        lax.fori_loop(0, n_trips, trip, 0)
        for r in range(pairs_per_trip - 1):
            pl.when(n_pairs - n_trips * pairs_per_trip > r)(
                functools.partial(pair, n_trips * pairs_per_trip + r))

        @pl.when(i % 2 == 0)
        def _():
            step(bank_a, jnp.maximum(i - 1, 0), next_diagonal_into_a)

        @pl.when(i % 2 == 1)
        def _():
            step(bank_a, jnp.where(i == 1, i, i - 2), block_into(bank_b, i - 1))
            step(bank_b, i - 1, next_diagonal_into_a)

    acc_ref[...] = jnp.zeros(acc_ref.shape, F32)
    pl.when(i == 0)(functools.partial(load_diagonal, pa_ref, to_prob))
    sweep(fixed_shift_step, pa_ref, pb_ref, to_prob, pairs_per_trip=2)

    bad = jnp.where(jnp.isfinite(acc_ref[...]), 0.0, 1.0)
    bad = jnp.max(jnp.max(jnp.max(bad, axis=0), axis=0, keepdims=True), axis=1, keepdims=True)

    @pl.when(bad[0, 0] > 0.0)
    def _():
        m_ref[...] = jnp.full(m_ref.shape, NEG, F32)
        acc_ref[...] = jnp.zeros(acc_ref.shape, F32)
        load_diagonal(sa_ref, lambda s: s)
        sweep(running_max_step, sa_ref, sb_ref, lambda s: s, pairs_per_trip=1)
        for h in range(hg):
            pa_ref[h] = to_prob(sa_ref[h])

    for h in range(hg):
        o_ref[h * V_DIM:(h + 1) * V_DIM, :] = acc_ref[h, 0:V_DIM, :] / acc_ref[h, V_DIM:V_DIM + 1, :]


def _attn_sample_kernel(q_ref, cn_ref, krn_ref, ch_ref, krht_ref, wukt_ref, wuvbd_ref, o_ref, *, t, n_past, nbs):
    rows = N_HEADS * t
    qchunk = (n_past + (lax.broadcasted_iota(jnp.int32, (rows, 1), 0) & (t - 1))) >> CHUNK_SHIFT
    seen_hist = (lax.broadcasted_iota(jnp.int32, (rows, n_past), 1) >> CHUNK_SHIFT) <= qchunk
    seen_new = ((n_past + lax.broadcasted_iota(jnp.int32, (rows, t), 1)) >> CHUNK_SHIFT) <= qchunk
    elements = range(nbs)

    def queries(e):
        q = q_ref[e * t:(e + 1) * t, :]
        qlat, qrope = [], []
        for h in range(N_HEADS):
            b0 = h * HEAD_PAD
            qn = q[:, b0:b0 + QK_NOPE].astype(BF16)
            qlat.append(jnp.dot(qn, wukt_ref[h], preferred_element_type=F32))
            qrope.append(q[:, b0 + ROPE_LO:b0 + ROPE_HI])
        return (jnp.concatenate(qlat, axis=0).astype(BF16),
                jnp.concatenate(qrope, axis=0).astype(BF16))

    def keys(e):
        return (ch_ref[e].astype(BF16),
                krht_ref[e].astype(BF16),
                cn_ref[e * t:(e + 1) * t, :].astype(BF16),
                krn_ref[e * t:(e + 1) * t, :].astype(BF16))

    def scores(qs, ks):
        (ql, qr), (chb, krhb, cnb, krnb) = qs, ks
        s_h = (lax.dot_general(ql, chb, NT_DIMS, preferred_element_type=F32)
               + jnp.dot(qr, krhb, preferred_element_type=F32))
        s_n = (lax.dot_general(ql, cnb, NT_DIMS, preferred_element_type=F32)
               + lax.dot_general(qr, krnb, NT_DIMS, preferred_element_type=F32))
        return jnp.where(seen_hist, s_h, NEG), jnp.where(seen_new, s_n, NEG)

    def softmax(s):
        s_h, s_n = s
        m = jnp.maximum(jnp.max(s_h, axis=-1, keepdims=True), jnp.max(s_n, axis=-1, keepdims=True))
        p_h = jnp.exp2(s_h - m)
        p_n = jnp.exp2(s_n - m)
        l = jnp.sum(p_h, axis=-1, keepdims=True) + jnp.sum(p_n, axis=-1, keepdims=True)
        return p_h.astype(BF16), p_n.astype(BF16), l

    def latent_out(p, ks):
        (p_h, p_n, l), (chb, _, cnb, _) = p, ks
        olat = (jnp.dot(p_h, chb, preferred_element_type=F32) + jnp.dot(p_n, cnb, preferred_element_type=F32)) / l
        return jnp.concatenate([olat[h * t:(h + 1) * t] for h in range(N_HEADS)], axis=1).astype(BF16)

    qs = [queries(e) for e in elements]
    ks = [keys(e) for e in elements]
    ss = [scores(qs[e], ks[e]) for e in elements]
    ps = [softmax(ss[e]) for e in elements]
    wide = jnp.concatenate([latent_out(ps[e], ks[e]) for e in elements], axis=0)
    o_ref[...] = jnp.dot(wide, wuvbd_ref[...], preferred_element_type=F32)


def _post_kernel(x_ref, o_ref, u_ref, uprev_ref, pe_ref, chist_ref,
                 wpool_ref, spool_ref, wo_ref, ln1g_ref, ln1b_ref, wup_ref, wdw_ref, bdw_ref, wdown_ref,
                 wpg_ref, wpe_ref, ln2g_ref, ln2b_ref,
                 y_ref, clast_ref,
                 ubuf, convbuf, hbuf, cbuf, *, nseq, seq_len, carry, o_transposed, pos0):
    i = pl.program_id(0)
    tm = nseq * seq_len
    hist_rows = slice(CONV_HALO - (CONV_W - 1), CONV_HALO)

    def init_conv_history():
        cbuf[...] = jnp.zeros_like(cbuf)
        cbuf[:, hist_rows, :] = chist_ref[...]

    if carry:
        pl.when(i == 0)(init_conv_history)
    else:
        init_conv_history()

    def gather(buf, halo, shift, cols):
        stride = halo + seq_len
        pieces = [buf[q * stride + halo - shift:q * stride + halo - shift + seq_len, cols] for q in range(nseq)]
        return pieces[0] if nseq == 1 else jnp.concatenate(pieces, axis=0)

    attn = o_ref[...].T if o_transposed else o_ref[...]
    mix_attn = jnp.dot(attn.astype(BF16), wo_ref[0:MLA_WIDTH, :], preferred_element_type=F32)
    pe_proj = jnp.dot(pe_ref[...].astype(BF16), wpe_ref[...], preferred_element_type=F32)

    u = u_ref[...]
    for q in range(nseq):
        uprev = uprev_ref[q]
        if carry:
            uprev = jnp.where(i > 0, uprev, 0.0)
        ubuf[q * (HALO + seq_len):q * (HALO + seq_len) + HALO, :] = uprev
        ubuf[q * (HALO + seq_len) + HALO:(q + 1) * (HALO + seq_len), :] = u[q * seq_len:(q + 1) * seq_len]
    row = lax.broadcasted_iota(jnp.int32, (tm, POOL_GROUP_W), 0)
    frame = row + i * tm if carry else row & (seq_len - 1)
    pooled = []
    for g, w in enumerate(POOL_WINDOWS):
        cols = slice(g * POOL_GROUP_W, (g + 1) * POOL_GROUP_W)
        win = u[:, cols]
        for k in range(1, w):
            win = win + gather(ubuf, HALO, k, cols)
        cnt = jnp.minimum(w, pos0 + frame + 1).astype(F32)
        d = win / cnt - u[:, cols]
        yg = jnp.dot(d.astype(BF16), wpool_ref[g], preferred_element_type=F32) * spool_ref[:, cols]
        pooled.append(yg.astype(BF16))
    pooled = jnp.concatenate(pooled, axis=1)

    mix = mix_attn + jnp.dot(pooled, wo_ref[MLA_WIDTH:D_MODEL, :], preferred_element_type=F32)
    x1 = _layer_norm(ALPHA * x_ref[...] + mix, ln1g_ref[...], ln1b_ref[...])
    x1b = x1.astype(BF16)

    def conv_chunk(cols, buf):
        up = jnp.dot(x1b, wup_ref[:, cols], preferred_element_type=F32)
        if nseq == 1:
            hist = cbuf[0, :, cols]
            cbuf[0, :, cols] = up[tm - CONV_HALO:tm]
            top = lax.broadcasted_iota(jnp.int32, (CONV_HALO, FF_CHUNK), 0)

            def shifted(k):
                rolled = pltpu.roll(up, k, 0)
                head = jnp.where(top < k, pltpu.roll(hist, k, 0), rolled[0:CONV_HALO])
                return jnp.concatenate([head, rolled[CONV_HALO:]], axis=0)

            return (shifted(2) * wdw_ref[0:1, cols] + shifted(1) * wdw_ref[1:2, cols]
                    + up * wdw_ref[2:3, cols] + bdw_ref[:, cols])
        stride = CONV_HALO + seq_len
        for q in range(nseq):
            buf[q * stride:q * stride + CONV_HALO, :] = cbuf[q, :, cols]
            buf[q * stride + CONV_HALO:(q + 1) * stride, :] = up[q * seq_len:(q + 1) * seq_len]
            cbuf[q, :, cols] = buf[q * stride + seq_len:(q + 1) * stride, :]
        everything = slice(None)
        return (gather(buf, CONV_HALO, 2, everything) * wdw_ref[0:1, cols]
                + gather(buf, CONV_HALO, 1, everything) * wdw_ref[1:2, cols]
                + up * wdw_ref[2:3, cols] + bdw_ref[:, cols])

    for j in range(D_FF // FF_CHUNK):
        a = conv_chunk(slice(j * FF_CHUNK, (j + 1) * FF_CHUNK), convbuf.at[2 * j])
        b = conv_chunk(slice(D_FF + j * FF_CHUNK, D_FF + (j + 1) * FF_CHUNK), convbuf.at[2 * j + 1])
        hbuf[:, j * FF_CHUNK:(j + 1) * FF_CHUNK] = (a * jax.nn.sigmoid(a) * b).astype(BF16)
    ffn = jnp.dot(hbuf[...], wdown_ref[...], preferred_element_type=F32)
    clast_ref[...] = cbuf[...]

    ple = jax.nn.sigmoid(jnp.dot(x1b, wpg_ref[...], preferred_element_type=F32)) * pe_proj
    y_ref[...] = _layer_norm(ALPHA * x1 + ffn + ple, ln2g_ref[...], ln2b_ref[...])


def _const_spec(shape):
    nd = len(shape)
    return pl.BlockSpec(shape, lambda *_: (0,) * nd)


def _params(semantics):
    return pltpu.CompilerParams(dimension_semantics=semantics, vmem_limit_bytes=VMEM_LIMIT)


def _pad_heads(w, per_head, used):
    k = w.shape[0]
    w = w.reshape(k, N_HEADS, per_head)[:, :, :used]
    return jnp.pad(w, ((0, 0), (0, 0), (0, HEAD_PAD - used))).reshape(k, N_HEADS * HEAD_PAD)


def _prep_weights(w_in, w_q_b, w_kv_b, w_pool, w_o, w_up, w_down, w_pg, w_pe):
    w_kr = jnp.pad(w_in[:, 512:512 + QK_ROPE], ((0, 0), (ROPE_LO, LANES - ROPE_HI)))
    win = jnp.concatenate([w_in[:, :512], w_kr, w_in[:, 512 + QK_ROPE:]], axis=1).astype(BF16)
    wq_pad = _pad_heads(w_q_b, QK_NOPE + QK_ROPE, QK_NOPE + QK_ROPE).astype(BF16)
    w_kv = w_kv_b.reshape(KV_LORA, N_HEADS, QK_NOPE + V_DIM)
    w_uk, w_uv = w_kv[..., :QK_NOPE], w_kv[..., QK_NOPE:]
    wuk_pad = jnp.pad(w_uk, ((0, 0), (0, 0), (0, HEAD_PAD - QK_NOPE))).reshape(KV_LORA, -1).astype(BF16)
    wuvt = w_uv.reshape(KV_LORA, N_HEADS * V_DIM).T.astype(BF16)
    wukt = jnp.transpose(w_uk, (1, 2, 0)).astype(BF16)
    eye = jnp.eye(N_HEADS, dtype=w_uv.dtype)
    wuv_bd = jnp.einsum('lhv,hg->hlgv', w_uv, eye).reshape(N_HEADS * KV_LORA, N_HEADS * V_DIM).astype(BF16)
    return dict(win=win, wq_pad=wq_pad, wqt=wq_pad.T, wuk_pad=wuk_pad, wuvt=wuvt, wukt=wukt, wuv_bd=wuv_bd,
                wpool=w_pool.astype(BF16), wo=w_o.astype(BF16), wup=w_up.astype(BF16),
                wdown=w_down.astype(BF16), wpg=w_pg.astype(BF16), wpe=w_pe.astype(BF16))


def _rope_inv():
    inv = 1.0 / (ROPE_THETA ** (jnp.arange(0, QK_ROPE, 2, dtype=F32) / QK_ROPE))
    inv_lane = jnp.zeros((1, LANES), F32).at[0, ROPE_LO:ROPE_HI].set(jnp.concatenate([inv, inv]))
    return inv_lane, inv[:, None]


def _project_prompt(x, wb, g_q, g_kv, tm, tk):
    s = x.shape[0]
    _, inv_sub = _rope_inv()
    n = s // tm
    outs = pl.pallas_call(
        functools.partial(_proj_prompt_kernel, tm=tm, tk=tk, pos0=0),
        grid=(n,),
        in_specs=[pl.BlockSpec((tm, D_MODEL), lambda i: (i, 0)),
                  _const_spec(wb['win'].shape), _const_spec((1, Q_LORA)), _const_spec((1, KV_LORA)),
                  _const_spec(wb['wqt'].shape), _const_spec(wb['wuk_pad'].shape), _const_spec(wb['wuvt'].shape),
                  _const_spec((QK_ROPE // 2, 1))],
        out_specs=[pl.BlockSpec((N_HEADS * HEAD_PAD, tm), lambda i: (0, i)),
                   pl.BlockSpec((tm, N_HEADS * HEAD_PAD), lambda i: (i, 0)),
                   pl.BlockSpec((tm // tk, N_HEADS * V_ROWS, tk), lambda i: (i, 0, 0)),
                   pl.BlockSpec((tm, KV_LORA), lambda i: (i, 0)),
                   pl.BlockSpec((QK_ROPE, tm), lambda i: (0, i)),
                   pl.BlockSpec((tm, POOL_WIDTH), lambda i: (i, 0))],
        out_shape=[jax.ShapeDtypeStruct((N_HEADS * HEAD_PAD, s), BF16),
                   jax.ShapeDtypeStruct((s, N_HEADS * HEAD_PAD), BF16),
                   jax.ShapeDtypeStruct((s // tk, N_HEADS * V_ROWS, tk), BF16),
                   jax.ShapeDtypeStruct((s, KV_LORA), F32),
                   jax.ShapeDtypeStruct((QK_ROPE, s), F32),
                   jax.ShapeDtypeStruct((s, POOL_WIDTH), F32)],
        compiler_params=_params(("arbitrary",)),
        name="proj_prompt",
    )(x, wb['win'], g_q.reshape(1, -1), g_kv.reshape(1, -1), wb['wqt'], wb['wuk_pad'], wb['wuvt'], inv_sub)
    return outs


def _project_sample(x, wb, g_q, g_kv, seq_len, pos0):
    rows = x.shape[0]
    assert seq_len & (seq_len - 1) == 0
    inv_lane, _ = _rope_inv()
    return pl.pallas_call(
        functools.partial(_proj_sample_kernel, tm=rows, seq_len=seq_len, pos0=pos0),
        grid=(1,),
        in_specs=[_const_spec((rows, D_MODEL)), _const_spec(wb['win'].shape), _const_spec((1, Q_LORA)),
                  _const_spec((1, KV_LORA)), _const_spec(wb['wq_pad'].shape), _const_spec((1, LANES))],
        out_specs=[_const_spec((rows, N_HEADS * HEAD_PAD)), _const_spec((rows, KV_LORA)),
                   _const_spec((rows, QK_ROPE)), _const_spec((rows, POOL_WIDTH))],
        out_shape=[jax.ShapeDtypeStruct((rows, N_HEADS * HEAD_PAD), F32),
                   jax.ShapeDtypeStruct((rows, KV_LORA), F32),
                   jax.ShapeDtypeStruct((rows, QK_ROPE), F32),
                   jax.ShapeDtypeStruct((rows, POOL_WIDTH), F32)],
        compiler_params=_params(("arbitrary",)),
        name="proj_sample",
    )(x, wb['win'], g_q.reshape(1, -1), g_kv.reshape(1, -1), wb['wq_pad'], inv_lane)


def _attend_prompt(qt, k, vt, tq, tk, hg):
    s = k.shape[0]
    nkv = s // tk
    k3 = k.reshape(nkv, tk, N_HEADS * HEAD_PAD)
    resident = pl.Buffered(1)
    assert tq == tk
    nq = s // tq
    return pl.pallas_call(
        functools.partial(_attn_prompt_kernel, tq=tq, hg=hg, nq=nq),
        grid=(N_HEADS // hg, nq),
        in_specs=[pl.BlockSpec((hg * HEAD_PAD, tq), lambda g, i: (g, i)),
                  pl.BlockSpec((hg * HEAD_PAD, tq), lambda g, i: (g, jnp.minimum(i + 1, nq - 1))),
                  pl.BlockSpec((nkv, tk, hg * HEAD_PAD), lambda g, i: (0, 0, g), pipeline_mode=resident),
                  pl.BlockSpec((nkv, hg * V_ROWS, tk), lambda g, i: (0, g, 0), pipeline_mode=resident)],
        out_specs=pl.BlockSpec((hg * V_DIM, tq), lambda g, i: (g, i)),
        out_shape=jax.ShapeDtypeStruct((MLA_WIDTH, s), F32),
        scratch_shapes=[pltpu.VMEM((hg, tk, tq), BF16), pltpu.VMEM((hg, tk, tq), BF16),
                        pltpu.VMEM((hg, tk, tq), F32), pltpu.VMEM((hg, tk, tq), F32),
                        pltpu.VMEM((hg, 1, tq), F32), pltpu.VMEM((hg, V_ROWS, tq), F32)],
        compiler_params=_params(("arbitrary", "arbitrary")),
        name="attn_prompt",
    )(qt, qt, k3, vt)


def _attend_sample(q, c_new, kr_new, c_hist, kr_hist, wb, t):
    nb, n_past, _ = c_hist.shape
    assert t & (t - 1) == 0
    nbs = SAMPLE_ATTN_GROUP if nb % SAMPLE_ATTN_GROUP == 0 else 1
    return pl.pallas_call(
        functools.partial(_attn_sample_kernel, t=t, n_past=n_past, nbs=nbs),
        grid=(nb // nbs,),
        in_specs=[pl.BlockSpec((nbs * t, N_HEADS * HEAD_PAD), lambda b: (b, 0)),
                  pl.BlockSpec((nbs * t, KV_LORA), lambda b: (b, 0)),
                  pl.BlockSpec((nbs * t, QK_ROPE), lambda b: (b, 0)),
                  pl.BlockSpec((nbs, n_past, KV_LORA), lambda b: (b, 0, 0)),
                  pl.BlockSpec((nbs, QK_ROPE, n_past), lambda b: (b, 0, 0)),
                  _const_spec(wb['wukt'].shape), _const_spec(wb['wuv_bd'].shape)],
        out_specs=pl.BlockSpec((nbs * t, MLA_WIDTH), lambda b: (b, 0)),
        out_shape=jax.ShapeDtypeStruct((nb * t, MLA_WIDTH), F32),
        compiler_params=_params(("arbitrary",)),
        name="attn_sample",
    )(q, c_new, kr_new, c_hist, jnp.swapaxes(kr_hist, 1, 2), wb['wukt'], wb['wuv_bd'])


def _post(x, o, u, uprev, uprev_map, pe, chist, wb, small, *, nseq, seq_len, carry, o_transposed, pos0):
    rows = x.shape[0]
    tm = nseq * seq_len
    n = rows // tm
    assert nseq == 1 if carry else seq_len & (seq_len - 1) == 0
    s_pool, ln1_g, ln1_b, w_dw, b_dw, ln2_g, ln2_b = small
    o_spec = (pl.BlockSpec((MLA_WIDTH, tm), lambda i: (0, i)) if o_transposed
              else pl.BlockSpec((tm, MLA_WIDTH), lambda i: (i, 0)))
    per_tile = (lambda i: (0, 0, 0)) if carry else (lambda i: (i, 0, 0))
    row = lambda v: v.reshape(1, -1)
    return pl.pallas_call(
        functools.partial(_post_kernel, nseq=nseq, seq_len=seq_len, carry=carry, o_transposed=o_transposed,
                          pos0=pos0),
        grid=(n,),
        in_specs=[pl.BlockSpec((tm, D_MODEL), lambda i: (i, 0)),
                  o_spec,
                  pl.BlockSpec((tm, POOL_WIDTH), lambda i: (i, 0)),
                  pl.BlockSpec((nseq, HALO, POOL_WIDTH), uprev_map),
                  pl.BlockSpec((tm, PLE_DIM), lambda i: (i, 0)),
                  pl.BlockSpec((nseq, CONV_W - 1, 2 * D_FF), per_tile),
                  _const_spec(wb['wpool'].shape), _const_spec((1, POOL_WIDTH)), _const_spec(wb['wo'].shape),
                  _const_spec((1, D_MODEL)), _const_spec((1, D_MODEL)),
                  _const_spec(wb['wup'].shape), _const_spec((CONV_W, 2 * D_FF)), _const_spec((1, 2 * D_FF)),
                  _const_spec(wb['wdown'].shape), _const_spec(wb['wpg'].shape), _const_spec(wb['wpe'].shape),
                  _const_spec((1, D_MODEL)), _const_spec((1, D_MODEL))],
        out_specs=[pl.BlockSpec((tm, D_MODEL), lambda i: (i, 0)),
                   pl.BlockSpec((nseq, CONV_HALO, 2 * D_FF), per_tile)],
        out_shape=[jax.ShapeDtypeStruct((rows, D_MODEL), F32),
                   jax.ShapeDtypeStruct((nseq if carry else n * nseq, CONV_HALO, 2 * D_FF), F32)],
        scratch_shapes=[pltpu.VMEM((nseq * (seq_len + HALO), POOL_WIDTH), F32),
                        pltpu.VMEM((2 * (D_FF // FF_CHUNK), nseq * (seq_len + CONV_HALO) if nseq > 1 else CONV_HALO,
                                    FF_CHUNK), F32),
                        pltpu.VMEM((tm, D_FF), BF16),
                        pltpu.VMEM((nseq, CONV_HALO, 2 * D_FF), F32)],
        compiler_params=_params(("arbitrary",)),
        name="post_prompt" if carry else "post_sample",
    )(x, o, u, uprev, pe, chist,
      wb['wpool'], row(s_pool), wb['wo'], row(ln1_g), row(ln1_b), wb['wup'], w_dw, row(b_dw), wb['wdown'],
      wb['wpg'], wb['wpe'], row(ln2_g), row(ln2_b))


def _layer_prompt(x, pe, wb, g_q, g_kv, small, *, tm_proj, tq, tk, hg, tm_post):
    s = x.shape[0]
    assert tm_proj == tq
    qt, k, vt, c, kr, u = _project_prompt(x, wb, g_q, g_kv, tm_proj, tk)
    ot = _attend_prompt(qt, k, vt, tq, tk, hg)
    per = tm_post // HALO
    y, clast = _post(x, ot, u, u.reshape(s // HALO, HALO, POOL_WIDTH),
                     lambda i: (jnp.maximum(i * per - 1, 0), 0, 0),
                     pe, jnp.zeros((1, CONV_W - 1, 2 * D_FF), F32),
                     wb, small, nseq=1, seq_len=tm_post, carry=True, o_transposed=True, pos0=0)
    return y, c, kr.T, u[s - POOL_HIST:], clast[0, CONV_HALO - (CONV_W - 1):]


def _layer_sample(x, pe, c_hist, kr_hist, pool_hist, conv_hist, wb, g_q, g_kv, small):
    nb, t, _ = x.shape
    n_past = c_hist.shape[1]
    xf = x.reshape(nb * t, D_MODEL)
    q, c, kr, u = _project_sample(xf, wb, g_q, g_kv, t, n_past)
    o = _attend_sample(q, c, kr, c_hist, kr_hist, wb, t)
    uprev = jnp.pad(pool_hist, ((0, 0), (HALO - POOL_HIST, 0), (0, 0)))
    y, clast = _post(xf, o, u, uprev, lambda i: (i, 0, 0), pe.reshape(nb * t, PLE_DIM), conv_hist,
                     wb, small, nseq=nb, seq_len=t, carry=False, o_transposed=False, pos0=n_past)
    u3 = u.reshape(nb, t, POOL_WIDTH)
    new_pool = jnp.concatenate([pool_hist, u3], axis=1)[:, -POOL_HIST:]
    return (y.reshape(nb, t, D_MODEL), c.reshape(nb, t, KV_LORA), kr.reshape(nb, t, QK_ROPE), new_pool,
            clast[:, CONV_HALO - (CONV_W - 1):])


def kernel(x_prompt, x_sample, cache_ckv, cache_krope, state_pool, state_ffn_conv, p_prompt, p_sample,
           w_in, g_q, w_q_b, g_kv, w_kv_b, w_pool, s_pool, w_o, ln1_g, ln1_b,
           w_up, w_dw, b_dw, w_down, w_pg, w_pe, ln2_g, ln2_b):
    assert x_prompt.shape[0] == 1 and w_in.shape[0] == DEPTH
    wb = _prep_weights(w_in[0], w_q_b[0], w_kv_b[0], w_pool[0], w_o[0], w_up[0], w_down[0], w_pg[0], w_pe[0])
    small = (s_pool[0], ln1_g[0], ln1_b[0], w_dw[0], b_dw[0], ln2_g[0], ln2_b[0])
    s = x_prompt.shape[1]
    tm_proj = min(512, s)
    tq = tk = min(512, s)
    yp, cp, krp, poolp, convp = _layer_prompt(x_prompt[0], p_prompt[0, 0], wb, g_q[0], g_kv[0], small,
                                              tm_proj=tm_proj, tq=tq, tk=tk, hg=ATTN_HEAD_GROUP, tm_post=min(512, s))
    ys, cs, krs, pools, convs = _layer_sample(x_sample, p_sample[0], cache_ckv[0], cache_krope[0],
                                              state_pool[0], state_ffn_conv[0], wb, g_q[0], g_kv[0], small)
    return (yp[None], ys, cp[None, None], krp[None, None], poolp[None, None], convp[None, None],
            cs[None], krs[None], pools[None], convs[None])
```

```python
import functools
import math

import jax
import jax.numpy as jnp
from jax import lax
from jax.experimental import pallas as pl
from jax.experimental.pallas import tpu as pltpu

F32 = jnp.float32
BF16 = jnp.bfloat16

D_MODEL = 1024
CHUNK = 64
CHUNK_SHIFT = 6
ATTN_HEAD_GROUP = 4
SAMPLE_ATTN_GROUP = 2
N_HEADS = 8
QK_NOPE = 64
QK_ROPE = 32
V_DIM = 64
V_ROWS = V_DIM + 16
Q_LORA = 256
KV_LORA = 256
ROPE_THETA = 10000.0
MLA_WIDTH = N_HEADS * V_DIM
POOL_WINDOWS = (2, 4, 8, 16)
POOL_GROUP_W = 128
POOL_WIDTH = D_MODEL - MLA_WIDTH
POOL_HIST = max(POOL_WINDOWS) - 1
D_FF = 2816
CONV_W = 3
PLE_DIM = 256
DEPTH = 1
ALPHA = (2 * DEPTH) ** 0.25
LN_EPS = 1e-5
RMS_EPS = 1e-6
NEG = -1e30
ATTN_SCALE = 1.0 / math.sqrt(QK_NOPE + QK_ROPE)

LANES = 128
HEAD_PAD = 128
ROPE_LO = QK_NOPE
ROPE_MID = QK_NOPE + QK_ROPE // 2
ROPE_HI = QK_NOPE + QK_ROPE
SHIFT_LANE = ROPE_HI
HALO = 16
CONV_HALO = 8
FF_CHUNK = 256
Q_SCALE = ATTN_SCALE * math.log2(math.e)
VMEM_LIMIT = 56 * 1024 * 1024

NT_DIMS = (((1,), (1,)), ((), ()))


def _rms(x, g):
    return x * lax.rsqrt(jnp.mean(x * x, axis=-1, keepdims=True) + RMS_EPS) * g


def _layer_norm(x, g, b):
    mu = jnp.mean(x, axis=-1, keepdims=True)
    xc = x - mu
    var = jnp.mean(xc * xc, axis=-1, keepdims=True)
    return xc * lax.rsqrt(var + LN_EPS) * g + b


def _token_rope_tables(pos_col, inv_lane):
    ang = pos_col.astype(F32) * inv_lane
    lane = lax.broadcasted_iota(jnp.int32, ang.shape, 1)
    first = lane < ROPE_MID
    sin = jnp.sin(ang)
    return jnp.cos(ang), jnp.where(first, -sin, sin), first


def _token_rope(x, cos, sin_signed, first):
    partner = jnp.where(first, pltpu.roll(x, LANES - QK_ROPE // 2, 1), pltpu.roll(x, QK_ROPE // 2, 1))
    return x * cos + partner * sin_signed


def _proj_prompt_kernel(x_ref, win_ref, gq_ref, gkv_ref, wqt_ref, wuk_ref, wuvt_ref, invs_ref,
                        wup_f32, wdown_f32, wo_f32, wpg_f32,
                        qt_ref, k_ref, vt_ref, c_ref, kr_ref, u_ref,
                        wup_bf16, wdown_bf16, wo_bf16, wpg_bf16, *, tm, tk, pos0):
    i = pl.program_id(0)
    for src, dst in ((wup_f32, wup_bf16), (wdown_f32, wdown_bf16), (wo_f32, wo_bf16), (wpg_f32, wpg_bf16)):
        dst[...] = src[...].astype(BF16)
    z = jnp.dot(x_ref[...].astype(BF16), win_ref[...], preferred_element_type=F32)
    q_a = z[:, 0:Q_LORA]
    c_raw = z[:, Q_LORA:Q_LORA + KV_LORA]
    kr_pad = z[:, 512:640]
    u_ref[...] = z[:, 640:640 + POOL_WIDTH]

    qn = _rms(q_a, gq_ref[...]).astype(BF16)
    c = _rms(c_raw, gkv_ref[...])
    c_ref[...] = c
    cb = c.astype(BF16)

    pos_row = pos0 + i * tm + lax.broadcasted_iota(jnp.int32, (QK_ROPE // 2, tm), 1)
    ang = pos_row.astype(F32) * invs_ref[...]
    cos_t, sin_t = jnp.cos(ang), jnp.sin(ang)
    lo, hi = (ROPE_LO, tm), (HEAD_PAD - ROPE_HI, tm)
    cos = jnp.concatenate([jnp.ones(lo, F32), cos_t, cos_t, jnp.ones(hi, F32)], axis=0).T
    sin_signed = jnp.concatenate([jnp.zeros(lo, F32), -sin_t, sin_t, jnp.zeros(hi, F32)], axis=0).T
    first = lax.broadcasted_iota(jnp.int32, (tm, LANES), 1) < ROPE_MID
    kr_rot = _token_rope(kr_pad, cos, sin_signed, first)
    kr_ref[...] = kr_rot.T[ROPE_LO:ROPE_HI, :]

    k_nope = jnp.dot(cb, wuk_ref[...], preferred_element_type=F32)
    k_tail = kr_rot + jnp.where(lax.broadcasted_iota(jnp.int32, (tm, LANES), 1) == SHIFT_LANE, 1.0, 0.0)
    for h in range(N_HEADS):
        sl = slice(h * HEAD_PAD, (h + 1) * HEAD_PAD)
        k_ref[:, sl] = (k_nope[:, sl] + k_tail).astype(BF16)

    vt = lax.dot_general(wuvt_ref[...], cb, NT_DIMS, preferred_element_type=F32)
    ones = jnp.ones((V_ROWS - V_DIM, tk), BF16)
    for s in range(tm // tk):
        for h in range(N_HEADS):
            vt_ref[s, h * V_ROWS:h * V_ROWS + V_DIM, :] = vt[h * V_DIM:(h + 1) * V_DIM, s * tk:(s + 1) * tk].astype(BF16)
            vt_ref[s, h * V_ROWS + V_DIM:(h + 1) * V_ROWS, :] = ones

    qt = lax.dot_general(wqt_ref[...], qn, NT_DIMS, preferred_element_type=F32)
    shift_row = lax.broadcasted_iota(jnp.int32, (HEAD_PAD - SHIFT_LANE, tm), 0) == 0
    for h in range(N_HEADS):
        b0 = h * HEAD_PAD
        qt_ref[b0:b0 + ROPE_LO, :] = (qt[b0:b0 + ROPE_LO] * Q_SCALE).astype(BF16)
        x1 = qt[b0 + ROPE_LO:b0 + ROPE_MID]
        x2 = qt[b0 + ROPE_MID:b0 + ROPE_HI]
        qt_ref[b0 + ROPE_LO:b0 + ROPE_MID, :] = ((x1 * cos_t - x2 * sin_t) * Q_SCALE).astype(BF16)
        qt_ref[b0 + ROPE_MID:b0 + ROPE_HI, :] = ((x2 * cos_t + x1 * sin_t) * Q_SCALE).astype(BF16)
        qt_ref[b0 + ROPE_HI:b0 + HEAD_PAD, :] = jnp.zeros((HEAD_PAD - ROPE_HI, tm), BF16)
        lead = jnp.dot(k_ref[0:CHUNK, b0:b0 + HEAD_PAD], qt_ref[b0:b0 + HEAD_PAD, :],
                       preferred_element_type=F32)
        shift = jnp.max(lead, axis=0, keepdims=True)
        qt_ref[b0 + SHIFT_LANE:b0 + HEAD_PAD, :] = jnp.where(shift_row, -shift, 0.0).astype(BF16)


def _proj_sample_kernel(x_ref, win_ref, gq_ref, gkv_ref, wq_ref, invl_ref,
                        q_ref, c_ref, kr_ref, u_ref, *, tm, seq_len, pos0):
    z = jnp.dot(x_ref[...].astype(BF16), win_ref[...], preferred_element_type=F32)
    q_a = z[:, 0:Q_LORA]
    c_raw = z[:, Q_LORA:Q_LORA + KV_LORA]
    kr_pad = z[:, 512:640]
    u_ref[...] = z[:, 640:640 + POOL_WIDTH]

    qn = _rms(q_a, gq_ref[...]).astype(BF16)
    c_ref[...] = _rms(c_raw, gkv_ref[...])

    row = lax.broadcasted_iota(jnp.int32, (tm, LANES), 0)
    pos_col = pos0 + (row & (seq_len - 1))
    cos, sin_signed, first = _token_rope_tables(pos_col, invl_ref[...])
    kr_ref[...] = _token_rope(kr_pad, cos, sin_signed, first)[:, ROPE_LO:ROPE_HI]

    q = jnp.dot(qn, wq_ref[...], preferred_element_type=F32)
    for h in range(N_HEADS):
        sl = slice(h * HEAD_PAD, (h + 1) * HEAD_PAD)
        q_ref[:, sl] = _token_rope(q[:, sl], cos, sin_signed, first) * Q_SCALE


def _attn_prompt_kernel(qt_ref, qn_ref, k_ref, vt_ref, o_ref, pa_ref, pb_ref, sa_ref, sb_ref, m_ref, acc_ref, *, tq, hg, nq):
    i = pl.program_id(1)

    def scores(j, h):
        return jnp.dot(k_ref[j, :, h * HEAD_PAD:(h + 1) * HEAD_PAD], qt_ref[h * HEAD_PAD:(h + 1) * HEAD_PAD, :],
                       preferred_element_type=F32)

    kchunk = lax.broadcasted_iota(jnp.int32, (tq, tq), 0) >> CHUNK_SHIFT
    qchunk = lax.broadcasted_iota(jnp.int32, (tq, tq), 1) >> CHUNK_SHIFT
    visible = kchunk <= qchunk

    def values(jv, h, p):
        return jnp.dot(vt_ref[jv, h * V_ROWS:(h + 1) * V_ROWS, :], p, preferred_element_type=F32)

    def to_prob(s):
        return jnp.exp2(s).astype(BF16)

    def fixed_shift_step(cur_ref, jv, issue_next):
        for h in range(hg):
            p = cur_ref[h]
            issue_next(h)
            acc_ref[h] += values(jv, h, p)

    def running_max_step(cur_ref, jv, issue_next):
        for h in range(hg):
            s = cur_ref[h]
            m_old = m_ref[h]
            m_new = jnp.maximum(m_old, jnp.max(s, axis=0, keepdims=True))
            alpha = jnp.exp2(m_old - m_new)
            p = jnp.exp2(s - m_new).astype(BF16)
            m_ref[h] = m_new
            issue_next(h)
            acc_ref[h] = alpha * acc_ref[h] + values(jv, h, p)

    def load_diagonal(bank_a, encode):
        for h in range(hg):
            bank_a[h] = encode(jnp.where(visible, scores(i, h), NEG))

    def sweep(step, bank_a, bank_b, encode, pairs_per_trip):
        def block_into(bank_ref, j):
            def issue(h):
                bank_ref[h] = encode(scores(j, h))
            return issue

        def next_diagonal_into_a(h):
            j = jnp.minimum(i + 1, nq - 1)
            s = jnp.dot(k_ref[j, :, h * HEAD_PAD:(h + 1) * HEAD_PAD], qn_ref[h * HEAD_PAD:(h + 1) * HEAD_PAD, :],
                        preferred_element_type=F32)
            bank_a[h] = encode(jnp.where(visible, s, NEG))

        def pair(t):
            step(bank_a, jnp.where(t == 0, i, 2 * t - 1), block_into(bank_b, 2 * t))
            step(bank_b, 2 * t, block_into(bank_a, 2 * t + 1))

        n_pairs = i // 2
        n_trips = n_pairs // pairs_per_trip

        def trip(u, c):
            for r in range(pairs_per_trip):
                pair(pairs_per_trip * u + r)
            return c

        lax.fori_loop(0, n_trips, trip, 0)
        for r in range(pairs_per_trip - 1):
            pl.when(n_pairs - n_trips * pairs_per_trip > r)(
                functools.partial(pair, n_trips * pairs_per_trip + r))

        @pl.when(i % 2 == 0)
        def _():
            step(bank_a, jnp.maximum(i - 1, 0), next_diagonal_into_a)

        @pl.when(i % 2 == 1)
        def _():
            step(bank_a, jnp.where(i == 1, i, i - 2), block_into(bank_b, i - 1))
            step(bank_b, i - 1, next_diagonal_into_a)

    acc_ref[...] = jnp.zeros(acc_ref.shape, F32)
    pl.when(i == 0)(functools.partial(load_diagonal, pa_ref, to_prob))
    sweep(fixed_shift_step, pa_ref, pb_ref, to_prob, pairs_per_trip=2)

    bad = jnp.where(jnp.isfinite(acc_ref[...]), 0.0, 1.0)
    bad = jnp.max(jnp.max(jnp.max(bad, axis=0), axis=0, keepdims=True), axis=1, keepdims=True)

    @pl.when(bad[0, 0] > 0.0)
    def _():
        m_ref[...] = jnp.full(m_ref.shape, NEG, F32)
        acc_ref[...] = jnp.zeros(acc_ref.shape, F32)
        load_diagonal(sa_ref, lambda s: s)
        sweep(running_max_step, sa_ref, sb_ref, lambda s: s, pairs_per_trip=1)
        for h in range(hg):
            pa_ref[h] = to_prob(sa_ref[h])

    for h in range(hg):
        o_ref[h * V_DIM:(h + 1) * V_DIM, :] = acc_ref[h, 0:V_DIM, :] / acc_ref[h, V_DIM:V_DIM + 1, :]


def _attn_sample_kernel(q_ref, cn_ref, krn_ref, ch_ref, krht_ref, wukt_ref, wuvbd_ref, o_ref, *, t, n_past, nbs):
    rows = N_HEADS * t
    qchunk = (n_past + (lax.broadcasted_iota(jnp.int32, (rows, 1), 0) & (t - 1))) >> CHUNK_SHIFT
    seen_hist = (lax.broadcasted_iota(jnp.int32, (rows, n_past), 1) >> CHUNK_SHIFT) <= qchunk
    seen_new = ((n_past + lax.broadcasted_iota(jnp.int32, (rows, t), 1)) >> CHUNK_SHIFT) <= qchunk
    elements = range(nbs)

    def queries(e):
        q = q_ref[e * t:(e + 1) * t, :]
        qlat, qrope = [], []
        for h in range(N_HEADS):
            b0 = h * HEAD_PAD
            qn = q[:, b0:b0 + QK_NOPE].astype(BF16)
            qlat.append(jnp.dot(qn, wukt_ref[h], preferred_element_type=F32))
            qrope.append(q[:, b0 + ROPE_LO:b0 + ROPE_HI])
        return (jnp.concatenate(qlat, axis=0).astype(BF16),
                jnp.concatenate(qrope, axis=0).astype(BF16))

    def keys(e):
        return (ch_ref[e].astype(BF16),
                krht_ref[e].astype(BF16),
                cn_ref[e * t:(e + 1) * t, :].astype(BF16),
                krn_ref[e * t:(e + 1) * t, :].astype(BF16))

    def scores(qs, ks):
        (ql, qr), (chb, krhb, cnb, krnb) = qs, ks
        s_h = (lax.dot_general(ql, chb, NT_DIMS, preferred_element_type=F32)
               + jnp.dot(qr, krhb, preferred_element_type=F32))
        s_n = (lax.dot_general(ql, cnb, NT_DIMS, preferred_element_type=F32)
               + lax.dot_general(qr, krnb, NT_DIMS, preferred_element_type=F32))
        return jnp.where(seen_hist, s_h, NEG), jnp.where(seen_new, s_n, NEG)

    def softmax(s):
        s_h, s_n = s
        m = jnp.maximum(jnp.max(s_h, axis=-1, keepdims=True), jnp.max(s_n, axis=-1, keepdims=True))
        p_h = jnp.exp2(s_h - m)
        p_n = jnp.exp2(s_n - m)
        l = jnp.sum(p_h, axis=-1, keepdims=True) + jnp.sum(p_n, axis=-1, keepdims=True)
        return p_h.astype(BF16), p_n.astype(BF16), l

    def latent_out(p, ks):
        (p_h, p_n, l), (chb, _, cnb, _) = p, ks
        olat = (jnp.dot(p_h, chb, preferred_element_type=F32) + jnp.dot(p_n, cnb, preferred_element_type=F32)) / l
        return jnp.concatenate([olat[h * t:(h + 1) * t] for h in range(N_HEADS)], axis=1).astype(BF16)

    qs = [queries(e) for e in elements]
    ks = [keys(e) for e in elements]
    ss = [scores(qs[e], ks[e]) for e in elements]
    ps = [softmax(ss[e]) for e in elements]
    wide = jnp.concatenate([latent_out(ps[e], ks[e]) for e in elements], axis=0)
    o_ref[...] = jnp.dot(wide, wuvbd_ref[...], preferred_element_type=F32)


def _post_kernel(x_ref, o_ref, u_ref, uprev_ref, pe_ref, chist_ref,
                 wpool_ref, spool_ref, wo_ref, ln1g_ref, ln1b_ref, wup_ref, wdw_ref, bdw_ref, wdown_ref,
                 wpg_ref, wpe_ref, ln2g_ref, ln2b_ref,
                 y_ref, clast_ref,
                 ubuf, convbuf, hbuf, cbuf, *, nseq, seq_len, carry, o_transposed, pos0):
    i = pl.program_id(0)
    tm = nseq * seq_len
    hist_rows = slice(CONV_HALO - (CONV_W - 1), CONV_HALO)

    def init_conv_history():
        cbuf[...] = jnp.zeros_like(cbuf)
        cbuf[:, hist_rows, :] = chist_ref[...]

    if carry:
        pl.when(i == 0)(init_conv_history)
    else:
        init_conv_history()

    def gather(buf, halo, shift, cols):
        stride = halo + seq_len
        pieces = [buf[q * stride + halo - shift:q * stride + halo - shift + seq_len, cols] for q in range(nseq)]
        return pieces[0] if nseq == 1 else jnp.concatenate(pieces, axis=0)

    attn = o_ref[...].T if o_transposed else o_ref[...]
    mix_attn = jnp.dot(attn.astype(BF16), wo_ref[0:MLA_WIDTH, :], preferred_element_type=F32)
    pe_proj = jnp.dot(pe_ref[...].astype(BF16), wpe_ref[...], preferred_element_type=F32)

    u = u_ref[...]
    for q in range(nseq):
        uprev = uprev_ref[q]
        if carry:
            uprev = jnp.where(i > 0, uprev, 0.0)
        ubuf[q * (HALO + seq_len):q * (HALO + seq_len) + HALO, :] = uprev
        ubuf[q * (HALO + seq_len) + HALO:(q + 1) * (HALO + seq_len), :] = u[q * seq_len:(q + 1) * seq_len]
    row = lax.broadcasted_iota(jnp.int32, (tm, POOL_GROUP_W), 0)
    frame = row + i * tm if carry else row & (seq_len - 1)
    pooled = []
    for g, w in enumerate(POOL_WINDOWS):
        cols = slice(g * POOL_GROUP_W, (g + 1) * POOL_GROUP_W)
        win = u[:, cols]
        for k in range(1, w):
            win = win + gather(ubuf, HALO, k, cols)
        cnt = jnp.minimum(w, pos0 + frame + 1).astype(F32)
        d = win / cnt - u[:, cols]
        yg = jnp.dot(d.astype(BF16), wpool_ref[g], preferred_element_type=F32) * spool_ref[:, cols]
        pooled.append(yg.astype(BF16))
    pooled = jnp.concatenate(pooled, axis=1)

    mix = mix_attn + jnp.dot(pooled, wo_ref[MLA_WIDTH:D_MODEL, :], preferred_element_type=F32)
    x1 = _layer_norm(ALPHA * x_ref[...] + mix, ln1g_ref[...], ln1b_ref[...])
    x1b = x1.astype(BF16)

    def conv_chunk(cols, buf):
        up = jnp.dot(x1b, wup_ref[:, cols], preferred_element_type=F32)
        if nseq == 1:
            hist = cbuf[0, :, cols]
            cbuf[0, :, cols] = up[tm - CONV_HALO:tm]
            top = lax.broadcasted_iota(jnp.int32, (CONV_HALO, FF_CHUNK), 0)

            def shifted(k):
                rolled = pltpu.roll(up, k, 0)
                head = jnp.where(top < k, pltpu.roll(hist, k, 0), rolled[0:CONV_HALO])
                return jnp.concatenate([head, rolled[CONV_HALO:]], axis=0)

            return (shifted(2) * wdw_ref[0:1, cols] + shifted(1) * wdw_ref[1:2, cols]
                    + up * wdw_ref[2:3, cols] + bdw_ref[:, cols])
        stride = CONV_HALO + seq_len
        for q in range(nseq):
            buf[q * stride:q * stride + CONV_HALO, :] = cbuf[q, :, cols]
            buf[q * stride + CONV_HALO:(q + 1) * stride, :] = up[q * seq_len:(q + 1) * seq_len]
            cbuf[q, :, cols] = buf[q * stride + seq_len:(q + 1) * stride, :]
        everything = slice(None)
        return (gather(buf, CONV_HALO, 2, everything) * wdw_ref[0:1, cols]
                + gather(buf, CONV_HALO, 1, everything) * wdw_ref[1:2, cols]
                + up * wdw_ref[2:3, cols] + bdw_ref[:, cols])

    for j in range(D_FF // FF_CHUNK):
        a = conv_chunk(slice(j * FF_CHUNK, (j + 1) * FF_CHUNK), convbuf.at[2 * j])
        b = conv_chunk(slice(D_FF + j * FF_CHUNK, D_FF + (j + 1) * FF_CHUNK), convbuf.at[2 * j + 1])
        hbuf[:, j * FF_CHUNK:(j + 1) * FF_CHUNK] = (a * jax.nn.sigmoid(a) * b).astype(BF16)
    ffn = jnp.dot(hbuf[...], wdown_ref[...], preferred_element_type=F32)
    clast_ref[...] = cbuf[...]

    ple = jax.nn.sigmoid(jnp.dot(x1b, wpg_ref[...], preferred_element_type=F32)) * pe_proj
    y_ref[...] = _layer_norm(ALPHA * x1 + ffn + ple, ln2g_ref[...], ln2b_ref[...])


def _const_spec(shape):
    nd = len(shape)
    return pl.BlockSpec(shape, lambda *_: (0,) * nd)


def _params(semantics):
    return pltpu.CompilerParams(dimension_semantics=semantics, vmem_limit_bytes=VMEM_LIMIT)


def _pad_heads(w, per_head, used):
    k = w.shape[0]
    w = w.reshape(k, N_HEADS, per_head)[:, :, :used]
    return jnp.pad(w, ((0, 0), (0, 0), (0, HEAD_PAD - used))).reshape(k, N_HEADS * HEAD_PAD)


def _prep_weights(w_in, w_q_b, w_kv_b, w_pool, w_pe):
    w_kr = jnp.pad(w_in[:, 512:512 + QK_ROPE], ((0, 0), (ROPE_LO, LANES - ROPE_HI)))
    win = jnp.concatenate([w_in[:, :512], w_kr, w_in[:, 512 + QK_ROPE:]], axis=1).astype(BF16)
    wq_pad = _pad_heads(w_q_b, QK_NOPE + QK_ROPE, QK_NOPE + QK_ROPE).astype(BF16)
    w_kv = w_kv_b.reshape(KV_LORA, N_HEADS, QK_NOPE + V_DIM)
    w_uk, w_uv = w_kv[..., :QK_NOPE], w_kv[..., QK_NOPE:]
    wuk_pad = jnp.pad(w_uk, ((0, 0), (0, 0), (0, HEAD_PAD - QK_NOPE))).reshape(KV_LORA, -1).astype(BF16)
    wuvt = w_uv.reshape(KV_LORA, N_HEADS * V_DIM).T.astype(BF16)
    wukt = jnp.transpose(w_uk, (1, 2, 0)).astype(BF16)
    eye = jnp.eye(N_HEADS, dtype=w_uv.dtype)
    wuv_bd = jnp.einsum('lhv,hg->hlgv', w_uv, eye).reshape(N_HEADS * KV_LORA, N_HEADS * V_DIM).astype(BF16)
    return dict(win=win, wq_pad=wq_pad, wqt=wq_pad.T, wuk_pad=wuk_pad, wuvt=wuvt, wukt=wukt, wuv_bd=wuv_bd,
                wpool=w_pool.astype(BF16), wpe=w_pe.astype(BF16))


def _rope_inv():
    inv = 1.0 / (ROPE_THETA ** (jnp.arange(0, QK_ROPE, 2, dtype=F32) / QK_ROPE))
    inv_lane = jnp.zeros((1, LANES), F32).at[0, ROPE_LO:ROPE_HI].set(jnp.concatenate([inv, inv]))
    return inv_lane, inv[:, None]


def _project_prompt(x, wb, g_q, g_kv, tm, tk, w_up, w_down, w_o, w_pg):
    s = x.shape[0]
    _, inv_sub = _rope_inv()
    n = s // tm
    casts = [w_up, w_down, w_o, w_pg]

    def slab(w):
        rows = w.shape[0]
        per = next(r for r in range(16 * pl.cdiv(rows, 16 * n), rows + 1, 16) if rows % r == 0)
        return pl.BlockSpec((per, w.shape[1]), lambda i: (jnp.minimum(i, rows // per - 1), 0))

    outs = pl.pallas_call(
        functools.partial(_proj_prompt_kernel, tm=tm, tk=tk, pos0=0),
        grid=(n,),
        in_specs=[pl.BlockSpec((tm, D_MODEL), lambda i: (i, 0)),
                  _const_spec(wb['win'].shape), _const_spec((1, Q_LORA)), _const_spec((1, KV_LORA)),
                  _const_spec(wb['wqt'].shape), _const_spec(wb['wuk_pad'].shape), _const_spec(wb['wuvt'].shape),
                  _const_spec((QK_ROPE // 2, 1))] + [slab(w) for w in casts],
        out_specs=[pl.BlockSpec((N_HEADS * HEAD_PAD, tm), lambda i: (0, i)),
                   pl.BlockSpec((tm, N_HEADS * HEAD_PAD), lambda i: (i, 0)),
                   pl.BlockSpec((tm // tk, N_HEADS * V_ROWS, tk), lambda i: (i, 0, 0)),
                   pl.BlockSpec((tm, KV_LORA), lambda i: (i, 0)),
                   pl.BlockSpec((QK_ROPE, tm), lambda i: (0, i)),
                   pl.BlockSpec((tm, POOL_WIDTH), lambda i: (i, 0))] + [slab(w) for w in casts],
        out_shape=[jax.ShapeDtypeStruct((N_HEADS * HEAD_PAD, s), BF16),
                   jax.ShapeDtypeStruct((s, N_HEADS * HEAD_PAD), BF16),
                   jax.ShapeDtypeStruct((s // tk, N_HEADS * V_ROWS, tk), BF16),
                   jax.ShapeDtypeStruct((s, KV_LORA), F32),
                   jax.ShapeDtypeStruct((QK_ROPE, s), F32),
                   jax.ShapeDtypeStruct((s, POOL_WIDTH), F32)]
                  + [jax.ShapeDtypeStruct(w.shape, BF16) for w in casts],
        compiler_params=_params(("arbitrary",)),
        name="proj_prompt",
    )(x, wb['win'], g_q.reshape(1, -1), g_kv.reshape(1, -1), wb['wqt'], wb['wuk_pad'], wb['wuvt'], inv_sub, *casts)
    return outs[:6], dict(zip(('wup', 'wdown', 'wo', 'wpg'), outs[6:]))


def _project_sample(x, wb, g_q, g_kv, seq_len, pos0):
    rows = x.shape[0]
    assert seq_len & (seq_len - 1) == 0
    inv_lane, _ = _rope_inv()
    return pl.pallas_call(
        functools.partial(_proj_sample_kernel, tm=rows, seq_len=seq_len, pos0=pos0),
        grid=(1,),
        in_specs=[_const_spec((rows, D_MODEL)), _const_spec(wb['win'].shape), _const_spec((1, Q_LORA)),
                  _const_spec((1, KV_LORA)), _const_spec(wb['wq_pad'].shape), _const_spec((1, LANES))],
        out_specs=[_const_spec((rows, N_HEADS * HEAD_PAD)), _const_spec((rows, KV_LORA)),
                   _const_spec((rows, QK_ROPE)), _const_spec((rows, POOL_WIDTH))],
        out_shape=[jax.ShapeDtypeStruct((rows, N_HEADS * HEAD_PAD), F32),
                   jax.ShapeDtypeStruct((rows, KV_LORA), F32),
                   jax.ShapeDtypeStruct((rows, QK_ROPE), F32),
                   jax.ShapeDtypeStruct((rows, POOL_WIDTH), F32)],
        compiler_params=_params(("arbitrary",)),
        name="proj_sample",
    )(x, wb['win'], g_q.reshape(1, -1), g_kv.reshape(1, -1), wb['wq_pad'], inv_lane)


def _attend_prompt(qt, k, vt, tq, tk, hg):
    s = k.shape[0]
    nkv = s // tk
    k3 = k.reshape(nkv, tk, N_HEADS * HEAD_PAD)
    resident = pl.Buffered(1)
    assert tq == tk
    nq = s // tq
    return pl.pallas_call(
        functools.partial(_attn_prompt_kernel, tq=tq, hg=hg, nq=nq),
        grid=(N_HEADS // hg, nq),
        in_specs=[pl.BlockSpec((hg * HEAD_PAD, tq), lambda g, i: (g, i)),
                  pl.BlockSpec((hg * HEAD_PAD, tq), lambda g, i: (g, jnp.minimum(i + 1, nq - 1))),
                  pl.BlockSpec((nkv, tk, hg * HEAD_PAD), lambda g, i: (0, 0, g), pipeline_mode=resident),
                  pl.BlockSpec((nkv, hg * V_ROWS, tk), lambda g, i: (0, g, 0), pipeline_mode=resident)],
        out_specs=pl.BlockSpec((hg * V_DIM, tq), lambda g, i: (g, i)),
        out_shape=jax.ShapeDtypeStruct((MLA_WIDTH, s), F32),
        scratch_shapes=[pltpu.VMEM((hg, tk, tq), BF16), pltpu.VMEM((hg, tk, tq), BF16),
                        pltpu.VMEM((hg, tk, tq), F32), pltpu.VMEM((hg, tk, tq), F32),
                        pltpu.VMEM((hg, 1, tq), F32), pltpu.VMEM((hg, V_ROWS, tq), F32)],
        compiler_params=_params(("arbitrary", "arbitrary")),
        name="attn_prompt",
    )(qt, qt, k3, vt)


def _attend_sample(q, c_new, kr_new, c_hist, kr_hist, wb, t):
    nb, n_past, _ = c_hist.shape
    assert t & (t - 1) == 0
    nbs = SAMPLE_ATTN_GROUP if nb % SAMPLE_ATTN_GROUP == 0 else 1
    return pl.pallas_call(
        functools.partial(_attn_sample_kernel, t=t, n_past=n_past, nbs=nbs),
        grid=(nb // nbs,),
        in_specs=[pl.BlockSpec((nbs * t, N_HEADS * HEAD_PAD), lambda b: (b, 0)),
                  pl.BlockSpec((nbs * t, KV_LORA), lambda b: (b, 0)),
                  pl.BlockSpec((nbs * t, QK_ROPE), lambda b: (b, 0)),
                  pl.BlockSpec((nbs, n_past, KV_LORA), lambda b: (b, 0, 0)),
                  pl.BlockSpec((nbs, QK_ROPE, n_past), lambda b: (b, 0, 0)),
                  _const_spec(wb['wukt'].shape), _const_spec(wb['wuv_bd'].shape)],
        out_specs=pl.BlockSpec((nbs * t, MLA_WIDTH), lambda b: (b, 0)),
        out_shape=jax.ShapeDtypeStruct((nb * t, MLA_WIDTH), F32),
        compiler_params=_params(("arbitrary",)),
        name="attn_sample",
    )(q, c_new, kr_new, c_hist, jnp.swapaxes(kr_hist, 1, 2), wb['wukt'], wb['wuv_bd'])


def _post(x, o, u, uprev, uprev_map, pe, chist, wb, small, *, nseq, seq_len, carry, o_transposed, pos0):
    rows = x.shape[0]
    tm = nseq * seq_len
    n = rows // tm
    assert nseq == 1 if carry else seq_len & (seq_len - 1) == 0
    s_pool, ln1_g, ln1_b, w_dw, b_dw, ln2_g, ln2_b = small
    o_spec = (pl.BlockSpec((MLA_WIDTH, tm), lambda i: (0, i)) if o_transposed
              else pl.BlockSpec((tm, MLA_WIDTH), lambda i: (i, 0)))
    per_tile = (lambda i: (0, 0, 0)) if carry else (lambda i: (i, 0, 0))
    row = lambda v: v.reshape(1, -1)
    return pl.pallas_call(
        functools.partial(_post_kernel, nseq=nseq, seq_len=seq_len, carry=carry, o_transposed=o_transposed,
                          pos0=pos0),
        grid=(n,),
        in_specs=[pl.BlockSpec((tm, D_MODEL), lambda i: (i, 0)),
                  o_spec,
                  pl.BlockSpec((tm, POOL_WIDTH), lambda i: (i, 0)),
                  pl.BlockSpec((nseq, HALO, POOL_WIDTH), uprev_map),
                  pl.BlockSpec((tm, PLE_DIM), lambda i: (i, 0)),
                  pl.BlockSpec((nseq, CONV_W - 1, 2 * D_FF), per_tile),
                  _const_spec(wb['wpool'].shape), _const_spec((1, POOL_WIDTH)), _const_spec(wb['wo'].shape),
                  _const_spec((1, D_MODEL)), _const_spec((1, D_MODEL)),
                  _const_spec(wb['wup'].shape), _const_spec((CONV_W, 2 * D_FF)), _const_spec((1, 2 * D_FF)),
                  _const_spec(wb['wdown'].shape), _const_spec(wb['wpg'].shape), _const_spec(wb['wpe'].shape),
                  _const_spec((1, D_MODEL)), _const_spec((1, D_MODEL))],
        out_specs=[pl.BlockSpec((tm, D_MODEL), lambda i: (i, 0)),
                   pl.BlockSpec((nseq, CONV_HALO, 2 * D_FF), per_tile)],
        out_shape=[jax.ShapeDtypeStruct((rows, D_MODEL), F32),
                   jax.ShapeDtypeStruct((nseq if carry else n * nseq, CONV_HALO, 2 * D_FF), F32)],
        scratch_shapes=[pltpu.VMEM((nseq * (seq_len + HALO), POOL_WIDTH), F32),
                        pltpu.VMEM((2 * (D_FF // FF_CHUNK), nseq * (seq_len + CONV_HALO) if nseq > 1 else CONV_HALO,
                                    FF_CHUNK), F32),
                        pltpu.VMEM((tm, D_FF), BF16),
                        pltpu.VMEM((nseq, CONV_HALO, 2 * D_FF), F32)],
        compiler_params=_params(("arbitrary",)),
        name="post_prompt" if carry else "post_sample",
    )(x, o, u, uprev, pe, chist,
      wb['wpool'], row(s_pool), wb['wo'], row(ln1_g), row(ln1_b), wb['wup'], w_dw, row(b_dw), wb['wdown'],
      wb['wpg'], wb['wpe'], row(ln2_g), row(ln2_b))


def _layer_prompt(x, pe, wb, big_weights, g_q, g_kv, small, *, tm_proj, tq, tk, hg, tm_post):
    s = x.shape[0]
    assert tm_proj == tq
    (qt, k, vt, c, kr, u), cast = _project_prompt(x, wb, g_q, g_kv, tm_proj, tk, *big_weights)
    wb = {**wb, **cast}
    ot = _attend_prompt(qt, k, vt, tq, tk, hg)
    per = tm_post // HALO
    y, clast = _post(x, ot, u, u.reshape(s // HALO, HALO, POOL_WIDTH),
                     lambda i: (jnp.maximum(i * per - 1, 0), 0, 0),
                     pe, jnp.zeros((1, CONV_W - 1, 2 * D_FF), F32),
                     wb, small, nseq=1, seq_len=tm_post, carry=True, o_transposed=True, pos0=0)
    return (y, c, kr.T, u[s - POOL_HIST:], clast[0, CONV_HALO - (CONV_W - 1):]), wb


def _layer_sample(x, pe, c_hist, kr_hist, pool_hist, conv_hist, wb, g_q, g_kv, small):
    nb, t, _ = x.shape
    n_past = c_hist.shape[1]
    xf = x.reshape(nb * t, D_MODEL)
    q, c, kr, u = _project_sample(xf, wb, g_q, g_kv, t, n_past)
    o = _attend_sample(q, c, kr, c_hist, kr_hist, wb, t)
    uprev = jnp.pad(pool_hist, ((0, 0), (HALO - POOL_HIST, 0), (0, 0)))
    y, clast = _post(xf, o, u, uprev, lambda i: (i, 0, 0), pe.reshape(nb * t, PLE_DIM), conv_hist,
                     wb, small, nseq=nb, seq_len=t, carry=False, o_transposed=False, pos0=n_past)
    u3 = u.reshape(nb, t, POOL_WIDTH)
    new_pool = jnp.concatenate([pool_hist, u3], axis=1)[:, -POOL_HIST:]
    return (y.reshape(nb, t, D_MODEL), c.reshape(nb, t, KV_LORA), kr.reshape(nb, t, QK_ROPE), new_pool,
            clast[:, CONV_HALO - (CONV_W - 1):])


def kernel(x_prompt, x_sample, cache_ckv, cache_krope, state_pool, state_ffn_conv, p_prompt, p_sample,
           w_in, g_q, w_q_b, g_kv, w_kv_b, w_pool, s_pool, w_o, ln1_g, ln1_b,
           w_up, w_dw, b_dw, w_down, w_pg, w_pe, ln2_g, ln2_b):
    assert x_prompt.shape[0] == 1 and w_in.shape[0] == DEPTH
    wb = _prep_weights(w_in[0], w_q_b[0], w_kv_b[0], w_pool[0], w_pe[0])
    small = (s_pool[0], ln1_g[0], ln1_b[0], w_dw[0], b_dw[0], ln2_g[0], ln2_b[0])
    s = x_prompt.shape[1]
    tm_proj = min(512, s)
    tq = tk = min(512, s)
    (yp, cp, krp, poolp, convp), wb = _layer_prompt(
        x_prompt[0], p_prompt[0, 0], wb, (w_up[0], w_down[0], w_o[0], w_pg[0]), g_q[0], g_kv[0], small,
        tm_proj=tm_proj, tq=tq, tk=tk, hg=ATTN_HEAD_GROUP, tm_post=min(512, s))
    ys, cs, krs, pools, convs = _layer_sample(x_sample, p_sample[0], cache_ckv[0], cache_krope[0],
                                              state_pool[0], state_ffn_conv[0], wb, g_q[0], g_kv[0], small)
    return (yp[None], ys, cp[None, None], krp[None, None], poolp[None, None], convp[None, None],
            cs[None], krs[None], pools[None], convs[None])
```

```python
import functools
import math

import jax
import jax.numpy as jnp
from jax import lax
from jax.experimental import pallas as pl
from jax.experimental.pallas import tpu as pltpu

F32 = jnp.float32
BF16 = jnp.bfloat16

D_MODEL = 1024
CHUNK = 64
CHUNK_SHIFT = 6
ATTN_HEAD_GROUP = 4
SAMPLE_ATTN_GROUP = 2
N_HEADS = 8
QK_NOPE = 64
QK_ROPE = 32
V_DIM = 64
V_ROWS = V_DIM + 16
Q_LORA = 256
KV_LORA = 256
ROPE_THETA = 10000.0
MLA_WIDTH = N_HEADS * V_DIM
POOL_WINDOWS = (2, 4, 8, 16)
POOL_GROUP_W = 128
POOL_WIDTH = D_MODEL - MLA_WIDTH
POOL_HIST = max(POOL_WINDOWS) - 1
D_FF = 2816
CONV_W = 3
PLE_DIM = 256
DEPTH = 1
ALPHA = (2 * DEPTH) ** 0.25
LN_EPS = 1e-5
RMS_EPS = 1e-6
NEG = -1e30
ATTN_SCALE = 1.0 / math.sqrt(QK_NOPE + QK_ROPE)

LANES = 128
HEAD_PAD = 128
ROPE_LO = QK_NOPE
ROPE_MID = QK_NOPE + QK_ROPE // 2
ROPE_HI = QK_NOPE + QK_ROPE
SHIFT_LANE = ROPE_HI
HALO = 16
CONV_HALO = 8
FF_CHUNK = 256
Q_SCALE = ATTN_SCALE * math.log2(math.e)
VMEM_LIMIT = 56 * 1024 * 1024

NT_DIMS = (((1,), (1,)), ((), ()))


def _rms(x, g):
    return x * lax.rsqrt(jnp.mean(x * x, axis=-1, keepdims=True) + RMS_EPS) * g


def _layer_norm(x, g, b):
    mu = jnp.mean(x, axis=-1, keepdims=True)
    xc = x - mu
    var = jnp.mean(xc * xc, axis=-1, keepdims=True)
    return xc * lax.rsqrt(var + LN_EPS) * g + b


def _token_rope_tables(pos_col, inv_lane):
    ang = pos_col.astype(F32) * inv_lane
    lane = lax.broadcasted_iota(jnp.int32, ang.shape, 1)
    first = lane < ROPE_MID
    sin = jnp.sin(ang)
    return jnp.cos(ang), jnp.where(first, -sin, sin), first


def _token_rope(x, cos, sin_signed, first):
    partner = jnp.where(first, pltpu.roll(x, LANES - QK_ROPE // 2, 1), pltpu.roll(x, QK_ROPE // 2, 1))
    return x * cos + partner * sin_signed


def _proj_prompt_kernel(x_ref, win_ref, gq_ref, gkv_ref, wqt_ref, wuk_ref, wuvt_ref, invs_ref,
                        qt_ref, k_ref, vt_ref, c_ref, kr_ref, u_ref, *, tm, tk, pos0):
    i = pl.program_id(0)
    z = jnp.dot(x_ref[...].astype(BF16), win_ref[...], preferred_element_type=F32)
    q_a = z[:, 0:Q_LORA]
    c_raw = z[:, Q_LORA:Q_LORA + KV_LORA]
    kr_pad = z[:, 512:640]
    u_ref[...] = z[:, 640:640 + POOL_WIDTH]

    qn = _rms(q_a, gq_ref[...]).astype(BF16)
    c = _rms(c_raw, gkv_ref[...])
    c_ref[...] = c
    cb = c.astype(BF16)

    pos_row = pos0 + i * tm + lax.broadcasted_iota(jnp.int32, (QK_ROPE // 2, tm), 1)
    ang = pos_row.astype(F32) * invs_ref[...]
    cos_t, sin_t = jnp.cos(ang), jnp.sin(ang)
    lo, hi = (ROPE_LO, tm), (HEAD_PAD - ROPE_HI, tm)
    cos = jnp.concatenate([jnp.ones(lo, F32), cos_t, cos_t, jnp.ones(hi, F32)], axis=0).T
    sin_signed = jnp.concatenate([jnp.zeros(lo, F32), -sin_t, sin_t, jnp.zeros(hi, F32)], axis=0).T
    first = lax.broadcasted_iota(jnp.int32, (tm, LANES), 1) < ROPE_MID
    kr_rot = _token_rope(kr_pad, cos, sin_signed, first)
    kr_ref[...] = kr_rot.T[ROPE_LO:ROPE_HI, :]

    k_nope = jnp.dot(cb, wuk_ref[...], preferred_element_type=F32)
    k_tail = kr_rot + jnp.where(lax.broadcasted_iota(jnp.int32, (tm, LANES), 1) == SHIFT_LANE, 1.0, 0.0)
    for h in range(N_HEADS):
        sl = slice(h * HEAD_PAD, (h + 1) * HEAD_PAD)
        k_ref[:, sl] = (k_nope[:, sl] + k_tail).astype(BF16)

    vt = lax.dot_general(wuvt_ref[...], cb, NT_DIMS, preferred_element_type=F32)
    ones = jnp.ones((V_ROWS - V_DIM, tk), BF16)
    for s in range(tm // tk):
        for h in range(N_HEADS):
            vt_ref[s, h * V_ROWS:h * V_ROWS + V_DIM, :] = vt[h * V_DIM:(h + 1) * V_DIM, s * tk:(s + 1) * tk].astype(BF16)
            vt_ref[s, h * V_ROWS + V_DIM:(h + 1) * V_ROWS, :] = ones

    qt = lax.dot_general(wqt_ref[...], qn, NT_DIMS, preferred_element_type=F32)
    shift_row = lax.broadcasted_iota(jnp.int32, (HEAD_PAD - SHIFT_LANE, tm), 0) == 0
    for h in range(N_HEADS):
        b0 = h * HEAD_PAD
        qt_ref[b0:b0 + ROPE_LO, :] = (qt[b0:b0 + ROPE_LO] * Q_SCALE).astype(BF16)
        x1 = qt[b0 + ROPE_LO:b0 + ROPE_MID]
        x2 = qt[b0 + ROPE_MID:b0 + ROPE_HI]
        qt_ref[b0 + ROPE_LO:b0 + ROPE_MID, :] = ((x1 * cos_t - x2 * sin_t) * Q_SCALE).astype(BF16)
        qt_ref[b0 + ROPE_MID:b0 + ROPE_HI, :] = ((x2 * cos_t + x1 * sin_t) * Q_SCALE).astype(BF16)
        qt_ref[b0 + ROPE_HI:b0 + HEAD_PAD, :] = jnp.zeros((HEAD_PAD - ROPE_HI, tm), BF16)
        lead = jnp.dot(k_ref[0:CHUNK, b0:b0 + HEAD_PAD], qt_ref[b0:b0 + HEAD_PAD, :],
                       preferred_element_type=F32)
        shift = jnp.max(lead, axis=0, keepdims=True)
        qt_ref[b0 + SHIFT_LANE:b0 + HEAD_PAD, :] = jnp.where(shift_row, -shift, 0.0).astype(BF16)


def _proj_sample_kernel(x_ref, win_ref, gq_ref, gkv_ref, wq_ref, invl_ref,
                        q_ref, c_ref, kr_ref, u_ref, *, tm, seq_len, pos0):
    z = jnp.dot(x_ref[...].astype(BF16), win_ref[...], preferred_element_type=F32)
    q_a = z[:, 0:Q_LORA]
    c_raw = z[:, Q_LORA:Q_LORA + KV_LORA]
    kr_pad = z[:, 512:640]
    u_ref[...] = z[:, 640:640 + POOL_WIDTH]

    qn = _rms(q_a, gq_ref[...]).astype(BF16)
    c_ref[...] = _rms(c_raw, gkv_ref[...])

    row = lax.broadcasted_iota(jnp.int32, (tm, LANES), 0)
    pos_col = pos0 + (row & (seq_len - 1))
    cos, sin_signed, first = _token_rope_tables(pos_col, invl_ref[...])
    kr_ref[...] = _token_rope(kr_pad, cos, sin_signed, first)[:, ROPE_LO:ROPE_HI]

    q = jnp.dot(qn, wq_ref[...], preferred_element_type=F32)
    for h in range(N_HEADS):
        sl = slice(h * HEAD_PAD, (h + 1) * HEAD_PAD)
        q_ref[:, sl] = _token_rope(q[:, sl], cos, sin_signed, first) * Q_SCALE


def _attn_prompt_kernel(qt_ref, qn_ref, k_ref, vt_ref, wup_f32, wdown_f32, wo_f32, wpg_f32,
                        o_ref, wup_bf16, wdown_bf16, wo_bf16, wpg_bf16,
                        pa_ref, pb_ref, sa_ref, sb_ref, m_ref, acc_ref, *, tq, hg, nq):
    i = pl.program_id(1)
    for src, dst in ((wup_f32, wup_bf16), (wdown_f32, wdown_bf16), (wo_f32, wo_bf16), (wpg_f32, wpg_bf16)):
        dst[...] = src[...].astype(BF16)

    def scores(j, h):
        return jnp.dot(k_ref[j, :, h * HEAD_PAD:(h + 1) * HEAD_PAD], qt_ref[h * HEAD_PAD:(h + 1) * HEAD_PAD, :],
                       preferred_element_type=F32)

    kchunk = lax.broadcasted_iota(jnp.int32, (tq, tq), 0) >> CHUNK_SHIFT
    qchunk = lax.broadcasted_iota(jnp.int32, (tq, tq), 1) >> CHUNK_SHIFT
    visible = kchunk <= qchunk

    def values(jv, h, p):
        return jnp.dot(vt_ref[jv, h * V_ROWS:(h + 1) * V_ROWS, :], p, preferred_element_type=F32)

    def to_prob(s):
        return jnp.exp2(s).astype(BF16)

    def fixed_shift_step(cur_ref, jv, issue_next):
        for h in range(hg):
            p = cur_ref[h]
            issue_next(h)
            acc_ref[h] += values(jv, h, p)

    def running_max_step(cur_ref, jv, issue_next):
        for h in range(hg):
            s = cur_ref[h]
            m_old = m_ref[h]
            m_new = jnp.maximum(m_old, jnp.max(s, axis=0, keepdims=True))
            alpha = jnp.exp2(m_old - m_new)
            p = jnp.exp2(s - m_new).astype(BF16)
            m_ref[h] = m_new
            issue_next(h)
            acc_ref[h] = alpha * acc_ref[h] + values(jv, h, p)

    def load_diagonal(bank_a, encode):
        for h in range(hg):
            bank_a[h] = encode(jnp.where(visible, scores(i, h), NEG))

    def sweep(step, bank_a, bank_b, encode, pairs_per_trip):
        def block_into(bank_ref, j):
            def issue(h):
                bank_ref[h] = encode(scores(j, h))
            return issue

        def next_diagonal_into_a(h):
            j = jnp.minimum(i + 1, nq - 1)
            s = jnp.dot(k_ref[j, :, h * HEAD_PAD:(h + 1) * HEAD_PAD], qn_ref[h * HEAD_PAD:(h + 1) * HEAD_PAD, :],
                        preferred_element_type=F32)
            bank_a[h] = encode(jnp.where(visible, s, NEG))

        def pair(t):
            step(bank_a, jnp.where(t == 0, i, 2 * t - 1), block_into(bank_b, 2 * t))
            step(bank_b, 2 * t, block_into(bank_a, 2 * t + 1))

        n_pairs = i // 2
        n_trips = n_pairs // pairs_per_trip

        def trip(u, c):
            for r in range(pairs_per_trip):
                pair(pairs_per_trip * u + r)
            return c

        lax.fori_loop(0, n_trips, trip, 0)
        for r in range(pairs_per_trip - 1):
            pl.when(n_pairs - n_trips * pairs_per_trip > r)(
                functools.partial(pair, n_trips * pairs_per_trip + r))

        @pl.when(i % 2 == 0)
        def _():
            step(bank_a, jnp.maximum(i - 1, 0), next_diagonal_into_a)

        @pl.when(i % 2 == 1)
        def _():
            step(bank_a, jnp.where(i == 1, i, i - 2), block_into(bank_b, i - 1))
            step(bank_b, i - 1, next_diagonal_into_a)

    acc_ref[...] = jnp.zeros(acc_ref.shape, F32)
    pl.when(i == 0)(functools.partial(load_diagonal, pa_ref, to_prob))
    sweep(fixed_shift_step, pa_ref, pb_ref, to_prob, pairs_per_trip=2)

    bad = jnp.where(jnp.isfinite(acc_ref[...]), 0.0, 1.0)
    bad = jnp.max(jnp.max(jnp.max(bad, axis=0), axis=0, keepdims=True), axis=1, keepdims=True)

    @pl.when(bad[0, 0] > 0.0)
    def _():
        m_ref[...] = jnp.full(m_ref.shape, NEG, F32)
        acc_ref[...] = jnp.zeros(acc_ref.shape, F32)
        load_diagonal(sa_ref, lambda s: s)
        sweep(running_max_step, sa_ref, sb_ref, lambda s: s, pairs_per_trip=1)
        for h in range(hg):
            pa_ref[h] = to_prob(sa_ref[h])

    for h in range(hg):
        o_ref[h * V_DIM:(h + 1) * V_DIM, :] = acc_ref[h, 0:V_DIM, :] / acc_ref[h, V_DIM:V_DIM + 1, :]


def _attn_sample_kernel(q_ref, cn_ref, krn_ref, ch_ref, krht_ref, wukt_ref, wuvbd_ref, o_ref, *, t, n_past, nbs):
    rows = N_HEADS * t
    qchunk = (n_past + (lax.broadcasted_iota(jnp.int32, (rows, 1), 0) & (t - 1))) >> CHUNK_SHIFT
    seen_hist = (lax.broadcasted_iota(jnp.int32, (rows, n_past), 1) >> CHUNK_SHIFT) <= qchunk
    seen_new = ((n_past + lax.broadcasted_iota(jnp.int32, (rows, t), 1)) >> CHUNK_SHIFT) <= qchunk
    elements = range(nbs)

    def queries(e):
        q = q_ref[e * t:(e + 1) * t, :]
        qlat, qrope = [], []
        for h in range(N_HEADS):
            b0 = h * HEAD_PAD
            qn = q[:, b0:b0 + QK_NOPE].astype(BF16)
            qlat.append(jnp.dot(qn, wukt_ref[h], preferred_element_type=F32))
            qrope.append(q[:, b0 + ROPE_LO:b0 + ROPE_HI])
        return (jnp.concatenate(qlat, axis=0).astype(BF16),
                jnp.concatenate(qrope, axis=0).astype(BF16))

    def keys(e):
        return (ch_ref[e].astype(BF16),
                krht_ref[e].astype(BF16),
                cn_ref[e * t:(e + 1) * t, :].astype(BF16),
                krn_ref[e * t:(e + 1) * t, :].astype(BF16))

    def scores(qs, ks):
        (ql, qr), (chb, krhb, cnb, krnb) = qs, ks
        s_h = (lax.dot_general(ql, chb, NT_DIMS, preferred_element_type=F32)
               + jnp.dot(qr, krhb, preferred_element_type=F32))
        s_n = (lax.dot_general(ql, cnb, NT_DIMS, preferred_element_type=F32)
               + lax.dot_general(qr, krnb, NT_DIMS, preferred_element_type=F32))
        return jnp.where(seen_hist, s_h, NEG), jnp.where(seen_new, s_n, NEG)

    def softmax(s):
        s_h, s_n = s
        m = jnp.maximum(jnp.max(s_h, axis=-1, keepdims=True), jnp.max(s_n, axis=-1, keepdims=True))
        p_h = jnp.exp2(s_h - m)
        p_n = jnp.exp2(s_n - m)
        l = jnp.sum(p_h, axis=-1, keepdims=True) + jnp.sum(p_n, axis=-1, keepdims=True)
        return p_h.astype(BF16), p_n.astype(BF16), l

    def latent_out(p, ks):
        (p_h, p_n, l), (chb, _, cnb, _) = p, ks
        olat = (jnp.dot(p_h, chb, preferred_element_type=F32) + jnp.dot(p_n, cnb, preferred_element_type=F32)) / l
        return jnp.concatenate([olat[h * t:(h + 1) * t] for h in range(N_HEADS)], axis=1).astype(BF16)

    qs = [queries(e) for e in elements]
    ks = [keys(e) for e in elements]
    ss = [scores(qs[e], ks[e]) for e in elements]
    ps = [softmax(ss[e]) for e in elements]
    wide = jnp.concatenate([latent_out(ps[e], ks[e]) for e in elements], axis=0)
    o_ref[...] = jnp.dot(wide, wuvbd_ref[...], preferred_element_type=F32)


def _post_kernel(x_ref, o_ref, u_ref, uprev_ref, pe_ref, chist_ref,
                 wpool_ref, spool_ref, wo_ref, ln1g_ref, ln1b_ref, wup_ref, wdw_ref, bdw_ref, wdown_ref,
                 wpg_ref, wpe_ref, ln2g_ref, ln2b_ref,
                 y_ref, clast_ref,
                 ubuf, convbuf, hbuf, cbuf, *, nseq, seq_len, carry, o_transposed, pos0):
    i = pl.program_id(0)
    tm = nseq * seq_len
    hist_rows = slice(CONV_HALO - (CONV_W - 1), CONV_HALO)

    def init_conv_history():
        cbuf[...] = jnp.zeros_like(cbuf)
        cbuf[:, hist_rows, :] = chist_ref[...]

    if carry:
        pl.when(i == 0)(init_conv_history)
    else:
        init_conv_history()

    def gather(buf, halo, shift, cols):
        stride = halo + seq_len
        pieces = [buf[q * stride + halo - shift:q * stride + halo - shift + seq_len, cols] for q in range(nseq)]
        return pieces[0] if nseq == 1 else jnp.concatenate(pieces, axis=0)

    attn = o_ref[...].T if o_transposed else o_ref[...]
    mix_attn = jnp.dot(attn.astype(BF16), wo_ref[0:MLA_WIDTH, :], preferred_element_type=F32)
    pe_proj = jnp.dot(pe_ref[...].astype(BF16), wpe_ref[...], preferred_element_type=F32)

    u = u_ref[...]
    for q in range(nseq):
        uprev = uprev_ref[q]
        if carry:
            uprev = jnp.where(i > 0, uprev, 0.0)
        ubuf[q * (HALO + seq_len):q * (HALO + seq_len) + HALO, :] = uprev
        ubuf[q * (HALO + seq_len) + HALO:(q + 1) * (HALO + seq_len), :] = u[q * seq_len:(q + 1) * seq_len]
    row = lax.broadcasted_iota(jnp.int32, (tm, POOL_GROUP_W), 0)
    frame = row + i * tm if carry else row & (seq_len - 1)
    pooled = []
    for g, w in enumerate(POOL_WINDOWS):
        cols = slice(g * POOL_GROUP_W, (g + 1) * POOL_GROUP_W)
        win = u[:, cols]
        for k in range(1, w):
            win = win + gather(ubuf, HALO, k, cols)
        cnt = jnp.minimum(w, pos0 + frame + 1).astype(F32)
        d = win / cnt - u[:, cols]
        yg = jnp.dot(d.astype(BF16), wpool_ref[g], preferred_element_type=F32) * spool_ref[:, cols]
        pooled.append(yg.astype(BF16))
    pooled = jnp.concatenate(pooled, axis=1)

    mix = mix_attn + jnp.dot(pooled, wo_ref[MLA_WIDTH:D_MODEL, :], preferred_element_type=F32)
    x1 = _layer_norm(ALPHA * x_ref[...] + mix, ln1g_ref[...], ln1b_ref[...])
    x1b = x1.astype(BF16)

    def conv_chunk(cols, buf):
        up = jnp.dot(x1b, wup_ref[:, cols], preferred_element_type=F32)
        if nseq == 1:
            hist = cbuf[0, :, cols]
            cbuf[0, :, cols] = up[tm - CONV_HALO:tm]
            top = lax.broadcasted_iota(jnp.int32, (CONV_HALO, FF_CHUNK), 0)

            def shifted(k):
                rolled = pltpu.roll(up, k, 0)
                head = jnp.where(top < k, pltpu.roll(hist, k, 0), rolled[0:CONV_HALO])
                return jnp.concatenate([head, rolled[CONV_HALO:]], axis=0)

            return (shifted(2) * wdw_ref[0:1, cols] + shifted(1) * wdw_ref[1:2, cols]
                    + up * wdw_ref[2:3, cols] + bdw_ref[:, cols])
        stride = CONV_HALO + seq_len
        for q in range(nseq):
            buf[q * stride:q * stride + CONV_HALO, :] = cbuf[q, :, cols]
            buf[q * stride + CONV_HALO:(q + 1) * stride, :] = up[q * seq_len:(q + 1) * seq_len]
            cbuf[q, :, cols] = buf[q * stride + seq_len:(q + 1) * stride, :]
        everything = slice(None)
        return (gather(buf, CONV_HALO, 2, everything) * wdw_ref[0:1, cols]
                + gather(buf, CONV_HALO, 1, everything) * wdw_ref[1:2, cols]
                + up * wdw_ref[2:3, cols] + bdw_ref[:, cols])

    for j in range(D_FF // FF_CHUNK):
        a = conv_chunk(slice(j * FF_CHUNK, (j + 1) * FF_CHUNK), convbuf.at[2 * j])
        b = conv_chunk(slice(D_FF + j * FF_CHUNK, D_FF + (j + 1) * FF_CHUNK), convbuf.at[2 * j + 1])
        hbuf[:, j * FF_CHUNK:(j + 1) * FF_CHUNK] = (a * jax.nn.sigmoid(a) * b).astype(BF16)
    ffn = jnp.dot(hbuf[...], wdown_ref[...], preferred_element_type=F32)
    clast_ref[...] = cbuf[...]

    ple = jax.nn.sigmoid(jnp.dot(x1b, wpg_ref[...], preferred_element_type=F32)) * pe_proj
    y_ref[...] = _layer_norm(ALPHA * x1 + ffn + ple, ln2g_ref[...], ln2b_ref[...])


def _const_spec(shape):
    nd = len(shape)
    return pl.BlockSpec(shape, lambda *_: (0,) * nd)


def _params(semantics):
    return pltpu.CompilerParams(dimension_semantics=semantics, vmem_limit_bytes=VMEM_LIMIT)


def _pad_heads(w, per_head, used):
    k = w.shape[0]
    w = w.reshape(k, N_HEADS, per_head)[:, :, :used]
    return jnp.pad(w, ((0, 0), (0, 0), (0, HEAD_PAD - used))).reshape(k, N_HEADS * HEAD_PAD)


def _prep_weights(w_in, w_q_b, w_kv_b, w_pool, w_pe):
    w_kr = jnp.pad(w_in[:, 512:512 + QK_ROPE], ((0, 0), (ROPE_LO, LANES - ROPE_HI)))
    win = jnp.concatenate([w_in[:, :512], w_kr, w_in[:, 512 + QK_ROPE:]], axis=1).astype(BF16)
    wq_pad = _pad_heads(w_q_b, QK_NOPE + QK_ROPE, QK_NOPE + QK_ROPE).astype(BF16)
    w_kv = w_kv_b.reshape(KV_LORA, N_HEADS, QK_NOPE + V_DIM)
    w_uk, w_uv = w_kv[..., :QK_NOPE], w_kv[..., QK_NOPE:]
    wuk_pad = jnp.pad(w_uk, ((0, 0), (0, 0), (0, HEAD_PAD - QK_NOPE))).reshape(KV_LORA, -1).astype(BF16)
    wuvt = w_uv.reshape(KV_LORA, N_HEADS * V_DIM).T.astype(BF16)
    wukt = jnp.transpose(w_uk, (1, 2, 0)).astype(BF16)
    eye = jnp.eye(N_HEADS, dtype=w_uv.dtype)
    wuv_bd = jnp.einsum('lhv,hg->hlgv', w_uv, eye).reshape(N_HEADS * KV_LORA, N_HEADS * V_DIM).astype(BF16)
    return dict(win=win, wq_pad=wq_pad, wqt=wq_pad.T, wuk_pad=wuk_pad, wuvt=wuvt, wukt=wukt, wuv_bd=wuv_bd,
                wpool=w_pool.astype(BF16), wpe=w_pe.astype(BF16))


def _rope_inv():
    inv = 1.0 / (ROPE_THETA ** (jnp.arange(0, QK_ROPE, 2, dtype=F32) / QK_ROPE))
    inv_lane = jnp.zeros((1, LANES), F32).at[0, ROPE_LO:ROPE_HI].set(jnp.concatenate([inv, inv]))
    return inv_lane, inv[:, None]


def _project_prompt(x, wb, g_q, g_kv, tm, tk):
    s = x.shape[0]
    _, inv_sub = _rope_inv()
    n = s // tm
    outs = pl.pallas_call(
        functools.partial(_proj_prompt_kernel, tm=tm, tk=tk, pos0=0),
        grid=(n,),
        in_specs=[pl.BlockSpec((tm, D_MODEL), lambda i: (i, 0)),
                  _const_spec(wb['win'].shape), _const_spec((1, Q_LORA)), _const_spec((1, KV_LORA)),
                  _const_spec(wb['wqt'].shape), _const_spec(wb['wuk_pad'].shape), _const_spec(wb['wuvt'].shape),
                  _const_spec((QK_ROPE // 2, 1))],
        out_specs=[pl.BlockSpec((N_HEADS * HEAD_PAD, tm), lambda i: (0, i)),
                   pl.BlockSpec((tm, N_HEADS * HEAD_PAD), lambda i: (i, 0)),
                   pl.BlockSpec((tm // tk, N_HEADS * V_ROWS, tk), lambda i: (i, 0, 0)),
                   pl.BlockSpec((tm, KV_LORA), lambda i: (i, 0)),
                   pl.BlockSpec((QK_ROPE, tm), lambda i: (0, i)),
                   pl.BlockSpec((tm, POOL_WIDTH), lambda i: (i, 0))],
        out_shape=[jax.ShapeDtypeStruct((N_HEADS * HEAD_PAD, s), BF16),
                   jax.ShapeDtypeStruct((s, N_HEADS * HEAD_PAD), BF16),
                   jax.ShapeDtypeStruct((s // tk, N_HEADS * V_ROWS, tk), BF16),
                   jax.ShapeDtypeStruct((s, KV_LORA), F32),
                   jax.ShapeDtypeStruct((QK_ROPE, s), F32),
                   jax.ShapeDtypeStruct((s, POOL_WIDTH), F32)],
        compiler_params=_params(("arbitrary",)),
        name="proj_prompt",
    )(x, wb['win'], g_q.reshape(1, -1), g_kv.reshape(1, -1), wb['wqt'], wb['wuk_pad'], wb['wuvt'], inv_sub)
    return outs


def _project_sample(x, wb, g_q, g_kv, seq_len, pos0):
    rows = x.shape[0]
    assert seq_len & (seq_len - 1) == 0
    inv_lane, _ = _rope_inv()
    return pl.pallas_call(
        functools.partial(_proj_sample_kernel, tm=rows, seq_len=seq_len, pos0=pos0),
        grid=(1,),
        in_specs=[_const_spec((rows, D_MODEL)), _const_spec(wb['win'].shape), _const_spec((1, Q_LORA)),
                  _const_spec((1, KV_LORA)), _const_spec(wb['wq_pad'].shape), _const_spec((1, LANES))],
        out_specs=[_const_spec((rows, N_HEADS * HEAD_PAD)), _const_spec((rows, KV_LORA)),
                   _const_spec((rows, QK_ROPE)), _const_spec((rows, POOL_WIDTH))],
        out_shape=[jax.ShapeDtypeStruct((rows, N_HEADS * HEAD_PAD), F32),
                   jax.ShapeDtypeStruct((rows, KV_LORA), F32),
                   jax.ShapeDtypeStruct((rows, QK_ROPE), F32),
                   jax.ShapeDtypeStruct((rows, POOL_WIDTH), F32)],
        compiler_params=_params(("arbitrary",)),
        name="proj_sample",
    )(x, wb['win'], g_q.reshape(1, -1), g_kv.reshape(1, -1), wb['wq_pad'], inv_lane)


def _attend_prompt(qt, k, vt, tq, tk, hg, w_up, w_down, w_o, w_pg):
    s = k.shape[0]
    nkv = s // tk
    k3 = k.reshape(nkv, tk, N_HEADS * HEAD_PAD)
    resident = pl.Buffered(1)
    assert tq == tk
    nq = s // tq
    casts = [w_up, w_down, w_o, w_pg]
    steps = (N_HEADS // hg) * nq

    def slab(w):
        rows = w.shape[0]
        per = next(r for r in range(16 * pl.cdiv(rows, 16 * steps), rows + 1, 16) if rows % r == 0)
        return pl.BlockSpec((per, w.shape[1]), lambda g, i: (jnp.minimum(g * nq + i, rows // per - 1), 0))

    outs = pl.pallas_call(
        functools.partial(_attn_prompt_kernel, tq=tq, hg=hg, nq=nq),
        grid=(N_HEADS // hg, nq),
        in_specs=[pl.BlockSpec((hg * HEAD_PAD, tq), lambda g, i: (g, i)),
                  pl.BlockSpec((hg * HEAD_PAD, tq), lambda g, i: (g, jnp.minimum(i + 1, nq - 1))),
                  pl.BlockSpec((nkv, tk, hg * HEAD_PAD), lambda g, i: (0, 0, g), pipeline_mode=resident),
                  pl.BlockSpec((nkv, hg * V_ROWS, tk), lambda g, i: (0, g, 0), pipeline_mode=resident)]
                 + [slab(w) for w in casts],
        out_specs=[pl.BlockSpec((hg * V_DIM, tq), lambda g, i: (g, i))] + [slab(w) for w in casts],
        out_shape=[jax.ShapeDtypeStruct((MLA_WIDTH, s), F32)] + [jax.ShapeDtypeStruct(w.shape, BF16) for w in casts],
        scratch_shapes=[pltpu.VMEM((hg, tk, tq), BF16), pltpu.VMEM((hg, tk, tq), BF16),
                        pltpu.VMEM((hg, tk, tq), F32), pltpu.VMEM((hg, tk, tq), F32),
                        pltpu.VMEM((hg, 1, tq), F32), pltpu.VMEM((hg, V_ROWS, tq), F32)],
        compiler_params=_params(("arbitrary", "arbitrary")),
        name="attn_prompt",
    )(qt, qt, k3, vt, *casts)
    return outs[0], dict(zip(('wup', 'wdown', 'wo', 'wpg'), outs[1:]))


def _attend_sample(q, c_new, kr_new, c_hist, kr_hist, wb, t):
    nb, n_past, _ = c_hist.shape
    assert t & (t - 1) == 0
    nbs = SAMPLE_ATTN_GROUP if nb % SAMPLE_ATTN_GROUP == 0 else 1
    return pl.pallas_call(
        functools.partial(_attn_sample_kernel, t=t, n_past=n_past, nbs=nbs),
        grid=(nb // nbs,),
        in_specs=[pl.BlockSpec((nbs * t, N_HEADS * HEAD_PAD), lambda b: (b, 0)),
                  pl.BlockSpec((nbs * t, KV_LORA), lambda b: (b, 0)),
                  pl.BlockSpec((nbs * t, QK_ROPE), lambda b: (b, 0)),
                  pl.BlockSpec((nbs, n_past, KV_LORA), lambda b: (b, 0, 0)),
                  pl.BlockSpec((nbs, QK_ROPE, n_past), lambda b: (b, 0, 0)),
                  _const_spec(wb['wukt'].shape), _const_spec(wb['wuv_bd'].shape)],
        out_specs=pl.BlockSpec((nbs * t, MLA_WIDTH), lambda b: (b, 0)),
        out_shape=jax.ShapeDtypeStruct((nb * t, MLA_WIDTH), F32),
        compiler_params=_params(("arbitrary",)),
        name="attn_sample",
    )(q, c_new, kr_new, c_hist, jnp.swapaxes(kr_hist, 1, 2), wb['wukt'], wb['wuv_bd'])


def _post(x, o, u, uprev, uprev_map, pe, chist, wb, small, *, nseq, seq_len, carry, o_transposed, pos0):
    rows = x.shape[0]
    tm = nseq * seq_len
    n = rows // tm
    assert nseq == 1 if carry else seq_len & (seq_len - 1) == 0
    s_pool, ln1_g, ln1_b, w_dw, b_dw, ln2_g, ln2_b = small
    o_spec = (pl.BlockSpec((MLA_WIDTH, tm), lambda i: (0, i)) if o_transposed
              else pl.BlockSpec((tm, MLA_WIDTH), lambda i: (i, 0)))
    per_tile = (lambda i: (0, 0, 0)) if carry else (lambda i: (i, 0, 0))
    row = lambda v: v.reshape(1, -1)
    return pl.pallas_call(
        functools.partial(_post_kernel, nseq=nseq, seq_len=seq_len, carry=carry, o_transposed=o_transposed,
                          pos0=pos0),
        grid=(n,),
        in_specs=[pl.BlockSpec((tm, D_MODEL), lambda i: (i, 0)),
                  o_spec,
                  pl.BlockSpec((tm, POOL_WIDTH), lambda i: (i, 0)),
                  pl.BlockSpec((nseq, HALO, POOL_WIDTH), uprev_map),
                  pl.BlockSpec((tm, PLE_DIM), lambda i: (i, 0)),
                  pl.BlockSpec((nseq, CONV_W - 1, 2 * D_FF), per_tile),
                  _const_spec(wb['wpool'].shape), _const_spec((1, POOL_WIDTH)), _const_spec(wb['wo'].shape),
                  _const_spec((1, D_MODEL)), _const_spec((1, D_MODEL)),
                  _const_spec(wb['wup'].shape), _const_spec((CONV_W, 2 * D_FF)), _const_spec((1, 2 * D_FF)),
                  _const_spec(wb['wdown'].shape), _const_spec(wb['wpg'].shape), _const_spec(wb['wpe'].shape),
                  _const_spec((1, D_MODEL)), _const_spec((1, D_MODEL))],
        out_specs=[pl.BlockSpec((tm, D_MODEL), lambda i: (i, 0)),
                   pl.BlockSpec((nseq, CONV_HALO, 2 * D_FF), per_tile)],
        out_shape=[jax.ShapeDtypeStruct((rows, D_MODEL), F32),
                   jax.ShapeDtypeStruct((nseq if carry else n * nseq, CONV_HALO, 2 * D_FF), F32)],
        scratch_shapes=[pltpu.VMEM((nseq * (seq_len + HALO), POOL_WIDTH), F32),
                        pltpu.VMEM((2 * (D_FF // FF_CHUNK), nseq * (seq_len + CONV_HALO) if nseq > 1 else CONV_HALO,
                                    FF_CHUNK), F32),
                        pltpu.VMEM((tm, D_FF), BF16),
                        pltpu.VMEM((nseq, CONV_HALO, 2 * D_FF), F32)],
        compiler_params=_params(("arbitrary",)),
        name="post_prompt" if carry else "post_sample",
    )(x, o, u, uprev, pe, chist,
      wb['wpool'], row(s_pool), wb['wo'], row(ln1_g), row(ln1_b), wb['wup'], w_dw, row(b_dw), wb['wdown'],
      wb['wpg'], wb['wpe'], row(ln2_g), row(ln2_b))


def _layer_prompt(x, pe, wb, big_weights, g_q, g_kv, small, *, tm_proj, tq, tk, hg, tm_post):
    s = x.shape[0]
    assert tm_proj == tq
    qt, k, vt, c, kr, u = _project_prompt(x, wb, g_q, g_kv, tm_proj, tk)
    ot, cast = _attend_prompt(qt, k, vt, tq, tk, hg, *big_weights)
    wb = {**wb, **cast}
    per = tm_post // HALO
    y, clast = _post(x, ot, u, u.reshape(s // HALO, HALO, POOL_WIDTH),
                     lambda i: (jnp.maximum(i * per - 1, 0), 0, 0),
                     pe, jnp.zeros((1, CONV_W - 1, 2 * D_FF), F32),
                     wb, small, nseq=1, seq_len=tm_post, carry=True, o_transposed=True, pos0=0)
    return (y, c, kr.T, u[s - POOL_HIST:], clast[0, CONV_HALO - (CONV_W - 1):]), wb


def _layer_sample(x, pe, c_hist, kr_hist, pool_hist, conv_hist, wb, g_q, g_kv, small):
    nb, t, _ = x.shape
    n_past = c_hist.shape[1]
    xf = x.reshape(nb * t, D_MODEL)
    q, c, kr, u = _project_sample(xf, wb, g_q, g_kv, t, n_past)
    o = _attend_sample(q, c, kr, c_hist, kr_hist, wb, t)
    uprev = jnp.pad(pool_hist, ((0, 0), (HALO - POOL_HIST, 0), (0, 0)))
    y, clast = _post(xf, o, u, uprev, lambda i: (i, 0, 0), pe.reshape(nb * t, PLE_DIM), conv_hist,
                     wb, small, nseq=nb, seq_len=t, carry=False, o_transposed=False, pos0=n_past)
    u3 = u.reshape(nb, t, POOL_WIDTH)
    new_pool = jnp.concatenate([pool_hist, u3], axis=1)[:, -POOL_HIST:]
    return (y.reshape(nb, t, D_MODEL), c.reshape(nb, t, KV_LORA), kr.reshape(nb, t, QK_ROPE), new_pool,
            clast[:, CONV_HALO - (CONV_W - 1):])


def kernel(x_prompt, x_sample, cache_ckv, cache_krope, state_pool, state_ffn_conv, p_prompt, p_sample,
           w_in, g_q, w_q_b, g_kv, w_kv_b, w_pool, s_pool, w_o, ln1_g, ln1_b,
           w_up, w_dw, b_dw, w_down, w_pg, w_pe, ln2_g, ln2_b):
    assert x_prompt.shape[0] == 1 and w_in.shape[0] == DEPTH
    wb = _prep_weights(w_in[0], w_q_b[0], w_kv_b[0], w_pool[0], w_pe[0])
    small = (s_pool[0], ln1_g[0], ln1_b[0], w_dw[0], b_dw[0], ln2_g[0], ln2_b[0])
    s = x_prompt.shape[1]
    tm_proj = min(512, s)
    tq = tk = min(512, s)
    (yp, cp, krp, poolp, convp), wb = _layer_prompt(
        x_prompt[0], p_prompt[0, 0], wb, (w_up[0], w_down[0], w_o[0], w_pg[0]), g_q[0], g_kv[0], small,
        tm_proj=tm_proj, tq=tq, tk=tk, hg=ATTN_HEAD_GROUP, tm_post=min(512, s))
    ys, cs, krs, pools, convs = _layer_sample(x_sample, p_sample[0], cache_ckv[0], cache_krope[0],
                                              state_pool[0], state_ffn_conv[0], wb, g_q[0], g_kv[0], small)
    return (yp[None], ys, cp[None, None], krp[None, None], poolp[None, None], convp[None, None],
            cs[None], krs[None], pools[None], convs[None])
```

```python
import functools
import math

import jax
import jax.numpy as jnp
from jax import lax
from jax.experimental import pallas as pl
from jax.experimental.pallas import tpu as pltpu

F32 = jnp.float32
BF16 = jnp.bfloat16

D_MODEL = 1024
CHUNK = 64
CHUNK_SHIFT = 6
ATTN_HEAD_GROUP = 4
SAMPLE_ATTN_GROUP = 4
N_HEADS = 8
QK_NOPE = 64
QK_ROPE = 32
V_DIM = 64
V_ROWS = V_DIM + 16
Q_LORA = 256
KV_LORA = 256
ROPE_THETA = 10000.0
MLA_WIDTH = N_HEADS * V_DIM
POOL_WINDOWS = (2, 4, 8, 16)
POOL_GROUP_W = 128
POOL_WIDTH = D_MODEL - MLA_WIDTH
POOL_HIST = max(POOL_WINDOWS) - 1
D_FF = 2816
CONV_W = 3
PLE_DIM = 256
DEPTH = 1
ALPHA = (2 * DEPTH) ** 0.25
LN_EPS = 1e-5
RMS_EPS = 1e-6
NEG = -1e30
ATTN_SCALE = 1.0 / math.sqrt(QK_NOPE + QK_ROPE)

LANES = 128
HEAD_PAD = 128
ROPE_LO = QK_NOPE
ROPE_MID = QK_NOPE + QK_ROPE // 2
ROPE_HI = QK_NOPE + QK_ROPE
SHIFT_LANE = ROPE_HI
HALO = 16
CONV_HALO = 8
FF_CHUNK = 256
Q_SCALE = ATTN_SCALE * math.log2(math.e)
VMEM_LIMIT = 56 * 1024 * 1024

NT_DIMS = (((1,), (1,)), ((), ()))


def _rms(x, g):
    return x * lax.rsqrt(jnp.mean(x * x, axis=-1, keepdims=True) + RMS_EPS) * g


def _layer_norm(x, g, b):
    mu = jnp.mean(x, axis=-1, keepdims=True)
    xc = x - mu
    var = jnp.mean(xc * xc, axis=-1, keepdims=True)
    return xc * lax.rsqrt(var + LN_EPS) * g + b


def _token_rope_tables(pos_col, inv_lane):
    ang = pos_col.astype(F32) * inv_lane
    lane = lax.broadcasted_iota(jnp.int32, ang.shape, 1)
    first = lane < ROPE_MID
    sin = jnp.sin(ang)
    return jnp.cos(ang), jnp.where(first, -sin, sin), first


def _token_rope(x, cos, sin_signed, first):
    partner = jnp.where(first, pltpu.roll(x, LANES - QK_ROPE // 2, 1), pltpu.roll(x, QK_ROPE // 2, 1))
    return x * cos + partner * sin_signed


def _proj_prompt_kernel(x_ref, win_ref, gq_ref, gkv_ref, wqt_ref, wuk_ref, wuvt_ref, invs_ref,
                        qt_ref, k_ref, vt_ref, c_ref, kr_ref, u_ref, *, tm, tk, pos0):
    i = pl.program_id(0)
    z = jnp.dot(x_ref[...].astype(BF16), win_ref[...], preferred_element_type=F32)
    q_a = z[:, 0:Q_LORA]
    c_raw = z[:, Q_LORA:Q_LORA + KV_LORA]
    kr_pad = z[:, 512:640]
    u_ref[...] = z[:, 640:640 + POOL_WIDTH]

    qn = _rms(q_a, gq_ref[...]).astype(BF16)
    c = _rms(c_raw, gkv_ref[...])
    c_ref[...] = c
    cb = c.astype(BF16)

    pos_row = pos0 + i * tm + lax.broadcasted_iota(jnp.int32, (QK_ROPE // 2, tm), 1)
    ang = pos_row.astype(F32) * invs_ref[...]
    cos_t, sin_t = jnp.cos(ang), jnp.sin(ang)
    lo, hi = (ROPE_LO, tm), (HEAD_PAD - ROPE_HI, tm)
    cos = jnp.concatenate([jnp.ones(lo, F32), cos_t, cos_t, jnp.ones(hi, F32)], axis=0).T
    sin_signed = jnp.concatenate([jnp.zeros(lo, F32), -sin_t, sin_t, jnp.zeros(hi, F32)], axis=0).T
    first = lax.broadcasted_iota(jnp.int32, (tm, LANES), 1) < ROPE_MID
    kr_rot = _token_rope(kr_pad, cos, sin_signed, first)
    kr_ref[...] = kr_rot.T[ROPE_LO:ROPE_HI, :]

    k_nope = jnp.dot(cb, wuk_ref[...], preferred_element_type=F32)
    k_tail = kr_rot + jnp.where(lax.broadcasted_iota(jnp.int32, (tm, LANES), 1) == SHIFT_LANE, 1.0, 0.0)
    for h in range(N_HEADS):
        sl = slice(h * HEAD_PAD, (h + 1) * HEAD_PAD)
        k_ref[:, sl] = (k_nope[:, sl] + k_tail).astype(BF16)

    vt = lax.dot_general(wuvt_ref[...], cb, NT_DIMS, preferred_element_type=F32)
    ones = jnp.ones((V_ROWS - V_DIM, tk), BF16)
    for s in range(tm // tk):
        for h in range(N_HEADS):
            vt_ref[s, h * V_ROWS:h * V_ROWS + V_DIM, :] = vt[h * V_DIM:(h + 1) * V_DIM, s * tk:(s + 1) * tk].astype(BF16)
            vt_ref[s, h * V_ROWS + V_DIM:(h + 1) * V_ROWS, :] = ones

    qt = lax.dot_general(wqt_ref[...], qn, NT_DIMS, preferred_element_type=F32)
    shift_row = lax.broadcasted_iota(jnp.int32, (HEAD_PAD - SHIFT_LANE, tm), 0) == 0
    for h in range(N_HEADS):
        b0 = h * HEAD_PAD
        qt_ref[b0:b0 + ROPE_LO, :] = (qt[b0:b0 + ROPE_LO] * Q_SCALE).astype(BF16)
        x1 = qt[b0 + ROPE_LO:b0 + ROPE_MID]
        x2 = qt[b0 + ROPE_MID:b0 + ROPE_HI]
        qt_ref[b0 + ROPE_LO:b0 + ROPE_MID, :] = ((x1 * cos_t - x2 * sin_t) * Q_SCALE).astype(BF16)
        qt_ref[b0 + ROPE_MID:b0 + ROPE_HI, :] = ((x2 * cos_t + x1 * sin_t) * Q_SCALE).astype(BF16)
        qt_ref[b0 + ROPE_HI:b0 + HEAD_PAD, :] = jnp.zeros((HEAD_PAD - ROPE_HI, tm), BF16)
        lead = jnp.dot(k_ref[0:CHUNK, b0:b0 + HEAD_PAD], qt_ref[b0:b0 + HEAD_PAD, :],
                       preferred_element_type=F32)
        shift = jnp.max(lead, axis=0, keepdims=True)
        qt_ref[b0 + SHIFT_LANE:b0 + HEAD_PAD, :] = jnp.where(shift_row, -shift, 0.0).astype(BF16)


def _proj_sample_kernel(x_ref, win_ref, gq_ref, gkv_ref, wq_ref, invl_ref,
                        q_ref, c_ref, kr_ref, u_ref, *, tm, seq_len, pos0):
    z = jnp.dot(x_ref[...].astype(BF16), win_ref[...], preferred_element_type=F32)
    q_a = z[:, 0:Q_LORA]
    c_raw = z[:, Q_LORA:Q_LORA + KV_LORA]
    kr_pad = z[:, 512:640]
    u_ref[...] = z[:, 640:640 + POOL_WIDTH]

    qn = _rms(q_a, gq_ref[...]).astype(BF16)
    c_ref[...] = _rms(c_raw, gkv_ref[...])

    row = lax.broadcasted_iota(jnp.int32, (tm, LANES), 0)
    pos_col = pos0 + (row & (seq_len - 1))
    cos, sin_signed, first = _token_rope_tables(pos_col, invl_ref[...])
    kr_ref[...] = _token_rope(kr_pad, cos, sin_signed, first)[:, ROPE_LO:ROPE_HI]

    q = jnp.dot(qn, wq_ref[...], preferred_element_type=F32)
    for h in range(N_HEADS):
        sl = slice(h * HEAD_PAD, (h + 1) * HEAD_PAD)
        q_ref[:, sl] = _token_rope(q[:, sl], cos, sin_signed, first) * Q_SCALE


def _attn_prompt_kernel(qt_ref, qn_ref, k_ref, vt_ref, wup_f32, wdown_f32, wo_f32, wpg_f32,
                        o_ref, wup_bf16, wdown_bf16, wo_bf16, wpg_bf16,
                        pa_ref, pb_ref, sa_ref, sb_ref, m_ref, acc_ref, *, tq, hg, nq):
    i = pl.program_id(1)
    for src, dst in ((wup_f32, wup_bf16), (wdown_f32, wdown_bf16), (wo_f32, wo_bf16), (wpg_f32, wpg_bf16)):
        dst[...] = src[...].astype(BF16)

    def scores(j, h):
        return jnp.dot(k_ref[j, :, h * HEAD_PAD:(h + 1) * HEAD_PAD], qt_ref[h * HEAD_PAD:(h + 1) * HEAD_PAD, :],
                       preferred_element_type=F32)

    kchunk = lax.broadcasted_iota(jnp.int32, (tq, tq), 0) >> CHUNK_SHIFT
    qchunk = lax.broadcasted_iota(jnp.int32, (tq, tq), 1) >> CHUNK_SHIFT
    visible = kchunk <= qchunk

    def values(jv, h, p):
        return jnp.dot(vt_ref[jv, h * V_ROWS:(h + 1) * V_ROWS, :], p, preferred_element_type=F32)

    def to_prob(s):
        return jnp.exp2(s).astype(BF16)

    def fixed_shift_step(cur_ref, jv, issue_next):
        for h in range(hg):
            p = cur_ref[h]
            issue_next(h)
            acc_ref[h] += values(jv, h, p)

    def running_max_step(cur_ref, jv, issue_next):
        for h in range(hg):
            s = cur_ref[h]
            m_old = m_ref[h]
            m_new = jnp.maximum(m_old, jnp.max(s, axis=0, keepdims=True))
            alpha = jnp.exp2(m_old - m_new)
            p = jnp.exp2(s - m_new).astype(BF16)
            m_ref[h] = m_new
            issue_next(h)
            acc_ref[h] = alpha * acc_ref[h] + values(jv, h, p)

    def load_diagonal(bank_a, encode):
        for h in range(hg):
            bank_a[h] = encode(jnp.where(visible, scores(i, h), NEG))

    def sweep(step, bank_a, bank_b, encode, pairs_per_trip):
        def block_into(bank_ref, j):
            def issue(h):
                bank_ref[h] = encode(scores(j, h))
            return issue

        def next_diagonal_into_a(h):
            j = jnp.minimum(i + 1, nq - 1)
            s = jnp.dot(k_ref[j, :, h * HEAD_PAD:(h + 1) * HEAD_PAD], qn_ref[h * HEAD_PAD:(h + 1) * HEAD_PAD, :],
                        preferred_element_type=F32)
            bank_a[h] = encode(jnp.where(visible, s, NEG))

        def pair(t):
            step(bank_a, jnp.where(t == 0, i, 2 * t - 1), block_into(bank_b, 2 * t))
            step(bank_b, 2 * t, block_into(bank_a, 2 * t + 1))

        n_pairs = i // 2
        n_trips = n_pairs // pairs_per_trip

        def trip(u, c):
            for r in range(pairs_per_trip):
                pair(pairs_per_trip * u + r)
            return c

        lax.fori_loop(0, n_trips, trip, 0)
        for r in range(pairs_per_trip - 1):
            pl.when(n_pairs - n_trips * pairs_per_trip > r)(
                functools.partial(pair, n_trips * pairs_per_trip + r))

        @pl.when(i % 2 == 0)
        def _():
            step(bank_a, jnp.maximum(i - 1, 0), next_diagonal_into_a)

        @pl.when(i % 2 == 1)
        def _():
            step(bank_a, jnp.where(i == 1, i, i - 2), block_into(bank_b, i - 1))
            step(bank_b, i - 1, next_diagonal_into_a)

    acc_ref[...] = jnp.zeros(acc_ref.shape, F32)
    pl.when(i == 0)(functools.partial(load_diagonal, pa_ref, to_prob))
    sweep(fixed_shift_step, pa_ref, pb_ref, to_prob, pairs_per_trip=2)

    bad = jnp.where(jnp.isfinite(acc_ref[...]), 0.0, 1.0)
    bad = jnp.max(jnp.max(jnp.max(bad, axis=0), axis=0, keepdims=True), axis=1, keepdims=True)

    @pl.when(bad[0, 0] > 0.0)
    def _():
        m_ref[...] = jnp.full(m_ref.shape, NEG, F32)
        acc_ref[...] = jnp.zeros(acc_ref.shape, F32)
        load_diagonal(sa_ref, lambda s: s)
        sweep(running_max_step, sa_ref, sb_ref, lambda s: s, pairs_per_trip=1)
        for h in range(hg):
            pa_ref[h] = to_prob(sa_ref[h])

    for h in range(hg):
        o_ref[h * V_DIM:(h + 1) * V_DIM, :] = acc_ref[h, 0:V_DIM, :] / acc_ref[h, V_DIM:V_DIM + 1, :]


def _attn_sample_kernel(q_ref, cn_ref, krn_ref, ch_ref, krht_ref, wukt_ref, wuvbd_ref, o_ref, *, t, n_past, nbs):
    rows = N_HEADS * t
    qchunk = (n_past + (lax.broadcasted_iota(jnp.int32, (rows, 1), 0) & (t - 1))) >> CHUNK_SHIFT
    seen_hist = (lax.broadcasted_iota(jnp.int32, (rows, n_past), 1) >> CHUNK_SHIFT) <= qchunk
    seen_new = ((n_past + lax.broadcasted_iota(jnp.int32, (rows, t), 1)) >> CHUNK_SHIFT) <= qchunk
    elements = range(nbs)

    def queries(e):
        q = q_ref[e * t:(e + 1) * t, :]
        qlat, qrope = [], []
        for h in range(N_HEADS):
            b0 = h * HEAD_PAD
            qn = q[:, b0:b0 + QK_NOPE].astype(BF16)
            qlat.append(jnp.dot(qn, wukt_ref[h], preferred_element_type=F32))
            qrope.append(q[:, b0 + ROPE_LO:b0 + ROPE_HI])
        return (jnp.concatenate(qlat, axis=0).astype(BF16),
                jnp.concatenate(qrope, axis=0).astype(BF16))

    def keys(e):
        return (ch_ref[e].astype(BF16),
                krht_ref[e].astype(BF16),
                cn_ref[e * t:(e + 1) * t, :].astype(BF16),
                krn_ref[e * t:(e + 1) * t, :].astype(BF16))

    def scores(qs, ks):
        (ql, qr), (chb, krhb, cnb, krnb) = qs, ks
        s_h = (lax.dot_general(ql, chb, NT_DIMS, preferred_element_type=F32)
               + jnp.dot(qr, krhb, preferred_element_type=F32))
        s_n = (lax.dot_general(ql, cnb, NT_DIMS, preferred_element_type=F32)
               + lax.dot_general(qr, krnb, NT_DIMS, preferred_element_type=F32))
        return jnp.where(seen_hist, s_h, NEG), jnp.where(seen_new, s_n, NEG)

    def softmax(s):
        s_h, s_n = s
        m = jnp.maximum(jnp.max(s_h, axis=-1, keepdims=True), jnp.max(s_n, axis=-1, keepdims=True))
        p_h = jnp.exp2(s_h - m)
        p_n = jnp.exp2(s_n - m)
        l = jnp.sum(p_h, axis=-1, keepdims=True) + jnp.sum(p_n, axis=-1, keepdims=True)
        return p_h.astype(BF16), p_n.astype(BF16), l

    def latent_out(p, ks):
        (p_h, p_n, l), (chb, _, cnb, _) = p, ks
        olat = (jnp.dot(p_h, chb, preferred_element_type=F32) + jnp.dot(p_n, cnb, preferred_element_type=F32)) / l
        return jnp.concatenate([olat[h * t:(h + 1) * t] for h in range(N_HEADS)], axis=1).astype(BF16)

    qs = [queries(e) for e in elements]
    ks = [keys(e) for e in elements]
    ss = [scores(qs[e], ks[e]) for e in elements]
    ps = [softmax(ss[e]) for e in elements]
    wide = jnp.concatenate([latent_out(ps[e], ks[e]) for e in elements], axis=0)
    o_ref[...] = jnp.dot(wide, wuvbd_ref[...], preferred_element_type=F32)


def _post_kernel(x_ref, o_ref, u_ref, uprev_ref, pe_ref, chist_ref,
                 wpool_ref, spool_ref, wo_ref, ln1g_ref, ln1b_ref, wup_ref, wdw_ref, bdw_ref, wdown_ref,
                 wpg_ref, wpe_ref, ln2g_ref, ln2b_ref,
                 y_ref, clast_ref,
                 ubuf, convbuf, hbuf, cbuf, *, nseq, seq_len, carry, o_transposed, pos0):
    i = pl.program_id(0)
    tm = nseq * seq_len
    hist_rows = slice(CONV_HALO - (CONV_W - 1), CONV_HALO)

    def init_conv_history():
        cbuf[...] = jnp.zeros_like(cbuf)
        cbuf[:, hist_rows, :] = chist_ref[...]

    if carry:
        pl.when(i == 0)(init_conv_history)
    else:
        init_conv_history()

    def gather(buf, halo, shift, cols):
        stride = halo + seq_len
        pieces = [buf[q * stride + halo - shift:q * stride + halo - shift + seq_len, cols] for q in range(nseq)]
        return pieces[0] if nseq == 1 else jnp.concatenate(pieces, axis=0)

    attn = o_ref[...].T if o_transposed else o_ref[...]
    mix_attn = jnp.dot(attn.astype(BF16), wo_ref[0:MLA_WIDTH, :], preferred_element_type=F32)
    pe_proj = jnp.dot(pe_ref[...].astype(BF16), wpe_ref[...], preferred_element_type=F32)

    u = u_ref[...]
    for q in range(nseq):
        uprev = uprev_ref[q]
        if carry:
            uprev = jnp.where(i > 0, uprev, 0.0)
        ubuf[q * (HALO + seq_len):q * (HALO + seq_len) + HALO, :] = uprev
        ubuf[q * (HALO + seq_len) + HALO:(q + 1) * (HALO + seq_len), :] = u[q * seq_len:(q + 1) * seq_len]
    row = lax.broadcasted_iota(jnp.int32, (tm, POOL_GROUP_W), 0)
    frame = row + i * tm if carry else row & (seq_len - 1)
    pooled = []
    for g, w in enumerate(POOL_WINDOWS):
        cols = slice(g * POOL_GROUP_W, (g + 1) * POOL_GROUP_W)
        win = u[:, cols]
        for k in range(1, w):
            win = win + gather(ubuf, HALO, k, cols)
        cnt = jnp.minimum(w, pos0 + frame + 1).astype(F32)
        d = win / cnt - u[:, cols]
        yg = jnp.dot(d.astype(BF16), wpool_ref[g], preferred_element_type=F32) * spool_ref[:, cols]
        pooled.append(yg.astype(BF16))
    pooled = jnp.concatenate(pooled, axis=1)

    mix = mix_attn + jnp.dot(pooled, wo_ref[MLA_WIDTH:D_MODEL, :], preferred_element_type=F32)
    x1 = _layer_norm(ALPHA * x_ref[...] + mix, ln1g_ref[...], ln1b_ref[...])
    x1b = x1.astype(BF16)

    def conv_chunk(cols, buf):
        up = jnp.dot(x1b, wup_ref[:, cols], preferred_element_type=F32)
        if nseq == 1:
            hist = cbuf[0, :, cols]
            cbuf[0, :, cols] = up[tm - CONV_HALO:tm]
            top = lax.broadcasted_iota(jnp.int32, (CONV_HALO, FF_CHUNK), 0)

            def shifted(k):
                rolled = pltpu.roll(up, k, 0)
                head = jnp.where(top < k, pltpu.roll(hist, k, 0), rolled[0:CONV_HALO])
                return jnp.concatenate([head, rolled[CONV_HALO:]], axis=0)

            return (shifted(2) * wdw_ref[0:1, cols] + shifted(1) * wdw_ref[1:2, cols]
                    + up * wdw_ref[2:3, cols] + bdw_ref[:, cols])
        stride = CONV_HALO + seq_len
        for q in range(nseq):
            buf[q * stride:q * stride + CONV_HALO, :] = cbuf[q, :, cols]
            buf[q * stride + CONV_HALO:(q + 1) * stride, :] = up[q * seq_len:(q + 1) * seq_len]
            cbuf[q, :, cols] = buf[q * stride + seq_len:(q + 1) * stride, :]
        everything = slice(None)
        return (gather(buf, CONV_HALO, 2, everything) * wdw_ref[0:1, cols]
                + gather(buf, CONV_HALO, 1, everything) * wdw_ref[1:2, cols]
                + up * wdw_ref[2:3, cols] + bdw_ref[:, cols])

    for j in range(D_FF // FF_CHUNK):
        a = conv_chunk(slice(j * FF_CHUNK, (j + 1) * FF_CHUNK), convbuf.at[2 * j])
        b = conv_chunk(slice(D_FF + j * FF_CHUNK, D_FF + (j + 1) * FF_CHUNK), convbuf.at[2 * j + 1])
        hbuf[:, j * FF_CHUNK:(j + 1) * FF_CHUNK] = (a * jax.nn.sigmoid(a) * b).astype(BF16)
    ffn = jnp.dot(hbuf[...], wdown_ref[...], preferred_element_type=F32)
    clast_ref[...] = cbuf[...]

    ple = jax.nn.sigmoid(jnp.dot(x1b, wpg_ref[...], preferred_element_type=F32)) * pe_proj
    y_ref[...] = _layer_norm(ALPHA * x1 + ffn + ple, ln2g_ref[...], ln2b_ref[...])


def _const_spec(shape):
    nd = len(shape)
    return pl.BlockSpec(shape, lambda *_: (0,) * nd)


def _params(semantics):
    return pltpu.CompilerParams(dimension_semantics=semantics, vmem_limit_bytes=VMEM_LIMIT)


def _pad_heads(w, per_head, used):
    k = w.shape[0]
    w = w.reshape(k, N_HEADS, per_head)[:, :, :used]
    return jnp.pad(w, ((0, 0), (0, 0), (0, HEAD_PAD - used))).reshape(k, N_HEADS * HEAD_PAD)


def _prep_weights(w_in, w_q_b, w_kv_b, w_pool, w_pe):
    w_kr = jnp.pad(w_in[:, 512:512 + QK_ROPE], ((0, 0), (ROPE_LO, LANES - ROPE_HI)))
    win = jnp.concatenate([w_in[:, :512], w_kr, w_in[:, 512 + QK_ROPE:]], axis=1).astype(BF16)
    wq_pad = _pad_heads(w_q_b, QK_NOPE + QK_ROPE, QK_NOPE + QK_ROPE).astype(BF16)
    w_kv = w_kv_b.reshape(KV_LORA, N_HEADS, QK_NOPE + V_DIM)
    w_uk, w_uv = w_kv[..., :QK_NOPE], w_kv[..., QK_NOPE:]
    wuk_pad = jnp.pad(w_uk, ((0, 0), (0, 0), (0, HEAD_PAD - QK_NOPE))).reshape(KV_LORA, -1).astype(BF16)
    wuvt = w_uv.reshape(KV_LORA, N_HEADS * V_DIM).T.astype(BF16)
    wukt = jnp.transpose(w_uk, (1, 2, 0)).astype(BF16)
    eye = jnp.eye(N_HEADS, dtype=w_uv.dtype)
    wuv_bd = jnp.einsum('lhv,hg->hlgv', w_uv, eye).reshape(N_HEADS * KV_LORA, N_HEADS * V_DIM).astype(BF16)
    return dict(win=win, wq_pad=wq_pad, wqt=wq_pad.T, wuk_pad=wuk_pad, wuvt=wuvt, wukt=wukt, wuv_bd=wuv_bd,
                wpool=w_pool.astype(BF16), wpe=w_pe.astype(BF16))


def _rope_inv():
    inv = 1.0 / (ROPE_THETA ** (jnp.arange(0, QK_ROPE, 2, dtype=F32) / QK_ROPE))
    inv_lane = jnp.zeros((1, LANES), F32).at[0, ROPE_LO:ROPE_HI].set(jnp.concatenate([inv, inv]))
    return inv_lane, inv[:, None]


def _project_prompt(x, wb, g_q, g_kv, tm, tk):
    s = x.shape[0]
    _, inv_sub = _rope_inv()
    n = s // tm
    outs = pl.pallas_call(
        functools.partial(_proj_prompt_kernel, tm=tm, tk=tk, pos0=0),
        grid=(n,),
        in_specs=[pl.BlockSpec((tm, D_MODEL), lambda i: (i, 0)),
                  _const_spec(wb['win'].shape), _const_spec((1, Q_LORA)), _const_spec((1, KV_LORA)),
                  _const_spec(wb['wqt'].shape), _const_spec(wb['wuk_pad'].shape), _const_spec(wb['wuvt'].shape),
                  _const_spec((QK_ROPE // 2, 1))],
        out_specs=[pl.BlockSpec((N_HEADS * HEAD_PAD, tm), lambda i: (0, i)),
                   pl.BlockSpec((tm, N_HEADS * HEAD_PAD), lambda i: (i, 0)),
                   pl.BlockSpec((tm // tk, N_HEADS * V_ROWS, tk), lambda i: (i, 0, 0)),
                   pl.BlockSpec((tm, KV_LORA), lambda i: (i, 0)),
                   pl.BlockSpec((QK_ROPE, tm), lambda i: (0, i)),
                   pl.BlockSpec((tm, POOL_WIDTH), lambda i: (i, 0))],
        out_shape=[jax.ShapeDtypeStruct((N_HEADS * HEAD_PAD, s), BF16),
                   jax.ShapeDtypeStruct((s, N_HEADS * HEAD_PAD), BF16),
                   jax.ShapeDtypeStruct((s // tk, N_HEADS * V_ROWS, tk), BF16),
                   jax.ShapeDtypeStruct((s, KV_LORA), F32),
                   jax.ShapeDtypeStruct((QK_ROPE, s), F32),
                   jax.ShapeDtypeStruct((s, POOL_WIDTH), F32)],
        compiler_params=_params(("arbitrary",)),
        name="proj_prompt",
    )(x, wb['win'], g_q.reshape(1, -1), g_kv.reshape(1, -1), wb['wqt'], wb['wuk_pad'], wb['wuvt'], inv_sub)
    return outs


def _project_sample(x, wb, g_q, g_kv, seq_len, pos0):
    rows = x.shape[0]
    assert seq_len & (seq_len - 1) == 0
    inv_lane, _ = _rope_inv()
    return pl.pallas_call(
        functools.partial(_proj_sample_kernel, tm=rows, seq_len=seq_len, pos0=pos0),
        grid=(1,),
        in_specs=[_const_spec((rows, D_MODEL)), _const_spec(wb['win'].shape), _const_spec((1, Q_LORA)),
                  _const_spec((1, KV_LORA)), _const_spec(wb['wq_pad'].shape), _const_spec((1, LANES))],
        out_specs=[_const_spec((rows, N_HEADS * HEAD_PAD)), _const_spec((rows, KV_LORA)),
                   _const_spec((rows, QK_ROPE)), _const_spec((rows, POOL_WIDTH))],
        out_shape=[jax.ShapeDtypeStruct((rows, N_HEADS * HEAD_PAD), F32),
                   jax.ShapeDtypeStruct((rows, KV_LORA), F32),
                   jax.ShapeDtypeStruct((rows, QK_ROPE), F32),
                   jax.ShapeDtypeStruct((rows, POOL_WIDTH), F32)],
        compiler_params=_params(("arbitrary",)),
        name="proj_sample",
    )(x, wb['win'], g_q.reshape(1, -1), g_kv.reshape(1, -1), wb['wq_pad'], inv_lane)


def _attend_prompt(qt, k, vt, tq, tk, hg, w_up, w_down, w_o, w_pg):
    s = k.shape[0]
    nkv = s // tk
    k3 = k.reshape(nkv, tk, N_HEADS * HEAD_PAD)
    resident = pl.Buffered(1)
    assert tq == tk
    nq = s // tq
    casts = [w_up, w_down, w_o, w_pg]
    steps = (N_HEADS // hg) * nq

    def slab(w):
        rows = w.shape[0]
        per = next(r for r in range(16 * pl.cdiv(rows, 16 * steps), rows + 1, 16) if rows % r == 0)
        return pl.BlockSpec((per, w.shape[1]), lambda g, i: (jnp.minimum(g * nq + i, rows // per - 1), 0))

    outs = pl.pallas_call(
        functools.partial(_attn_prompt_kernel, tq=tq, hg=hg, nq=nq),
        grid=(N_HEADS // hg, nq),
        in_specs=[pl.BlockSpec((hg * HEAD_PAD, tq), lambda g, i: (g, i)),
                  pl.BlockSpec((hg * HEAD_PAD, tq), lambda g, i: (g, jnp.minimum(i + 1, nq - 1))),
                  pl.BlockSpec((nkv, tk, hg * HEAD_PAD), lambda g, i: (0, 0, g), pipeline_mode=resident),
                  pl.BlockSpec((nkv, hg * V_ROWS, tk), lambda g, i: (0, g, 0), pipeline_mode=resident)]
                 + [slab(w) for w in casts],
        out_specs=[pl.BlockSpec((hg * V_DIM, tq), lambda g, i: (g, i))] + [slab(w) for w in casts],
        out_shape=[jax.ShapeDtypeStruct((MLA_WIDTH, s), F32)] + [jax.ShapeDtypeStruct(w.shape, BF16) for w in casts],
        scratch_shapes=[pltpu.VMEM((hg, tk, tq), BF16), pltpu.VMEM((hg, tk, tq), BF16),
                        pltpu.VMEM((hg, tk, tq), F32), pltpu.VMEM((hg, tk, tq), F32),
                        pltpu.VMEM((hg, 1, tq), F32), pltpu.VMEM((hg, V_ROWS, tq), F32)],
        compiler_params=_params(("arbitrary", "arbitrary")),
        name="attn_prompt",
    )(qt, qt, k3, vt, *casts)
    return outs[0], dict(zip(('wup', 'wdown', 'wo', 'wpg'), outs[1:]))


def _attend_sample(q, c_new, kr_new, c_hist, kr_hist, wb, t):
    nb, n_past, _ = c_hist.shape
    assert t & (t - 1) == 0
    nbs = SAMPLE_ATTN_GROUP if nb % SAMPLE_ATTN_GROUP == 0 else 1
    return pl.pallas_call(
        functools.partial(_attn_sample_kernel, t=t, n_past=n_past, nbs=nbs),
        grid=(nb // nbs,),
        in_specs=[pl.BlockSpec((nbs * t, N_HEADS * HEAD_PAD), lambda b: (b, 0)),
                  pl.BlockSpec((nbs * t, KV_LORA), lambda b: (b, 0)),
                  pl.BlockSpec((nbs * t, QK_ROPE), lambda b: (b, 0)),
                  pl.BlockSpec((nbs, n_past, KV_LORA), lambda b: (b, 0, 0)),
                  pl.BlockSpec((nbs, QK_ROPE, n_past), lambda b: (b, 0, 0)),
                  _const_spec(wb['wukt'].shape), _const_spec(wb['wuv_bd'].shape)],
        out_specs=pl.BlockSpec((nbs * t, MLA_WIDTH), lambda b: (b, 0)),
        out_shape=jax.ShapeDtypeStruct((nb * t, MLA_WIDTH), F32),
        compiler_params=_params(("arbitrary",)),
        name="attn_sample",
    )(q, c_new, kr_new, c_hist, jnp.swapaxes(kr_hist, 1, 2), wb['wukt'], wb['wuv_bd'])


def _post(x, o, u, uprev, uprev_map, pe, chist, wb, small, *, nseq, seq_len, carry, o_transposed, pos0):
    rows = x.shape[0]
    tm = nseq * seq_len
    n = rows // tm
    assert nseq == 1 if carry else seq_len & (seq_len - 1) == 0
    s_pool, ln1_g, ln1_b, w_dw, b_dw, ln2_g, ln2_b = small
    o_spec = (pl.BlockSpec((MLA_WIDTH, tm), lambda i: (0, i)) if o_transposed
              else pl.BlockSpec((tm, MLA_WIDTH), lambda i: (i, 0)))
    per_tile = (lambda i: (0, 0, 0)) if carry else (lambda i: (i, 0, 0))
    row = lambda v: v.reshape(1, -1)
    return pl.pallas_call(
        functools.partial(_post_kernel, nseq=nseq, seq_len=seq_len, carry=carry, o_transposed=o_transposed,
                          pos0=pos0),
        grid=(n,),
        in_specs=[pl.BlockSpec((tm, D_MODEL), lambda i: (i, 0)),
                  o_spec,
                  pl.BlockSpec((tm, POOL_WIDTH), lambda i: (i, 0)),
                  pl.BlockSpec((nseq, HALO, POOL_WIDTH), uprev_map),
                  pl.BlockSpec((tm, PLE_DIM), lambda i: (i, 0)),
                  pl.BlockSpec((nseq, CONV_W - 1, 2 * D_FF), per_tile),
                  _const_spec(wb['wpool'].shape), _const_spec((1, POOL_WIDTH)), _const_spec(wb['wo'].shape),
                  _const_spec((1, D_MODEL)), _const_spec((1, D_MODEL)),
                  _const_spec(wb['wup'].shape), _const_spec((CONV_W, 2 * D_FF)), _const_spec((1, 2 * D_FF)),
                  _const_spec(wb['wdown'].shape), _const_spec(wb['wpg'].shape), _const_spec(wb['wpe'].shape),
                  _const_spec((1, D_MODEL)), _const_spec((1, D_MODEL))],
        out_specs=[pl.BlockSpec((tm, D_MODEL), lambda i: (i, 0)),
                   pl.BlockSpec((nseq, CONV_HALO, 2 * D_FF), per_tile)],
        out_shape=[jax.ShapeDtypeStruct((rows, D_MODEL), F32),
                   jax.ShapeDtypeStruct((nseq if carry else n * nseq, CONV_HALO, 2 * D_FF), F32)],
        scratch_shapes=[pltpu.VMEM((nseq * (seq_len + HALO), POOL_WIDTH), F32),
                        pltpu.VMEM((2 * (D_FF // FF_CHUNK), nseq * (seq_len + CONV_HALO) if nseq > 1 else CONV_HALO,
                                    FF_CHUNK), F32),
                        pltpu.VMEM((tm, D_FF), BF16),
                        pltpu.VMEM((nseq, CONV_HALO, 2 * D_FF), F32)],
        compiler_params=_params(("arbitrary",)),
        name="post_prompt" if carry else "post_sample",
    )(x, o, u, uprev, pe, chist,
      wb['wpool'], row(s_pool), wb['wo'], row(ln1_g), row(ln1_b), wb['wup'], w_dw, row(b_dw), wb['wdown'],
      wb['wpg'], wb['wpe'], row(ln2_g), row(ln2_b))


def _layer_prompt(x, pe, wb, big_weights, g_q, g_kv, small, *, tm_proj, tq, tk, hg, tm_post):
    s = x.shape[0]
    assert tm_proj == tq
    qt, k, vt, c, kr, u = _project_prompt(x, wb, g_q, g_kv, tm_proj, tk)
    ot, cast = _attend_prompt(qt, k, vt, tq, tk, hg, *big_weights)
    wb = {**wb, **cast}
    per = tm_post // HALO
    y, clast = _post(x, ot, u, u.reshape(s // HALO, HALO, POOL_WIDTH),
                     lambda i: (jnp.maximum(i * per - 1, 0), 0, 0),
                     pe, jnp.zeros((1, CONV_W - 1, 2 * D_FF), F32),
                     wb, small, nseq=1, seq_len=tm_post, carry=True, o_transposed=True, pos0=0)
    return (y, c, kr.T, u[s - POOL_HIST:], clast[0, CONV_HALO - (CONV_W - 1):]), wb


def _layer_sample(x, pe, c_hist, kr_hist, pool_hist, conv_hist, wb, g_q, g_kv, small):
    nb, t, _ = x.shape
    n_past = c_hist.shape[1]
    xf = x.reshape(nb * t, D_MODEL)
    q, c, kr, u = _project_sample(xf, wb, g_q, g_kv, t, n_past)
    o = _attend_sample(q, c, kr, c_hist, kr_hist, wb, t)
    uprev = jnp.pad(pool_hist, ((0, 0), (HALO - POOL_HIST, 0), (0, 0)))
    y, clast = _post(xf, o, u, uprev, lambda i: (i, 0, 0), pe.reshape(nb * t, PLE_DIM), conv_hist,
                     wb, small, nseq=nb, seq_len=t, carry=False, o_transposed=False, pos0=n_past)
    u3 = u.reshape(nb, t, POOL_WIDTH)
    new_pool = jnp.concatenate([pool_hist, u3], axis=1)[:, -POOL_HIST:]
    return (y.reshape(nb, t, D_MODEL), c.reshape(nb, t, KV_LORA), kr.reshape(nb, t, QK_ROPE), new_pool,
            clast[:, CONV_HALO - (CONV_W - 1):])


def kernel(x_prompt, x_sample, cache_ckv, cache_krope, state_pool, state_ffn_conv, p_prompt, p_sample,
           w_in, g_q, w_q_b, g_kv, w_kv_b, w_pool, s_pool, w_o, ln1_g, ln1_b,
           w_up, w_dw, b_dw, w_down, w_pg, w_pe, ln2_g, ln2_b):
    assert x_prompt.shape[0] == 1 and w_in.shape[0] == DEPTH
    wb = _prep_weights(w_in[0], w_q_b[0], w_kv_b[0], w_pool[0], w_pe[0])
    small = (s_pool[0], ln1_g[0], ln1_b[0], w_dw[0], b_dw[0], ln2_g[0], ln2_b[0])
    s = x_prompt.shape[1]
    tm_proj = min(512, s)
    tq = tk = min(512, s)
    (yp, cp, krp, poolp, convp), wb = _layer_prompt(
        x_prompt[0], p_prompt[0, 0], wb, (w_up[0], w_down[0], w_o[0], w_pg[0]), g_q[0], g_kv[0], small,
        tm_proj=tm_proj, tq=tq, tk=tk, hg=ATTN_HEAD_GROUP, tm_post=min(512, s))
    ys, cs, krs, pools, convs = _layer_sample(x_sample, p_sample[0], cache_ckv[0], cache_krope[0],
                                              state_pool[0], state_ffn_conv[0], wb, g_q[0], g_kv[0], small)
    return (yp[None], ys, cp[None, None], krp[None, None], poolp[None, None], convp[None, None],
            cs[None], krs[None], pools[None], convs[None])
```

```python
import functools
import math

import jax
import jax.numpy as jnp
from jax import lax
from jax.experimental import pallas as pl
from jax.experimental.pallas import tpu as pltpu

F32 = jnp.float32
BF16 = jnp.bfloat16

D_MODEL = 1024
CHUNK = 64
CHUNK_SHIFT = 6
ATTN_HEAD_GROUP = 4
SAMPLE_ATTN_GROUP = 2
N_HEADS = 8
QK_NOPE = 64
QK_ROPE = 32
V_DIM = 64
V_ROWS = V_DIM + 16
Q_LORA = 256
KV_LORA = 256
ROPE_THETA = 10000.0
MLA_WIDTH = N_HEADS * V_DIM
POOL_WINDOWS = (2, 4, 8, 16)
POOL_GROUP_W = 128
POOL_WIDTH = D_MODEL - MLA_WIDTH
POOL_HIST = max(POOL_WINDOWS) - 1
D_FF = 2816
CONV_W = 3
PLE_DIM = 256
DEPTH = 1
ALPHA = (2 * DEPTH) ** 0.25
LN_EPS = 1e-5
RMS_EPS = 1e-6
NEG = -1e30
ATTN_SCALE = 1.0 / math.sqrt(QK_NOPE + QK_ROPE)

LANES = 128
HEAD_PAD = 128
ROPE_LO = QK_NOPE
ROPE_MID = QK_NOPE + QK_ROPE // 2
ROPE_HI = QK_NOPE + QK_ROPE
SHIFT_LANE = ROPE_HI
HALO = 16
CONV_HALO = 8
FF_CHUNK = 256
Q_SCALE = ATTN_SCALE * math.log2(math.e)
VMEM_LIMIT = 56 * 1024 * 1024

NT_DIMS = (((1,), (1,)), ((), ()))


def _rms(x, g):
    return x * lax.rsqrt(jnp.mean(x * x, axis=-1, keepdims=True) + RMS_EPS) * g


def _layer_norm(x, g, b):
    mu = jnp.mean(x, axis=-1, keepdims=True)
    xc = x - mu
    var = jnp.mean(xc * xc, axis=-1, keepdims=True)
    return xc * lax.rsqrt(var + LN_EPS) * g + b


def _token_rope_tables(pos_col, inv_lane):
    ang = pos_col.astype(F32) * inv_lane
    lane = lax.broadcasted_iota(jnp.int32, ang.shape, 1)
    first = lane < ROPE_MID
    sin = jnp.sin(ang)
    return jnp.cos(ang), jnp.where(first, -sin, sin), first


def _token_rope(x, cos, sin_signed, first):
    partner = jnp.where(first, pltpu.roll(x, LANES - QK_ROPE // 2, 1), pltpu.roll(x, QK_ROPE // 2, 1))
    return x * cos + partner * sin_signed


def _proj_prompt_kernel(x_ref, win_ref, gq_ref, gkv_ref, wqt_ref, wuk_ref, wuvt_ref, invs_ref,
                        qt_ref, k_ref, vt_ref, c_ref, kr_ref, u_ref, *, tm, tk, pos0):
    i = pl.program_id(0)
    z = jnp.dot(x_ref[...].astype(BF16), win_ref[...], preferred_element_type=F32)
    q_a = z[:, 0:Q_LORA]
    c_raw = z[:, Q_LORA:Q_LORA + KV_LORA]
    kr_pad = z[:, 512:640]
    u_ref[...] = z[:, 640:640 + POOL_WIDTH]

    qn = _rms(q_a, gq_ref[...]).astype(BF16)
    c = _rms(c_raw, gkv_ref[...])
    c_ref[...] = c
    cb = c.astype(BF16)

    pos_row = pos0 + i * tm + lax.broadcasted_iota(jnp.int32, (QK_ROPE // 2, tm), 1)
    ang = pos_row.astype(F32) * invs_ref[...]
    cos_t, sin_t = jnp.cos(ang), jnp.sin(ang)
    lo, hi = (ROPE_LO, tm), (HEAD_PAD - ROPE_HI, tm)
    cos = jnp.concatenate([jnp.ones(lo, F32), cos_t, cos_t, jnp.ones(hi, F32)], axis=0).T
    sin_signed = jnp.concatenate([jnp.zeros(lo, F32), -sin_t, sin_t, jnp.zeros(hi, F32)], axis=0).T
    first = lax.broadcasted_iota(jnp.int32, (tm, LANES), 1) < ROPE_MID
    kr_rot = _token_rope(kr_pad, cos, sin_signed, first)
    kr_ref[...] = kr_rot.T[ROPE_LO:ROPE_HI, :]

    k_nope = jnp.dot(cb, wuk_ref[...], preferred_element_type=F32)
    k_tail = kr_rot + jnp.where(lax.broadcasted_iota(jnp.int32, (tm, LANES), 1) == SHIFT_LANE, 1.0, 0.0)
    for h in range(N_HEADS):
        sl = slice(h * HEAD_PAD, (h + 1) * HEAD_PAD)
        k_ref[:, sl] = (k_nope[:, sl] + k_tail).astype(BF16)

    vt = lax.dot_general(wuvt_ref[...], cb, NT_DIMS, preferred_element_type=F32)
    ones = jnp.ones((V_ROWS - V_DIM, tk), BF16)
    for s in range(tm // tk):
        for h in range(N_HEADS):
            vt_ref[s, h * V_ROWS:h * V_ROWS + V_DIM, :] = vt[h * V_DIM:(h + 1) * V_DIM, s * tk:(s + 1) * tk].astype(BF16)
            vt_ref[s, h * V_ROWS + V_DIM:(h + 1) * V_ROWS, :] = ones

    qt = lax.dot_general(wqt_ref[...], qn, NT_DIMS, preferred_element_type=F32)
    shift_row = lax.broadcasted_iota(jnp.int32, (HEAD_PAD - SHIFT_LANE, tm), 0) == 0
    for h in range(N_HEADS):
        b0 = h * HEAD_PAD
        qt_ref[b0:b0 + ROPE_LO, :] = (qt[b0:b0 + ROPE_LO] * Q_SCALE).astype(BF16)
        x1 = qt[b0 + ROPE_LO:b0 + ROPE_MID]
        x2 = qt[b0 + ROPE_MID:b0 + ROPE_HI]
        qt_ref[b0 + ROPE_LO:b0 + ROPE_MID, :] = ((x1 * cos_t - x2 * sin_t) * Q_SCALE).astype(BF16)
        qt_ref[b0 + ROPE_MID:b0 + ROPE_HI, :] = ((x2 * cos_t + x1 * sin_t) * Q_SCALE).astype(BF16)
        qt_ref[b0 + ROPE_HI:b0 + HEAD_PAD, :] = jnp.zeros((HEAD_PAD - ROPE_HI, tm), BF16)
        lead = jnp.dot(k_ref[0:CHUNK, b0:b0 + HEAD_PAD], qt_ref[b0:b0 + HEAD_PAD, :],
                       preferred_element_type=F32)
        shift = jnp.max(lead, axis=0, keepdims=True)
        qt_ref[b0 + SHIFT_LANE:b0 + HEAD_PAD, :] = jnp.where(shift_row, -shift, 0.0).astype(BF16)


def _proj_sample_kernel(x_ref, win_ref, gq_ref, gkv_ref, wq_ref, invl_ref,
                        q_ref, c_ref, kr_ref, u_ref, *, tm, seq_len, pos0):
    z = jnp.dot(x_ref[...].astype(BF16), win_ref[...], preferred_element_type=F32)
    q_a = z[:, 0:Q_LORA]
    c_raw = z[:, Q_LORA:Q_LORA + KV_LORA]
    kr_pad = z[:, 512:640]
    u_ref[...] = z[:, 640:640 + POOL_WIDTH]

    qn = _rms(q_a, gq_ref[...]).astype(BF16)
    c_ref[...] = _rms(c_raw, gkv_ref[...])

    row = lax.broadcasted_iota(jnp.int32, (tm, LANES), 0)
    pos_col = pos0 + (row & (seq_len - 1))
    cos, sin_signed, first = _token_rope_tables(pos_col, invl_ref[...])
    kr_ref[...] = _token_rope(kr_pad, cos, sin_signed, first)[:, ROPE_LO:ROPE_HI]

    q = jnp.dot(qn, wq_ref[...], preferred_element_type=F32)
    for h in range(N_HEADS):
        sl = slice(h * HEAD_PAD, (h + 1) * HEAD_PAD)
        q_ref[:, sl] = _token_rope(q[:, sl], cos, sin_signed, first) * Q_SCALE


def _attn_prompt_kernel(qt_ref, qn_ref, k_ref, vt_ref, wup_f32, wdown_f32, wo_f32, wpg_f32,
                        o_ref, wup_bf16, wdown_bf16, wo_bf16, wpg_bf16,
                        pa_ref, pb_ref, sa_ref, sb_ref, m_ref, acc_ref, *, tq, hg, nq):
    i = pl.program_id(1)
    for src, dst in ((wup_f32, wup_bf16), (wdown_f32, wdown_bf16), (wo_f32, wo_bf16), (wpg_f32, wpg_bf16)):
        dst[...] = src[...].astype(BF16)

    def scores(j, h):
        return jnp.dot(k_ref[j, :, h * HEAD_PAD:(h + 1) * HEAD_PAD], qt_ref[h * HEAD_PAD:(h + 1) * HEAD_PAD, :],
                       preferred_element_type=F32)

    kchunk = lax.broadcasted_iota(jnp.int32, (tq, tq), 0) >> CHUNK_SHIFT
    qchunk = lax.broadcasted_iota(jnp.int32, (tq, tq), 1) >> CHUNK_SHIFT
    visible = kchunk <= qchunk

    def values(jv, h, p):
        return jnp.dot(vt_ref[jv, h * V_ROWS:(h + 1) * V_ROWS, :], p, preferred_element_type=F32)

    def to_prob(s):
        return jnp.exp2(s).astype(BF16)

    def fixed_shift_step(cur_ref, jv, issue_next):
        for h in range(hg):
            p = cur_ref[h]
            issue_next(h)
            acc_ref[h] += values(jv, h, p)

    def running_max_step(cur_ref, jv, issue_next):
        for h in range(hg):
            s = cur_ref[h]
            m_old = m_ref[h]
            m_new = jnp.maximum(m_old, jnp.max(s, axis=0, keepdims=True))
            alpha = jnp.exp2(m_old - m_new)
            p = jnp.exp2(s - m_new).astype(BF16)
            m_ref[h] = m_new
            issue_next(h)
            acc_ref[h] = alpha * acc_ref[h] + values(jv, h, p)

    def load_diagonal(bank_a, encode):
        for h in range(hg):
            bank_a[h] = encode(jnp.where(visible, scores(i, h), NEG))

    def sweep(step, bank_a, bank_b, encode, pairs_per_trip):
        def block_into(bank_ref, j):
            def issue(h):
                bank_ref[h] = encode(scores(j, h))
            return issue

        def next_diagonal_into_a(h):
            j = jnp.minimum(i + 1, nq - 1)
            s = jnp.dot(k_ref[j, :, h * HEAD_PAD:(h + 1) * HEAD_PAD], qn_ref[h * HEAD_PAD:(h + 1) * HEAD_PAD, :],
                        preferred_element_type=F32)
            bank_a[h] = encode(jnp.where(visible, s, NEG))

        def pair(t):
            step(bank_a, jnp.where(t == 0, i, 2 * t - 1), block_into(bank_b, 2 * t))
            step(bank_b, 2 * t, block_into(bank_a, 2 * t + 1))

        n_pairs = i // 2
        n_trips = n_pairs // pairs_per_trip

        def trip(u, c):
            for r in range(pairs_per_trip):
                pair(pairs_per_trip * u + r)
            return c

        lax.fori_loop(0, n_trips, trip, 0)
        for r in range(pairs_per_trip - 1):
            pl.when(n_pairs - n_trips * pairs_per_trip > r)(
                functools.partial(pair, n_trips * pairs_per_trip + r))

        @pl.when(i % 2 == 0)
        def _():
            step(bank_a, jnp.maximum(i - 1, 0), next_diagonal_into_a)

        @pl.when(i % 2 == 1)
        def _():
            step(bank_a, jnp.where(i == 1, i, i - 2), block_into(bank_b, i - 1))
            step(bank_b, i - 1, next_diagonal_into_a)

    acc_ref[...] = jnp.zeros(acc_ref.shape, F32)
    pl.when(i == 0)(functools.partial(load_diagonal, pa_ref, to_prob))
    sweep(fixed_shift_step, pa_ref, pb_ref, to_prob, pairs_per_trip=2)

    bad = jnp.where(jnp.isfinite(acc_ref[...]), 0.0, 1.0)
    bad = jnp.max(jnp.max(jnp.max(bad, axis=0), axis=0, keepdims=True), axis=1, keepdims=True)

    @pl.when(bad[0, 0] > 0.0)
    def _():
        m_ref[...] = jnp.full(m_ref.shape, NEG, F32)
        acc_ref[...] = jnp.zeros(acc_ref.shape, F32)
        load_diagonal(sa_ref, lambda s: s)
        sweep(running_max_step, sa_ref, sb_ref, lambda s: s, pairs_per_trip=1)
        for h in range(hg):
            pa_ref[h] = to_prob(sa_ref[h])

    for h in range(hg):
        o_ref[h * V_DIM:(h + 1) * V_DIM, :] = acc_ref[h, 0:V_DIM, :] / acc_ref[h, V_DIM:V_DIM + 1, :]


def _attn_sample_kernel(q_ref, cn_ref, krn_ref, ch_ref, krht_ref, wukt_ref, wuvbd_ref, o_ref, *, t, n_past, nbs):
    rows = N_HEADS * t
    qchunk = (n_past + (lax.broadcasted_iota(jnp.int32, (rows, 1), 0) & (t - 1))) >> CHUNK_SHIFT
    seen_hist = (lax.broadcasted_iota(jnp.int32, (rows, n_past), 1) >> CHUNK_SHIFT) <= qchunk
    seen_new = ((n_past + lax.broadcasted_iota(jnp.int32, (rows, t), 1)) >> CHUNK_SHIFT) <= qchunk
    elements = range(nbs)

    def queries(e):
        q = q_ref[e * t:(e + 1) * t, :]
        qlat, qrope = [], []
        for h in range(N_HEADS):
            b0 = h * HEAD_PAD
            qn = q[:, b0:b0 + QK_NOPE].astype(BF16)
            qlat.append(jnp.dot(qn, wukt_ref[h], preferred_element_type=F32))
            qrope.append(q[:, b0 + ROPE_LO:b0 + ROPE_HI])
        return (jnp.concatenate(qlat, axis=0).astype(BF16),
                jnp.concatenate(qrope, axis=0).astype(BF16))

    def keys(e):
        return (ch_ref[e].astype(BF16),
                krht_ref[e].astype(BF16),
                cn_ref[e * t:(e + 1) * t, :].astype(BF16),
                krn_ref[e * t:(e + 1) * t, :].astype(BF16))

    def scores(qs, ks):
        (ql, qr), (chb, krhb, cnb, krnb) = qs, ks
        s_h = (lax.dot_general(ql, chb, NT_DIMS, preferred_element_type=F32)
               + jnp.dot(qr, krhb, preferred_element_type=F32))
        s_n = (lax.dot_general(ql, cnb, NT_DIMS, preferred_element_type=F32)
               + lax.dot_general(qr, krnb, NT_DIMS, preferred_element_type=F32))
        return jnp.where(seen_hist, s_h, NEG), jnp.where(seen_new, s_n, NEG)

    def softmax(s):
        s_h, s_n = s
        m = jnp.maximum(jnp.max(s_h, axis=-1, keepdims=True), jnp.max(s_n, axis=-1, keepdims=True))
        p_h = jnp.exp2(s_h - m)
        p_n = jnp.exp2(s_n - m)
        l = jnp.sum(p_h, axis=-1, keepdims=True) + jnp.sum(p_n, axis=-1, keepdims=True)
        return p_h.astype(BF16), p_n.astype(BF16), l

    def latent_out(p, ks):
        (p_h, p_n, l), (chb, _, cnb, _) = p, ks
        olat = (jnp.dot(p_h, chb, preferred_element_type=F32) + jnp.dot(p_n, cnb, preferred_element_type=F32)) / l
        return jnp.concatenate([olat[h * t:(h + 1) * t] for h in range(N_HEADS)], axis=1).astype(BF16)

    qs = [queries(e) for e in elements]
    ks = [keys(e) for e in elements]
    ss = [scores(qs[e], ks[e]) for e in elements]
    ps = [softmax(ss[e]) for e in elements]
    wide = jnp.concatenate([latent_out(ps[e], ks[e]) for e in elements], axis=0)
    o_ref[...] = jnp.dot(wide, wuvbd_ref[...], preferred_element_type=F32)


def _post_kernel(x_ref, o_ref, u_ref, uprev_ref, pe_ref, chist_ref,
                 wpool_ref, spool_ref, wo_ref, ln1g_ref, ln1b_ref, wup_ref, wdw_ref, bdw_ref, wdown_ref,
                 wpg_ref, wpe_ref, ln2g_ref, ln2b_ref,
                 y_ref, clast_ref,
                 ubuf, convbuf, hbuf, cbuf, *, nseq, seq_len, carry, o_transposed, pos0):
    i = pl.program_id(0)
    tm = nseq * seq_len
    hist_rows = slice(CONV_HALO - (CONV_W - 1), CONV_HALO)

    def init_conv_history():
        cbuf[...] = jnp.zeros_like(cbuf)
        cbuf[:, hist_rows, :] = chist_ref[...]

    if carry:
        pl.when(i == 0)(init_conv_history)
    else:
        init_conv_history()

    def gather(buf, halo, shift, cols):
        stride = halo + seq_len
        pieces = [buf[q * stride + halo - shift:q * stride + halo - shift + seq_len, cols] for q in range(nseq)]
        return pieces[0] if nseq == 1 else jnp.concatenate(pieces, axis=0)

    attn = o_ref[...].T if o_transposed else o_ref[...]
    mix_attn = jnp.dot(attn.astype(BF16), wo_ref[0:MLA_WIDTH, :], preferred_element_type=F32)
    pe_proj = jnp.dot(pe_ref[...].astype(BF16), wpe_ref[...], preferred_element_type=F32)

    u = u_ref[...]
    for q in range(nseq):
        uprev = uprev_ref[q]
        if carry:
            uprev = jnp.where(i > 0, uprev, 0.0)
        ubuf[q * (HALO + seq_len):q * (HALO + seq_len) + HALO, :] = uprev
        ubuf[q * (HALO + seq_len) + HALO:(q + 1) * (HALO + seq_len), :] = u[q * seq_len:(q + 1) * seq_len]
    row = lax.broadcasted_iota(jnp.int32, (tm, POOL_GROUP_W), 0)
    frame = row + i * tm if carry else row & (seq_len - 1)
    pooled = []
    for g, w in enumerate(POOL_WINDOWS):
        cols = slice(g * POOL_GROUP_W, (g + 1) * POOL_GROUP_W)
        win = u[:, cols]
        for k in range(1, w):
            win = win + gather(ubuf, HALO, k, cols)
        cnt = jnp.minimum(w, pos0 + frame + 1).astype(F32)
        d = win / cnt - u[:, cols]
        yg = jnp.dot(d.astype(BF16), wpool_ref[g], preferred_element_type=F32) * spool_ref[:, cols]
        pooled.append(yg.astype(BF16))
    pooled = jnp.concatenate(pooled, axis=1)

    mix = mix_attn + jnp.dot(pooled, wo_ref[MLA_WIDTH:D_MODEL, :], preferred_element_type=F32)
    x1 = _layer_norm(ALPHA * x_ref[...] + mix, ln1g_ref[...], ln1b_ref[...])
    x1b = x1.astype(BF16)

    def conv_chunk(cols, buf):
        up = jnp.dot(x1b, wup_ref[:, cols], preferred_element_type=F32)
        if nseq == 1:
            hist = cbuf[0, :, cols]
            cbuf[0, :, cols] = up[tm - CONV_HALO:tm]
            top = lax.broadcasted_iota(jnp.int32, (CONV_HALO, FF_CHUNK), 0)

            def shifted(k):
                rolled = pltpu.roll(up, k, 0)
                head = jnp.where(top < k, pltpu.roll(hist, k, 0), rolled[0:CONV_HALO])
                return jnp.concatenate([head, rolled[CONV_HALO:]], axis=0)

            return (shifted(2) * wdw_ref[0:1, cols] + shifted(1) * wdw_ref[1:2, cols]
                    + up * wdw_ref[2:3, cols] + bdw_ref[:, cols])
        stride = CONV_HALO + seq_len
        for q in range(nseq):
            buf[q * stride:q * stride + CONV_HALO, :] = cbuf[q, :, cols]
            buf[q * stride + CONV_HALO:(q + 1) * stride, :] = up[q * seq_len:(q + 1) * seq_len]
            cbuf[q, :, cols] = buf[q * stride + seq_len:(q + 1) * stride, :]
        everything = slice(None)
        return (gather(buf, CONV_HALO, 2, everything) * wdw_ref[0:1, cols]
                + gather(buf, CONV_HALO, 1, everything) * wdw_ref[1:2, cols]
                + up * wdw_ref[2:3, cols] + bdw_ref[:, cols])

    ffn, pending = None, None
    for j in range(D_FF // FF_CHUNK + 1):
        if j < D_FF // FF_CHUNK:
            a = conv_chunk(slice(j * FF_CHUNK, (j + 1) * FF_CHUNK), convbuf.at[2 * j])
            b = conv_chunk(slice(D_FF + j * FF_CHUNK, D_FF + (j + 1) * FF_CHUNK), convbuf.at[2 * j + 1])
        if pending is not None:
            part = jnp.dot(pending, wdown_ref[(j - 1) * FF_CHUNK:j * FF_CHUNK, :], preferred_element_type=F32)
            ffn = part if ffn is None else ffn + part
        if j < D_FF // FF_CHUNK:
            pending = (a * jax.nn.sigmoid(a) * b).astype(BF16)
    clast_ref[...] = cbuf[...]

    ple = jax.nn.sigmoid(jnp.dot(x1b, wpg_ref[...], preferred_element_type=F32)) * pe_proj
    y_ref[...] = _layer_norm(ALPHA * x1 + ffn + ple, ln2g_ref[...], ln2b_ref[...])


def _const_spec(shape):
    nd = len(shape)
    return pl.BlockSpec(shape, lambda *_: (0,) * nd)


def _params(semantics):
    return pltpu.CompilerParams(dimension_semantics=semantics, vmem_limit_bytes=VMEM_LIMIT)


def _pad_heads(w, per_head, used):
    k = w.shape[0]
    w = w.reshape(k, N_HEADS, per_head)[:, :, :used]
    return jnp.pad(w, ((0, 0), (0, 0), (0, HEAD_PAD - used))).reshape(k, N_HEADS * HEAD_PAD)


def _prep_weights(w_in, w_q_b, w_kv_b, w_pool, w_pe):
    w_kr = jnp.pad(w_in[:, 512:512 + QK_ROPE], ((0, 0), (ROPE_LO, LANES - ROPE_HI)))
    win = jnp.concatenate([w_in[:, :512], w_kr, w_in[:, 512 + QK_ROPE:]], axis=1).astype(BF16)
    wq_pad = _pad_heads(w_q_b, QK_NOPE + QK_ROPE, QK_NOPE + QK_ROPE).astype(BF16)
    w_kv = w_kv_b.reshape(KV_LORA, N_HEADS, QK_NOPE + V_DIM)
    w_uk, w_uv = w_kv[..., :QK_NOPE], w_kv[..., QK_NOPE:]
    wuk_pad = jnp.pad(w_uk, ((0, 0), (0, 0), (0, HEAD_PAD - QK_NOPE))).reshape(KV_LORA, -1).astype(BF16)
    wuvt = w_uv.reshape(KV_LORA, N_HEADS * V_DIM).T.astype(BF16)
    wukt = jnp.transpose(w_uk, (1, 2, 0)).astype(BF16)
    eye = jnp.eye(N_HEADS, dtype=w_uv.dtype)
    wuv_bd = jnp.einsum('lhv,hg->hlgv', w_uv, eye).reshape(N_HEADS * KV_LORA, N_HEADS * V_DIM).astype(BF16)
    return dict(win=win, wq_pad=wq_pad, wqt=wq_pad.T, wuk_pad=wuk_pad, wuvt=wuvt, wukt=wukt, wuv_bd=wuv_bd,
                wpool=w_pool.astype(BF16), wpe=w_pe.astype(BF16))


def _rope_inv():
    inv = 1.0 / (ROPE_THETA ** (jnp.arange(0, QK_ROPE, 2, dtype=F32) / QK_ROPE))
    inv_lane = jnp.zeros((1, LANES), F32).at[0, ROPE_LO:ROPE_HI].set(jnp.concatenate([inv, inv]))
    return inv_lane, inv[:, None]


def _project_prompt(x, wb, g_q, g_kv, tm, tk):
    s = x.shape[0]
    _, inv_sub = _rope_inv()
    n = s // tm
    outs = pl.pallas_call(
        functools.partial(_proj_prompt_kernel, tm=tm, tk=tk, pos0=0),
        grid=(n,),
        in_specs=[pl.BlockSpec((tm, D_MODEL), lambda i: (i, 0)),
                  _const_spec(wb['win'].shape), _const_spec((1, Q_LORA)), _const_spec((1, KV_LORA)),
                  _const_spec(wb['wqt'].shape), _const_spec(wb['wuk_pad'].shape), _const_spec(wb['wuvt'].shape),
                  _const_spec((QK_ROPE // 2, 1))],
        out_specs=[pl.BlockSpec((N_HEADS * HEAD_PAD, tm), lambda i: (0, i)),
                   pl.BlockSpec((tm, N_HEADS * HEAD_PAD), lambda i: (i, 0)),
                   pl.BlockSpec((tm // tk, N_HEADS * V_ROWS, tk), lambda i: (i, 0, 0)),
                   pl.BlockSpec((tm, KV_LORA), lambda i: (i, 0)),
                   pl.BlockSpec((QK_ROPE, tm), lambda i: (0, i)),
                   pl.BlockSpec((tm, POOL_WIDTH), lambda i: (i, 0))],
        out_shape=[jax.ShapeDtypeStruct((N_HEADS * HEAD_PAD, s), BF16),
                   jax.ShapeDtypeStruct((s, N_HEADS * HEAD_PAD), BF16),
                   jax.ShapeDtypeStruct((s // tk, N_HEADS * V_ROWS, tk), BF16),
                   jax.ShapeDtypeStruct((s, KV_LORA), F32),
                   jax.ShapeDtypeStruct((QK_ROPE, s), F32),
                   jax.ShapeDtypeStruct((s, POOL_WIDTH), F32)],
        compiler_params=_params(("arbitrary",)),
        name="proj_prompt",
    )(x, wb['win'], g_q.reshape(1, -1), g_kv.reshape(1, -1), wb['wqt'], wb['wuk_pad'], wb['wuvt'], inv_sub)
    return outs


def _project_sample(x, wb, g_q, g_kv, seq_len, pos0):
    rows = x.shape[0]
    assert seq_len & (seq_len - 1) == 0
    inv_lane, _ = _rope_inv()
    return pl.pallas_call(
        functools.partial(_proj_sample_kernel, tm=rows, seq_len=seq_len, pos0=pos0),
        grid=(1,),
        in_specs=[_const_spec((rows, D_MODEL)), _const_spec(wb['win'].shape), _const_spec((1, Q_LORA)),
                  _const_spec((1, KV_LORA)), _const_spec(wb['wq_pad'].shape), _const_spec((1, LANES))],
        out_specs=[_const_spec((rows, N_HEADS * HEAD_PAD)), _const_spec((rows, KV_LORA)),
                   _const_spec((rows, QK_ROPE)), _const_spec((rows, POOL_WIDTH))],
        out_shape=[jax.ShapeDtypeStruct((rows, N_HEADS * HEAD_PAD), F32),
                   jax.ShapeDtypeStruct((rows, KV_LORA), F32),
                   jax.ShapeDtypeStruct((rows, QK_ROPE), F32),
                   jax.ShapeDtypeStruct((rows, POOL_WIDTH), F32)],
        compiler_params=_params(("arbitrary",)),
        name="proj_sample",
    )(x, wb['win'], g_q.reshape(1, -1), g_kv.reshape(1, -1), wb['wq_pad'], inv_lane)


def _attend_prompt(qt, k, vt, tq, tk, hg, w_up, w_down, w_o, w_pg):
    s = k.shape[0]
    nkv = s // tk
    k3 = k.reshape(nkv, tk, N_HEADS * HEAD_PAD)
    resident = pl.Buffered(1)
    assert tq == tk
    nq = s // tq
    casts = [w_up, w_down, w_o, w_pg]
    steps = (N_HEADS // hg) * nq

    def slab(w):
        rows = w.shape[0]
        per = next(r for r in range(16 * pl.cdiv(rows, 16 * steps), rows + 1, 16) if rows % r == 0)
        return pl.BlockSpec((per, w.shape[1]), lambda g, i: (jnp.minimum(g * nq + i, rows // per - 1), 0))

    outs = pl.pallas_call(
        functools.partial(_attn_prompt_kernel, tq=tq, hg=hg, nq=nq),
        grid=(N_HEADS // hg, nq),
        in_specs=[pl.BlockSpec((hg * HEAD_PAD, tq), lambda g, i: (g, i)),
                  pl.BlockSpec((hg * HEAD_PAD, tq), lambda g, i: (g, jnp.minimum(i + 1, nq - 1))),
                  pl.BlockSpec((nkv, tk, hg * HEAD_PAD), lambda g, i: (0, 0, g), pipeline_mode=resident),
                  pl.BlockSpec((nkv, hg * V_ROWS, tk), lambda g, i: (0, g, 0), pipeline_mode=resident)]
                 + [slab(w) for w in casts],
        out_specs=[pl.BlockSpec((hg * V_DIM, tq), lambda g, i: (g, i))] + [slab(w) for w in casts],
        out_shape=[jax.ShapeDtypeStruct((MLA_WIDTH, s), F32)] + [jax.ShapeDtypeStruct(w.shape, BF16) for w in casts],
        scratch_shapes=[pltpu.VMEM((hg, tk, tq), BF16), pltpu.VMEM((hg, tk, tq), BF16),
                        pltpu.VMEM((hg, tk, tq), F32), pltpu.VMEM((hg, tk, tq), F32),
                        pltpu.VMEM((hg, 1, tq), F32), pltpu.VMEM((hg, V_ROWS, tq), F32)],
        compiler_params=_params(("arbitrary", "arbitrary")),
        name="attn_prompt",
    )(qt, qt, k3, vt, *casts)
    return outs[0], dict(zip(('wup', 'wdown', 'wo', 'wpg'), outs[1:]))


def _attend_sample(q, c_new, kr_new, c_hist, kr_hist, wb, t):
    nb, n_past, _ = c_hist.shape
    assert t & (t - 1) == 0
    nbs = SAMPLE_ATTN_GROUP if nb % SAMPLE_ATTN_GROUP == 0 else 1
    return pl.pallas_call(
        functools.partial(_attn_sample_kernel, t=t, n_past=n_past, nbs=nbs),
        grid=(nb // nbs,),
        in_specs=[pl.BlockSpec((nbs * t, N_HEADS * HEAD_PAD), lambda b: (b, 0)),
                  pl.BlockSpec((nbs * t, KV_LORA), lambda b: (b, 0)),
                  pl.BlockSpec((nbs * t, QK_ROPE), lambda b: (b, 0)),
                  pl.BlockSpec((nbs, n_past, KV_LORA), lambda b: (b, 0, 0)),
                  pl.BlockSpec((nbs, QK_ROPE, n_past), lambda b: (b, 0, 0)),
                  _const_spec(wb['wukt'].shape), _const_spec(wb['wuv_bd'].shape)],
        out_specs=pl.BlockSpec((nbs * t, MLA_WIDTH), lambda b: (b, 0)),
        out_shape=jax.ShapeDtypeStruct((nb * t, MLA_WIDTH), F32),
        compiler_params=_params(("arbitrary",)),
        name="attn_sample",
    )(q, c_new, kr_new, c_hist, jnp.swapaxes(kr_hist, 1, 2), wb['wukt'], wb['wuv_bd'])


def _post(x, o, u, uprev, uprev_map, pe, chist, wb, small, *, nseq, seq_len, carry, o_transposed, pos0):
    rows = x.shape[0]
    tm = nseq * seq_len
    n = rows // tm
    assert nseq == 1 if carry else seq_len & (seq_len - 1) == 0
    s_pool, ln1_g, ln1_b, w_dw, b_dw, ln2_g, ln2_b = small
    o_spec = (pl.BlockSpec((MLA_WIDTH, tm), lambda i: (0, i)) if o_transposed
              else pl.BlockSpec((tm, MLA_WIDTH), lambda i: (i, 0)))
    per_tile = (lambda i: (0, 0, 0)) if carry else (lambda i: (i, 0, 0))
    row = lambda v: v.reshape(1, -1)
    return pl.pallas_call(
        functools.partial(_post_kernel, nseq=nseq, seq_len=seq_len, carry=carry, o_transposed=o_transposed,
                          pos0=pos0),
        grid=(n,),
        in_specs=[pl.BlockSpec((tm, D_MODEL), lambda i: (i, 0)),
                  o_spec,
                  pl.BlockSpec((tm, POOL_WIDTH), lambda i: (i, 0)),
                  pl.BlockSpec((nseq, HALO, POOL_WIDTH), uprev_map),
                  pl.BlockSpec((tm, PLE_DIM), lambda i: (i, 0)),
                  pl.BlockSpec((nseq, CONV_W - 1, 2 * D_FF), per_tile),
                  _const_spec(wb['wpool'].shape), _const_spec((1, POOL_WIDTH)), _const_spec(wb['wo'].shape),
                  _const_spec((1, D_MODEL)), _const_spec((1, D_MODEL)),
                  _const_spec(wb['wup'].shape), _const_spec((CONV_W, 2 * D_FF)), _const_spec((1, 2 * D_FF)),
                  _const_spec(wb['wdown'].shape), _const_spec(wb['wpg'].shape), _const_spec(wb['wpe'].shape),
                  _const_spec((1, D_MODEL)), _const_spec((1, D_MODEL))],
        out_specs=[pl.BlockSpec((tm, D_MODEL), lambda i: (i, 0)),
                   pl.BlockSpec((nseq, CONV_HALO, 2 * D_FF), per_tile)],
        out_shape=[jax.ShapeDtypeStruct((rows, D_MODEL), F32),
                   jax.ShapeDtypeStruct((nseq if carry else n * nseq, CONV_HALO, 2 * D_FF), F32)],
        scratch_shapes=[pltpu.VMEM((nseq * (seq_len + HALO), POOL_WIDTH), F32),
                        pltpu.VMEM((2 * (D_FF // FF_CHUNK), nseq * (seq_len + CONV_HALO) if nseq > 1 else CONV_HALO,
                                    FF_CHUNK), F32),
                        pltpu.VMEM((tm, D_FF), BF16),
                        pltpu.VMEM((nseq, CONV_HALO, 2 * D_FF), F32)],
        compiler_params=_params(("arbitrary",)),
        name="post_prompt" if carry else "post_sample",
    )(x, o, u, uprev, pe, chist,
      wb['wpool'], row(s_pool), wb['wo'], row(ln1_g), row(ln1_b), wb['wup'], w_dw, row(b_dw), wb['wdown'],
      wb['wpg'], wb['wpe'], row(ln2_g), row(ln2_b))


def _layer_prompt(x, pe, wb, big_weights, g_q, g_kv, small, *, tm_proj, tq, tk, hg, tm_post):
    s = x.shape[0]
    assert tm_proj == tq
    qt, k, vt, c, kr, u = _project_prompt(x, wb, g_q, g_kv, tm_proj, tk)
    ot, cast = _attend_prompt(qt, k, vt, tq, tk, hg, *big_weights)
    wb = {**wb, **cast}
    per = tm_post // HALO
    y, clast = _post(x, ot, u, u.reshape(s // HALO, HALO, POOL_WIDTH),
                     lambda i: (jnp.maximum(i * per - 1, 0), 0, 0),
                     pe, jnp.zeros((1, CONV_W - 1, 2 * D_FF), F32),
                     wb, small, nseq=1, seq_len=tm_post, carry=True, o_transposed=True, pos0=0)
    return (y, c, kr.T, u[s - POOL_HIST:], clast[0, CONV_HALO - (CONV_W - 1):]), wb


def _layer_sample(x, pe, c_hist, kr_hist, pool_hist, conv_hist, wb, g_q, g_kv, small):
    nb, t, _ = x.shape
    n_past = c_hist.shape[1]
    xf = x.reshape(nb * t, D_MODEL)
    q, c, kr, u = _project_sample(xf, wb, g_q, g_kv, t, n_past)
    o = _attend_sample(q, c, kr, c_hist, kr_hist, wb, t)
    uprev = jnp.pad(pool_hist, ((0, 0), (HALO - POOL_HIST, 0), (0, 0)))
    y, clast = _post(xf, o, u, uprev, lambda i: (i, 0, 0), pe.reshape(nb * t, PLE_DIM), conv_hist,
                     wb, small, nseq=nb, seq_len=t, carry=False, o_transposed=False, pos0=n_past)
    u3 = u.reshape(nb, t, POOL_WIDTH)
    new_pool = jnp.concatenate([pool_hist, u3], axis=1)[:, -POOL_HIST:]
    return (y.reshape(nb, t, D_MODEL), c.reshape(nb, t, KV_LORA), kr.reshape(nb, t, QK_ROPE), new_pool,
            clast[:, CONV_HALO - (CONV_W - 1):])


def kernel(x_prompt, x_sample, cache_ckv, cache_krope, state_pool, state_ffn_conv, p_prompt, p_sample,
           w_in, g_q, w_q_b, g_kv, w_kv_b, w_pool, s_pool, w_o, ln1_g, ln1_b,
           w_up, w_dw, b_dw, w_down, w_pg, w_pe, ln2_g, ln2_b):
    assert x_prompt.shape[0] == 1 and w_in.shape[0] == DEPTH
    wb = _prep_weights(w_in[0], w_q_b[0], w_kv_b[0], w_pool[0], w_pe[0])
    small = (s_pool[0], ln1_g[0], ln1_b[0], w_dw[0], b_dw[0], ln2_g[0], ln2_b[0])
    s = x_prompt.shape[1]
    tm_proj = min(512, s)
    tq = tk = min(512, s)
    (yp, cp, krp, poolp, convp), wb = _layer_prompt(
        x_prompt[0], p_prompt[0, 0], wb, (w_up[0], w_down[0], w_o[0], w_pg[0]), g_q[0], g_kv[0], small,
        tm_proj=tm_proj, tq=tq, tk=tk, hg=ATTN_HEAD_GROUP, tm_post=min(256, s))
    ys, cs, krs, pools, convs = _layer_sample(x_sample, p_sample[0], cache_ckv[0], cache_krope[0],
                                              state_pool[0], state_ffn_conv[0], wb, g_q[0], g_kv[0], small)
    return (yp[None], ys, cp[None, None], krp[None, None], poolp[None, None], convp[None, None],
            cs[None], krs[None], pools[None], convs[None])
```
